```python
import math
import jax
import jax.numpy as jnp
from jax import lax
import numpy as np

D_MODEL = 1024
BATCH = 2
SEQ = 8192
DEPTH = 4

GRID_W = 64
CTX_LEN = 256
N_BRANCH = 4
BR_W = 384

S5_GROUP = 16
S5_GROUPS = BR_W // S5_GROUP
S5_STATE = 64
S5_DT_MIN = 1e-3
S5_DT_MAX = 1e-1

SGU_CHUNK = 128
SGU_HEADS = 6

SSD_HEADS = 6
SSD_HEAD_DIM = BR_W // SSD_HEADS
SSD_GROUPS = 2
SSD_STATE = 64
SSD_CHUNK = 128
SSD_CONV = 3
SSD_GN = SSD_GROUPS * SSD_STATE
SSD_CONV_CH = BR_W + 2 * SSD_GN
SSD_DT_MIN = 1e-3
SSD_DT_MAX = 1e-1

MLA_HEADS = 6
MLA_NOPE = 64
MLA_ROPE = 32
MLA_V = 64
MLA_QK = MLA_NOPE + MLA_ROPE
MLA_Q_LORA = 384
MLA_KV_LORA = 256
ATTN_BLOCK = 128
ROPE_BASE = 10000.0
ROPE_PAIRS_AXIS = MLA_ROPE // 4

FFN_HIDDEN = int(math.ceil(8 * D_MODEL / 3 / 256)) * 256

IN_SIZES = (BR_W, 2 * BR_W, BR_W + SSD_CONV_CH + 2 * SSD_HEADS, MLA_Q_LORA + MLA_KV_LORA + MLA_ROPE, N_BRANCH * D_MODEL)
IN_W = sum(IN_SIZES)

kernel_name = 'hybrid_s5_sgu_ssd_mla_prefix_dit'


def rms_norm(x, g, eps=1e-6):
    xf = x.astype(jnp.float32)
    y = xf * lax.rsqrt(jnp.mean(xf * xf, axis=-1, keepdims=True) + eps)
    return (y * g.astype(jnp.float32)).astype(x.dtype)


def layer_norm(x, g, b, eps=1e-5):
    xf = x.astype(jnp.float32)
    mu = jnp.mean(xf, axis=-1, keepdims=True)
    xc = xf - mu
    y = xc * lax.rsqrt(jnp.mean(xc * xc, axis=-1, keepdims=True) + eps)
    return (y * g.astype(jnp.float32) + b.astype(jnp.float32)).astype(x.dtype)


def split_cols(p, sizes):
    idx = [int(i) for i in np.cumsum(sizes)[:-1]]
    return jnp.split(p, idx, axis=-1)


def modulate(h, shift, scale):
    return h * (1.0 + scale) + shift


def s5_discretise(a_re, a_im, b_re, b_im, log_dt):
    f32 = jnp.float32
    a_re, a_im, b_re, b_im = a_re.astype(f32), a_im.astype(f32), b_re.astype(f32), b_im.astype(f32)
    dt = jnp.exp(log_dt.astype(f32))[:, None]
    mag = jnp.exp(a_re * dt)
    ang = a_im * dt
    ab_re = mag * jnp.cos(ang)
    ab_im = mag * jnp.sin(ang)
    den = a_re * a_re + a_im * a_im
    f_re = ((ab_re - 1.0) * a_re + ab_im * a_im) / den
    f_im = (ab_im * a_re - (ab_re - 1.0) * a_im) / den
    bb_re = f_re[..., None] * b_re - f_im[..., None] * b_im
    bb_im = f_re[..., None] * b_im + f_im[..., None] * b_re
    return ab_re, ab_im, bb_re, bb_im


def _affine_combine(e1, e2):
    a1r, a1i, b1r, b1i = e1
    a2r, a2i, b2r, b2i = e2
    return (a2r * a1r - a2i * a1i, a2r * a1i + a2i * a1r,
            a2r * b1r - a2i * b1i + b2r, a2r * b1i + a2i * b1r + b2i)


def s5_scan(ab_re, ab_im, bu_re, bu_im, reverse, h0=None):
    shape = bu_re.shape
    elems = (jnp.broadcast_to(ab_re, shape), jnp.broadcast_to(ab_im, shape), bu_re, bu_im)
    p_re, p_im, h_re, h_im = lax.associative_scan(_affine_combine, elems, axis=1, reverse=reverse)
    if h0 is not None:
        h0r, h0i = h0[0][:, None], h0[1][:, None]
        h_re, h_im = h_re + p_re * h0r - p_im * h0i, h_im + p_re * h0i + p_im * h0r
    return h_re, h_im


def s5_branch(u_lat, u_ctx, lp, need_ctx):
    f32 = jnp.float32

    def drive(u, bb_re, bb_im):
        ug = u.astype(f32).reshape(u.shape[0], u.shape[1], S5_GROUPS, S5_GROUP)
        return jnp.einsum('blgc,gpc->blgp', ug, bb_re), jnp.einsum('blgc,gpc->blgp', ug, bb_im)

    def read(h_re, h_im, c_re, c_im):
        y = jnp.einsum('gcp,blgp->blgc', c_re, h_re) - jnp.einsum('gcp,blgp->blgc', c_im, h_im)
        return y.reshape(y.shape[0], y.shape[1], BR_W)

    d_skip = lp['s5_d'].astype(f32)
    y_lat = d_skip * u_lat.astype(f32)
    y_ctx = d_skip * u_ctx.astype(f32) if need_ctx else None
    for d, rev in enumerate((False, True)):
        ab_re, ab_im, bb_re, bb_im = s5_discretise(lp['s5_a_re'][d], lp['s5_a_im'][d], lp['s5_b_re'][d], lp['s5_b_im'][d], lp['s5_log_dt'][d])
        c_re = lp['s5_c_re'][d].astype(f32)
        c_im = lp['s5_c_im'][d].astype(f32)
        h_re, h_im = s5_scan(ab_re, ab_im, *drive(u_ctx, bb_re, bb_im), reverse=rev)
        end = 0 if rev else -1
        h_last = (h_re[:, end], h_im[:, end])
        if need_ctx:
            y_ctx = y_ctx + read(h_re, h_im, c_re, c_im)
        h_re, h_im = s5_scan(ab_re, ab_im, *drive(u_lat, bb_re, bb_im), reverse=rev, h0=h_last)
        y_lat = y_lat + read(h_re, h_im, c_re, c_im)

    def glu(y, dtype):
        y = jax.nn.gelu(y.astype(dtype))
        return y * jax.nn.sigmoid(y @ lp['s5_w_glu'])

    return glu(y_lat, u_lat.dtype), (glu(y_ctx, u_ctx.dtype) if need_ctx else None)


def sgu_branch(z, lp):
    z = jax.nn.gelu(z)
    u, v = jnp.split(z, 2, axis=-1)
    v = layer_norm(v, lp['sgu_ln_g'], lp['sgu_ln_b'])
    bsz, L, _ = v.shape
    vc = v.reshape(bsz, L // SGU_CHUNK, SGU_CHUNK, SGU_HEADS, BR_W // SGU_HEADS)
    mixed = jnp.einsum('hts,bnshd->bnthd', lp['sgu_w_s'], vc) + lp['sgu_b_s'].T[:, :, None]
    return u * mixed.reshape(bsz, L, BR_W)


def conv_centred(x, w, b):
    y = lax.conv_general_dilated(x, w[:, None, :].astype(x.dtype), window_strides=(1,),
                                 padding=[((SSD_CONV - 1) // 2, SSD_CONV // 2)],
                                 dimension_numbers=('NWC', 'WIO', 'NWC'),
                                 feature_group_count=x.shape[-1])
    return y + b


def ssd_scan(x, dA, Bh, Ch, h0, want_y):
    b, L, H, P = x.shape
    N = Bh.shape[-1]
    T = SSD_CHUNK
    nc = L // T
    xc = x.reshape(b, nc, T, H, P)
    Bc = Bh.reshape(b, nc, T, H, N)
    Cc = Ch.reshape(b, nc, T, H, N)
    a_cum = jnp.cumsum(dA.reshape(b, nc, T, H), axis=2)
    a_last = a_cum[:, :, -1]
    states = jnp.einsum('bclhn,bclh,bclhp->bchpn', Bc, jnp.exp(a_last[:, :, None] - a_cum), xc)

    def step(s, inp):
        st, dec = inp
        return s * dec[..., None, None] + st, s

    final, s_in = lax.scan(step, h0, (jnp.moveaxis(states, 1, 0), jnp.moveaxis(jnp.exp(a_last), 1, 0)))
    if not want_y:
        return None, final
    s_in = jnp.moveaxis(s_in, 0, 1)
    seg = a_cum[:, :, :, None, :] - a_cum[:, :, None, :, :]
    lower = jnp.tril(jnp.ones((T, T), dtype=bool))[None, None, :, :, None]
    decay = jnp.exp(jnp.where(lower, seg, -jnp.inf))
    scores = jnp.einsum('bclhn,bcshn->bclsh', Cc, Bc) * decay
    y_diag = jnp.einsum('bclsh,bcshp->bclhp', scores, xc)
    y_off = jnp.einsum('bclhn,bchpn->bclhp', Cc, s_in) * jnp.exp(a_cum)[..., None]
    return (y_diag + y_off).reshape(b, L, H, P), final


def ssd_branch(p_lat, p_ctx, lp, need_ctx):
    f32 = jnp.float32
    rep = SSD_HEADS // SSD_GROUPS

    def prep(p):
        bsz, L, _ = p.shape
        z, xbc, dt_raw = split_cols(p, (BR_W, SSD_CONV_CH, 2 * SSD_HEADS))
        xbc = jax.nn.silu(conv_centred(xbc, lp['ssd_conv_w'], lp['ssd_conv_b']))
        xs, bm, cm = split_cols(xbc, (BR_W, SSD_GN, SSD_GN))
        xs = xs.reshape(bsz, L, SSD_HEADS, SSD_HEAD_DIM).astype(f32)
        bm = jnp.repeat(bm.reshape(bsz, L, SSD_GROUPS, SSD_STATE), rep, axis=2).astype(f32)
        cm = jnp.repeat(cm.reshape(bsz, L, SSD_GROUPS, SSD_STATE), rep, axis=2).astype(f32)
        return z, xs, bm, cm, dt_raw.astype(f32)

    z_l, x_l, b_l, c_l, dt_l = prep(p_lat)
    z_c, x_c, b_c, c_c, dt_c = prep(p_ctx)
    d_skip = lp['ssd_d'].astype(f32)[:, None]
    y_lat = d_skip * x_l
    y_ctx = d_skip * x_c if need_ctx else None
    for d in range(2):
        A = -jnp.exp(lp['ssd_a_log'][d].astype(f32))
        dt_bias = lp['ssd_dt_bias'][d].astype(f32)

        def run(xs, bm, cm, dt_raw, h0, want_y):
            dt = jax.nn.softplus(dt_raw[..., d * SSD_HEADS:(d + 1) * SSD_HEADS] + dt_bias)
            args = (xs * dt[..., None], dt * A, bm, cm)
            if d == 1:
                args = tuple(jnp.flip(t, axis=1) for t in args)
            y, fin = ssd_scan(*args, h0, want_y)
            if d == 1 and y is not None:
                y = jnp.flip(y, axis=1)
            return y, fin

        h0 = jnp.zeros((x_c.shape[0], SSD_HEADS, SSD_HEAD_DIM, SSD_STATE), f32)
        yc, h_ctx_final = run(x_c, b_c, c_c, dt_c, h0, need_ctx)
        if need_ctx:
            y_ctx = y_ctx + yc
        yl, _ = run(x_l, b_l, c_l, dt_l, h_ctx_final, True)
        y_lat = y_lat + yl

    def out(y, z):
        y = y.reshape(z.shape).astype(z.dtype)
        return rms_norm(y * jax.nn.silu(z), lp['ssd_norm_g'])

    return out(y_lat, z_l), (out(y_ctx, z_c) if need_ctx else None)


def mla_qkv(p, lp):
    bsz, L, _ = p.shape
    cq, ckv, k_rope = split_cols(p, (MLA_Q_LORA, MLA_KV_LORA, MLA_ROPE))
    q = (rms_norm(cq, lp['mla_q_a_norm']) @ lp['mla_w_uq']).reshape(bsz, L, MLA_HEADS, MLA_QK)
    kv = (rms_norm(ckv, lp['mla_kv_a_norm']) @ lp['mla_w_ukv']).reshape(bsz, L, MLA_HEADS, MLA_NOPE + MLA_V)
    k_nope, v = jnp.split(kv, [MLA_NOPE], axis=-1)
    k = jnp.concatenate([k_nope, jnp.broadcast_to(k_rope[:, :, None, :], (bsz, L, MLA_HEADS, MLA_ROPE))], axis=-1)
    return rms_norm(q, lp['mla_q_norm']), rms_norm(k, lp['mla_k_norm']), v


def rope_2d(t, cos, sin):
    nope, rot = t[..., :MLA_NOPE], t[..., MLA_NOPE:]
    x1, x2 = jnp.split(rot, 2, axis=-1)
    c = cos[None, :, None, :].astype(t.dtype)
    s = sin[None, :, None, :].astype(t.dtype)
    return jnp.concatenate([nope, x1 * c - x2 * s, x1 * s + x2 * c], axis=-1)


def softmax_attend(q, k, v):
    s = jnp.einsum('bqhd,bkhd->bhqk', q, k).astype(jnp.float32) * (MLA_QK ** -0.5)
    pr = jax.nn.softmax(s, axis=-1).astype(v.dtype)
    return jnp.einsum('bhqk,bkhd->bqhd', pr, v)


def mla_branch(p_lat, p_ctx, cos, sin, lp, need_ctx):
    q_l, k_l, v_l = mla_qkv(p_lat, lp)
    q_c, k_c, v_c = mla_qkv(p_ctx, lp)
    q_l = rope_2d(q_l, cos, sin)
    k_l = rope_2d(k_l, cos, sin)
    k_all = jnp.concatenate([k_c, k_l], axis=1)
    v_all = jnp.concatenate([v_c, v_l], axis=1)
    bsz, L = q_l.shape[0], q_l.shape[1]
    nb = L // ATTN_BLOCK
    qb = jnp.moveaxis(q_l.reshape(bsz, nb, ATTN_BLOCK, MLA_HEADS, MLA_QK), 1, 0)
    ob = lax.map(lambda qq: softmax_attend(qq, k_all, v_all), qb)
    o_lat = jnp.moveaxis(ob, 0, 1).reshape(bsz, L, MLA_HEADS * MLA_V)
    o_ctx = softmax_attend(q_c, k_c, v_c).reshape(bsz, q_c.shape[1], MLA_HEADS * MLA_V) if need_ctx else None
    return o_lat, o_ctx


def gated_merge(ys, gate_cols, w_branch, w_out):
    gates = jax.nn.sigmoid(gate_cols)
    merged = None
    for i, y in enumerate(ys):
        term = gates[..., i * D_MODEL:(i + 1) * D_MODEL] * (y @ w_branch[i])
        merged = term if merged is None else merged + term
    return merged @ w_out


def token_mixing(h_lat, h_ctx, cos, sin, lp, need_ctx):
    p_lat = h_lat @ lp['w_in']
    p_ctx = h_ctx @ lp['w_in']
    s5_l, sgu_l, ssd_l, mla_l, gate_l = split_cols(p_lat, IN_SIZES)
    s5_c, sgu_c, ssd_c, mla_c, gate_c = split_cols(p_ctx, IN_SIZES)
    a_l, a_c = s5_branch(s5_l, s5_c, lp, need_ctx)
    b_l = sgu_branch(sgu_l, lp)
    b_c = sgu_branch(sgu_c, lp) if need_ctx else None
    c_l, c_c = ssd_branch(ssd_l, ssd_c, lp, need_ctx)
    d_l, d_c = mla_branch(mla_l, mla_c, cos, sin, lp, need_ctx)
    out_l = gated_merge((a_l, b_l, c_l, d_l), gate_l, lp['w_branch'], lp['w_out'])
    out_c = gated_merge((a_c, b_c, c_c, d_c), gate_c, lp['w_branch'], lp['w_out']) if need_ctx else None
    return out_l, out_c


def swiglu(h, w_in, w_out):
    g, u = jnp.split(h @ w_in, 2, axis=-1)
    return (jax.nn.silu(g) * u) @ w_out


def setup_inputs(seed: int = 0) -> dict:
    key = jax.random.key(seed)
    ks = iter(jax.random.split(key, 48))
    f32 = jnp.float32

    def nrm(shape, scale):
        return jax.random.normal(next(ks), shape, f32) * scale

    def gain(shape):
        return 1.0 + nrm(shape, 0.05)

    def log_uniform(shape, lo, hi):
        return jax.random.uniform(next(ks), shape, f32, math.log(lo), math.log(hi))

    x = nrm((BATCH, SEQ, D_MODEL), 1.0)
    c = nrm((BATCH, D_MODEL), 1.0)
    ctx = nrm((BATCH, CTX_LEN, D_MODEL), 1.0)
    c_ctx = nrm((D_MODEL,), 1.0)
    w_ada = nrm((DEPTH, D_MODEL, 6 * D_MODEL), 0.5 * D_MODEL ** -0.5)
    b_ada = nrm((DEPTH, 6 * D_MODEL), 0.02)
    norm1_g = gain((DEPTH, D_MODEL))
    norm2_g = gain((DEPTH, D_MODEL))
    w_in = nrm((DEPTH, D_MODEL, IN_W), D_MODEL ** -0.5)
    sg = (DEPTH, 2, S5_GROUPS, S5_STATE)
    s5_a_re = -0.5 + nrm(sg, 0.01)
    s5_a_im = math.pi * jnp.arange(S5_STATE, dtype=f32) + nrm(sg, 0.01)
    s5_b_re = nrm(sg + (S5_GROUP,), (2 * S5_GROUP) ** -0.5)
    s5_b_im = nrm(sg + (S5_GROUP,), (2 * S5_GROUP) ** -0.5)
    s5_c_re = nrm((DEPTH, 2, S5_GROUPS, S5_GROUP, S5_STATE), S5_STATE ** -0.5)
    s5_c_im = nrm((DEPTH, 2, S5_GROUPS, S5_GROUP, S5_STATE), S5_STATE ** -0.5)
    s5_log_dt = log_uniform((DEPTH, 2, S5_GROUPS), S5_DT_MIN, S5_DT_MAX)
    s5_d = nrm((DEPTH, BR_W), 1.0)
    s5_w_glu = nrm((DEPTH, BR_W, BR_W), BR_W ** -0.5)
    sgu_ln_g = gain((DEPTH, BR_W))
    sgu_ln_b = nrm((DEPTH, BR_W), 0.02)
    sgu_w_s = nrm((DEPTH, SGU_HEADS, SGU_CHUNK, SGU_CHUNK), SGU_CHUNK ** -0.5)
    sgu_b_s = 1.0 + nrm((DEPTH, SGU_HEADS, SGU_CHUNK), 0.1)
    ssd_conv_w = nrm((DEPTH, SSD_CONV, SSD_CONV_CH), SSD_CONV ** -0.5)
    ssd_conv_b = nrm((DEPTH, SSD_CONV_CH), 0.02)
    ssd_a_log = jnp.log(jax.random.uniform(next(ks), (DEPTH, 2, SSD_HEADS), f32, 1.0, 16.0))
    dt0 = jnp.exp(log_uniform((DEPTH, 2, SSD_HEADS), SSD_DT_MIN, SSD_DT_MAX))
    ssd_dt_bias = dt0 + jnp.log(-jnp.expm1(-dt0))
    ssd_d = 1.0 + nrm((DEPTH, SSD_HEADS), 0.1)
    ssd_norm_g = gain((DEPTH, BR_W))
    mla_q_a_norm = gain((DEPTH, MLA_Q_LORA))
    mla_w_uq = nrm((DEPTH, MLA_Q_LORA, MLA_HEADS * MLA_QK), MLA_Q_LORA ** -0.5)
    mla_kv_a_norm = gain((DEPTH, MLA_KV_LORA))
    mla_w_ukv = nrm((DEPTH, MLA_KV_LORA, MLA_HEADS * (MLA_NOPE + MLA_V)), MLA_KV_LORA ** -0.5)
    mla_q_norm = gain((DEPTH, MLA_QK))
    mla_k_norm = gain((DEPTH, MLA_QK))
    w_branch = nrm((DEPTH, N_BRANCH, BR_W, D_MODEL), BR_W ** -0.5)
    w_out = nrm((DEPTH, D_MODEL, D_MODEL), D_MODEL ** -0.5)
    w_ffn_in = nrm((DEPTH, D_MODEL, 2 * FFN_HIDDEN), D_MODEL ** -0.5)
    w_ffn_out = nrm((DEPTH, FFN_HIDDEN, D_MODEL), FFN_HIDDEN ** -0.5)
    return {'x': x, 'c': c, 'ctx': ctx, 'c_ctx': c_ctx, 'w_ada': w_ada, 'b_ada': b_ada,
            'norm1_g': norm1_g, 'norm2_g': norm2_g, 'w_in': w_in,
            's5_a_re': s5_a_re, 's5_a_im': s5_a_im, 's5_b_re': s5_b_re, 's5_b_im': s5_b_im,
            's5_c_re': s5_c_re, 's5_c_im': s5_c_im, 's5_log_dt': s5_log_dt, 's5_d': s5_d, 's5_w_glu': s5_w_glu,
            'sgu_ln_g': sgu_ln_g, 'sgu_ln_b': sgu_ln_b, 'sgu_w_s': sgu_w_s, 'sgu_b_s': sgu_b_s,
            'ssd_conv_w': ssd_conv_w, 'ssd_conv_b': ssd_conv_b, 'ssd_a_log': ssd_a_log,
            'ssd_dt_bias': ssd_dt_bias, 'ssd_d': ssd_d, 'ssd_norm_g': ssd_norm_g,
            'mla_q_a_norm': mla_q_a_norm, 'mla_w_uq': mla_w_uq, 'mla_kv_a_norm': mla_kv_a_norm,
            'mla_w_ukv': mla_w_ukv, 'mla_q_norm': mla_q_norm, 'mla_k_norm': mla_k_norm,
            'w_branch': w_branch, 'w_out': w_out, 'w_ffn_in': w_ffn_in, 'w_ffn_out': w_ffn_out}


def reference(x, c, ctx, c_ctx, w_ada, b_ada, norm1_g, norm2_g, w_in,
              s5_a_re, s5_a_im, s5_b_re, s5_b_im, s5_c_re, s5_c_im, s5_log_dt, s5_d, s5_w_glu,
              sgu_ln_g, sgu_ln_b, sgu_w_s, sgu_b_s,
              ssd_conv_w, ssd_conv_b, ssd_a_log, ssd_dt_bias, ssd_d, ssd_norm_g,
              mla_q_a_norm, mla_w_uq, mla_kv_a_norm, mla_w_ukv, mla_q_norm, mla_k_norm,
              w_branch, w_out, w_ffn_in, w_ffn_out):
    f32 = jnp.float32
    L = x.shape[1]
    n_rows = L // GRID_W
    pos_row = jnp.repeat(jnp.arange(n_rows, dtype=f32), GRID_W)
    pos_col = jnp.tile(jnp.arange(GRID_W, dtype=f32), n_rows)
    inv_freq = ROPE_BASE ** (-jnp.arange(ROPE_PAIRS_AXIS, dtype=f32) / ROPE_PAIRS_AXIS)
    ang = jnp.concatenate([pos_row[:, None] * inv_freq, pos_col[:, None] * inv_freq], axis=-1)
    cos, sin = jnp.cos(ang), jnp.sin(ang)

    silu_c = jax.nn.silu(c)
    silu_cc = jax.nn.silu(c_ctx)
    h_ctx_stream = ctx
    for l in range(DEPTH):
        need_ctx = l < DEPTH - 1
        lp = {'w_in': w_in[l],
              's5_a_re': s5_a_re[l], 's5_a_im': s5_a_im[l], 's5_b_re': s5_b_re[l], 's5_b_im': s5_b_im[l],
              's5_c_re': s5_c_re[l], 's5_c_im': s5_c_im[l], 's5_log_dt': s5_log_dt[l], 's5_d': s5_d[l],
              's5_w_glu': s5_w_glu[l],
              'sgu_ln_g': sgu_ln_g[l], 'sgu_ln_b': sgu_ln_b[l], 'sgu_w_s': sgu_w_s[l], 'sgu_b_s': sgu_b_s[l],
              'ssd_conv_w': ssd_conv_w[l], 'ssd_conv_b': ssd_conv_b[l], 'ssd_a_log': ssd_a_log[l],
              'ssd_dt_bias': ssd_dt_bias[l], 'ssd_d': ssd_d[l], 'ssd_norm_g': ssd_norm_g[l],
              'mla_q_a_norm': mla_q_a_norm[l], 'mla_w_uq': mla_w_uq[l], 'mla_kv_a_norm': mla_kv_a_norm[l],
              'mla_w_ukv': mla_w_ukv[l], 'mla_q_norm': mla_q_norm[l], 'mla_k_norm': mla_k_norm[l],
              'w_branch': w_branch[l], 'w_out': w_out[l]}
        mod_l = (silu_c @ w_ada[l] + b_ada[l])[:, None, :]
        mod_c = silu_cc @ w_ada[l] + b_ada[l]
        sh1, sc1, g1, sh2, sc2, g2 = jnp.split(mod_l, 6, axis=-1)
        csh1, csc1, cg1, csh2, csc2, cg2 = jnp.split(mod_c, 6, axis=-1)
        h_lat = modulate(rms_norm(x, norm1_g[l]), sh1, sc1)
        h_ctx = modulate(rms_norm(h_ctx_stream, norm1_g[l]), csh1, csc1)
        mix_lat, mix_ctx = token_mixing(h_lat, h_ctx, cos, sin, lp, need_ctx)
        x = x + g1 * mix_lat
        x = x + g2 * swiglu(modulate(rms_norm(x, norm2_g[l]), sh2, sc2), w_ffn_in[l], w_ffn_out[l])
        if need_ctx:
            h_ctx_stream = h_ctx_stream + cg1 * mix_ctx
            h_ctx_stream = h_ctx_stream + cg2 * swiglu(modulate(rms_norm(h_ctx_stream, norm2_g[l]), csh2, csc2), w_ffn_in[l], w_ffn_out[l])
    return x
```

```python
import functools

import jax
import jax.numpy as jnp
import numpy as np
from jax import lax
from jax.experimental import pallas as pl
from jax.experimental.pallas import tpu as pltpu

F32 = jnp.float32
BF16 = jnp.bfloat16
HIGHEST = lax.Precision.HIGHEST

LANE = 128
SUBLANE = 8
VMEM_LIMIT = 56 * 1024 * 1024

GRID_W = 64
BR_W = 384
S5_GROUP = 16
S5_GROUPS = BR_W // S5_GROUP
S5_STATE = 64
S5_NSEG = SUBLANE
S5_JB = BR_W // LANE
S5_BW = (LANE // S5_GROUP) * S5_STATE
SGU_CHUNK = 128
SGU_HEADS = 6
SSD_HEADS = 6
SSD_HEAD_DIM = 64
SSD_GROUPS = 2
SSD_STATE = 64
SSD_GN = SSD_GROUPS * SSD_STATE
SSD_CONV_CH = BR_W + 2 * SSD_GN
SSD_T = 128
MLA_HEADS = 6
MLA_NOPE = 64
MLA_ROPE = 32
MLA_V = 64
MLA_QK = MLA_NOPE + MLA_ROPE
MLA_Q_LORA = 384
MLA_KV_LORA = 256
ROPE_BASE = 10000.0
N_BRANCH = 4

TM = 512
TQ = 256


def _cp(*sem):
    return pltpu.CompilerParams(dimension_semantics=sem, vmem_limit_bytes=VMEM_LIMIT)


def _full(shape):
    n = len(shape)
    return pl.BlockSpec(shape, lambda *_: (0,) * n)


def _silu(x):
    return x * jax.nn.sigmoid(x)


def _rms(x, n, eps=1e-6):
    return x * lax.rsqrt(jnp.sum(x * x, axis=-1, keepdims=True) * (1.0 / n) + eps)


def _ada_kernel(cc_ref, w_ref, b_ref, o_ref):
    s = _silu(cc_ref[...])
    o_ref[...] = jnp.dot(s, w_ref[...], preferred_element_type=F32, precision=HIGHEST) + b_ref[...]


def ada_table(cc8, w_ada, b_ada):
    depth, d, n = w_ada.shape
    tn = n // 4
    return pl.pallas_call(
        _ada_kernel,
        out_shape=jax.ShapeDtypeStruct((depth, 8, n), F32),
        grid=(depth, n // tn),
        in_specs=[pl.BlockSpec((8, d), lambda l, j: (0, 0)),
                  pl.BlockSpec((None, d, tn), lambda l, j: (l, 0, j)),
                  pl.BlockSpec((None, 1, tn), lambda l, j: (l, 0, j))],
        out_specs=pl.BlockSpec((None, 8, tn), lambda l, j: (l, 0, j)),
        compiler_params=_cp("arbitrary", "arbitrary"),
        name="ada_table",
    )(cc8, w_ada, b_ada.reshape(depth, 1, n))


def _mod_spec(l, midx, col, d):
    return pl.BlockSpec((None, None, 1, d), lambda i: (l, midx(i), 0, col))


def _in_kernel(x_ref, g_ref, sh_ref, sc_ref, w1, w2, w3, w4, w5, h_ref, o1, o2, o3, o4, o5):
    x = x_ref[...]
    y = _rms(x, x.shape[-1]) * g_ref[...]
    hb = (y * (1.0 + sc_ref[...]) + sh_ref[...]).astype(BF16)
    h_ref[...] = hb
    for w, o in ((w1, o1), (w2, o2), (w3, o3), (w4, o4), (w5, o5)):
        o[...] = jnp.dot(hb, w[...], preferred_element_type=F32).astype(o.dtype)


def in_proj(x, g, mod4, l, midx, ws, out_dtypes):
    r, d = x.shape
    in_specs = [pl.BlockSpec((TM, d), lambda i: (i, 0)), _full((1, d)),
                _mod_spec(l, midx, 0, d), _mod_spec(l, midx, 1, d)]
    in_specs += [_full(w.shape) for w in ws]
    out_shape = [jax.ShapeDtypeStruct((r, d), BF16)]
    out_specs = [pl.BlockSpec((TM, d), lambda i: (i, 0))]
    for w, dt in zip(ws, out_dtypes):
        out_shape.append(jax.ShapeDtypeStruct((r, w.shape[1]), dt))
        out_specs.append(pl.BlockSpec((TM, w.shape[1]), lambda i: (i, 0)))
    return pl.pallas_call(
        _in_kernel, out_shape=out_shape, grid=(r // TM,), in_specs=in_specs, out_specs=out_specs,
        compiler_params=_cp("parallel"), name="in_proj",
    )(x, g, mod4, mod4, *ws)


def _s5_disc_kernel(are, aim, ldt, bre, bim, lam_re, lam_im, bbre, bbim):
    a_re, a_im = are[...], aim[...]
    dt = jnp.exp(ldt[...])
    mag = jnp.exp(a_re * dt)
    ang = a_im * dt
    ab_re = mag * jnp.cos(ang)
    ab_im = mag * jnp.sin(ang)
    den = a_re * a_re + a_im * a_im
    f_re = ((ab_re - 1.0) * a_re + ab_im * a_im) / den
    f_im = (ab_im * a_re - (ab_re - 1.0) * a_im) / den
    lam_re[...] = ab_re
    lam_im[...] = ab_im
    for c in range(S5_GROUP):
        bbre[c] = f_re * bre[c] - f_im * bim[c]
        bbim[c] = f_re * bim[c] + f_im * bre[c]


def s5_discretise(a_re, a_im, b_re, b_im, log_dt):
    shp = a_re.shape
    rows = int(np.prod(shp)) // LANE
    are = a_re.astype(F32).reshape(rows, LANE)
    aim = a_im.astype(F32).reshape(rows, LANE)
    ldt = jnp.broadcast_to(log_dt.astype(F32)[..., None], shp).reshape(rows, LANE)
    bre = jnp.moveaxis(b_re.astype(F32), -1, 0).reshape(S5_GROUP, rows, LANE)
    bim = jnp.moveaxis(b_im.astype(F32), -1, 0).reshape(S5_GROUP, rows, LANE)
    outs = pl.pallas_call(
        _s5_disc_kernel,
        out_shape=[jax.ShapeDtypeStruct((rows, LANE), F32)] * 2
        + [jax.ShapeDtypeStruct((S5_GROUP, rows, LANE), F32)] * 2,
        name="s5_discretise",
    )(are, aim, ldt, bre, bim)
    lam_re, lam_im = outs[0].reshape(shp), outs[1].reshape(shp)
    bb_re = jnp.moveaxis(outs[2].reshape((S5_GROUP,) + shp), 0, -1)
    bb_im = jnp.moveaxis(outs[3].reshape((S5_GROUP,) + shp), 0, -1)
    return lam_re, lam_im, bb_re, bb_im


def s5_pack(lam_re, lam_im, bb_re, bb_im, c_re, c_im):
    depth = lam_re.shape[0]
    gpb = LANE // S5_GROUP
    eye = jnp.eye(gpb, dtype=F32)

    def lam_blocks(v):
        return v.reshape(depth, 2, S5_JB, 1, S5_BW)

    lam = jnp.concatenate([lam_blocks(lam_re), lam_blocks(lam_im)], axis=-1)
    lam = jnp.broadcast_to(lam, (depth, 2, S5_JB, SUBLANE, 2 * S5_BW))

    def b_blocks(bb):
        v = bb.reshape(depth, 2, S5_JB, gpb, S5_STATE, S5_GROUP)
        return jnp.einsum('ldjgpc,gh->ldjgchp', v, eye).reshape(depth, 2, S5_JB, LANE, S5_BW)

    bblk = jnp.concatenate([b_blocks(bb_re), b_blocks(bb_im)], axis=-1).astype(BF16)

    def c_blocks(cc):
        v = cc.astype(F32).reshape(depth, 2, S5_JB, gpb, S5_GROUP, S5_STATE)
        return jnp.einsum('ldjgcp,gh->ldjhpgc', v, eye).reshape(depth, 2, S5_JB, S5_BW, LANE)

    cblk = jnp.concatenate([c_blocks(c_re), -c_blocks(c_im)], axis=-2).astype(BF16)
    return lam, bblk, cblk


def _cpow(re, im, n):
    out = None
    while n:
        if n & 1:
            out = (re, im) if out is None else (out[0] * re - out[1] * im, out[0] * im + out[1] * re)
        n >>= 1
        if n:
            re, im = re * re - im * im, 2.0 * re * im
    return out


def _s5_pass_kernel(*refs, tt, lseg, with_y):
    if with_y:
        u_ref, bblk, lam, cblk, sloc, h0, y_ref, hfin, uperm, bu, hst, yperm = refs
    else:
        u_ref, bblk, lam, s_out, uperm, bu, hst = refs
    d = pl.program_id(1)
    j = pl.program_id(2)

    @pl.when(j == 0)
    def _():
        if not with_y:
            hst[...] = jnp.zeros(hst.shape, F32)
        else:
            for jb in range(S5_JB):
                pr, pi = _cpow(lam[jb, 0:1, :S5_BW], lam[jb, 0:1, S5_BW:], lseg)

                def chain(order, jb=jb, pr=pr, pi=pi):
                    cr, ci = h0[jb, :, :S5_BW], h0[jb, :, S5_BW:]
                    for s in order:
                        hst[jb, s:s + 1, :S5_BW] = cr
                        hst[jb, s:s + 1, S5_BW:] = ci
                        sr, si = sloc[jb, s:s + 1, :S5_BW], sloc[jb, s:s + 1, S5_BW:]
                        cr, ci = pr * cr - pi * ci + sr, pr * ci + pi * cr + si
                    hfin[jb, :, :S5_BW] = cr
                    hfin[jb, :, S5_BW:] = ci

                @pl.when(d == 0)
                def _():
                    chain(range(S5_NSEG))

                @pl.when(d == 1)
                def _():
                    chain(range(S5_NSEG - 1, -1, -1))

    for s in range(S5_NSEG):
        us = u_ref[s].astype(F32)
        for k in range(S5_JB):
            uperm[k, pl.ds(s, tt, stride=S5_NSEG), :] = us[:, k * LANE:(k + 1) * LANE]

    for jb in range(S5_JB):
        bu[...] = jnp.dot(uperm[jb].astype(BF16), bblk[jb], preferred_element_type=F32)
        lr, li = lam[jb, :, :S5_BW], lam[jb, :, S5_BW:]

        def step(i, carry, lr=lr, li=li):
            hr, hi = carry
            t = i + d * (tt - 1 - 2 * i)
            r0 = pl.multiple_of(t * S5_NSEG, S5_NSEG)
            nr = lr * hr - li * hi + bu[pl.ds(r0, S5_NSEG), :S5_BW]
            ni = lr * hi + li * hr + bu[pl.ds(r0, S5_NSEG), S5_BW:]
            if with_y:
                bu[pl.ds(r0, S5_NSEG), :S5_BW] = nr
                bu[pl.ds(r0, S5_NSEG), S5_BW:] = ni
            return nr, ni

        hr, hi = lax.fori_loop(0, tt, step, (hst[jb, :, :S5_BW], hst[jb, :, S5_BW:]), unroll=4)
        hst[jb, :, :S5_BW] = hr
        hst[jb, :, S5_BW:] = hi
        if with_y:
            yperm[jb] = jnp.dot(bu[...].astype(BF16), cblk[jb], preferred_element_type=F32)

    if with_y:
        for s in range(S5_NSEG):
            for k in range(S5_JB):
                y_ref[s, :, k * LANE:(k + 1) * LANE] = (
                    yperm[k, pl.ds(s, tt, stride=S5_NSEG), :].astype(y_ref.dtype))
    else:
        s_out[...] = hst[...]


def s5_scan(u, bblk, lam, cblk, l, bsz, h0):
    seq_len = u.shape[0] // bsz
    lseg = seq_len // S5_NSEG
    tt = min(lseg, 128)
    nt = lseg // tt
    u5 = u.reshape(bsz, S5_NSEG, lseg, BR_W)

    def tile(d, j):
        return j + d * (nt - 1 - 2 * j)

    u_spec = pl.BlockSpec((None, S5_NSEG, tt, BR_W), lambda b, d, j: (b, 0, tile(d, j), 0))
    y_spec = pl.BlockSpec((None, None, S5_NSEG, tt, BR_W), lambda b, d, j: (d, b, 0, tile(d, j), 0))
    w_b = pl.BlockSpec((None, None, S5_JB, LANE, 2 * S5_BW), lambda b, d, j: (l, d, 0, 0, 0))
    w_lam = pl.BlockSpec((None, None, S5_JB, SUBLANE, 2 * S5_BW), lambda b, d, j: (l, d, 0, 0, 0))
    w_c = pl.BlockSpec((None, None, S5_JB, 2 * S5_BW, LANE), lambda b, d, j: (l, d, 0, 0, 0))
    st8 = pl.BlockSpec((None, None, S5_JB, SUBLANE, 2 * S5_BW), lambda b, d, j: (b, d, 0, 0, 0))
    st1 = pl.BlockSpec((None, None, S5_JB, 1, 2 * S5_BW), lambda b, d, j: (b, d, 0, 0, 0))
    n_rows = S5_NSEG * tt
    scratch = [pltpu.VMEM((S5_JB, n_rows, LANE), F32), pltpu.VMEM((n_rows, 2 * S5_BW), F32),
               pltpu.VMEM((S5_JB, SUBLANE, 2 * S5_BW), F32)]
    grid = (bsz, 2, nt)
    cp = _cp("arbitrary", "arbitrary", "arbitrary")

    sloc = pl.pallas_call(
        functools.partial(_s5_pass_kernel, tt=tt, lseg=lseg, with_y=False),
        out_shape=jax.ShapeDtypeStruct((bsz, 2, S5_JB, SUBLANE, 2 * S5_BW), F32),
        grid=grid, in_specs=[u_spec, w_b, w_lam], out_specs=st8,
        scratch_shapes=scratch, compiler_params=cp, name="s5_local",
    )(u5, bblk, lam)

    y, hfin = pl.pallas_call(
        functools.partial(_s5_pass_kernel, tt=tt, lseg=lseg, with_y=True),
        out_shape=[jax.ShapeDtypeStruct((2, bsz, S5_NSEG, lseg, BR_W), BF16),
                   jax.ShapeDtypeStruct((bsz, 2, S5_JB, 1, 2 * S5_BW), F32)],
        grid=grid, in_specs=[u_spec, w_b, w_lam, w_c, st8, st1], out_specs=[y_spec, st1],
        scratch_shapes=scratch + [pltpu.VMEM((S5_JB, n_rows, LANE), F32)],
        compiler_params=cp, name="s5_emit",
    )(u5, bblk, lam, cblk, sloc, h0)
    return y.reshape(2, bsz * seq_len, BR_W), hfin


def _s5_glu_kernel(u_ref, yf_ref, yb_ref, d_ref, w_ref, o_ref):
    y = d_ref[...] * u_ref[...].astype(F32) + yf_ref[...].astype(F32) + yb_ref[...].astype(F32)
    g = jax.nn.gelu(y)
    z = jnp.dot(g.astype(BF16), w_ref[...], preferred_element_type=F32)
    o_ref[...] = (g * jax.nn.sigmoid(z)).astype(o_ref.dtype)


def s5_glu(u, y, d_skip, w_glu):
    r = u.shape[0]
    row = pl.BlockSpec((TM, BR_W), lambda i: (i, 0))
    return pl.pallas_call(
        _s5_glu_kernel, out_shape=jax.ShapeDtypeStruct((r, BR_W), BF16), grid=(r // TM,),
        in_specs=[row, pl.BlockSpec((None, TM, BR_W), lambda i: (0, i, 0)),
                  pl.BlockSpec((None, TM, BR_W), lambda i: (1, i, 0)),
                  _full((1, BR_W)), _full((BR_W, BR_W))],
        out_specs=row, compiler_params=_cp("parallel"), name="s5_glu",
    )(u, y, y, d_skip, w_glu)


def _sgu_kernel(z_ref, g_ref, b_ref, w_ref, bias_ref, o_ref):
    z = jax.nn.gelu(z_ref[...].astype(F32))
    u, v = z[:, :BR_W], z[:, BR_W:]
    mu = jnp.mean(v, axis=-1, keepdims=True)
    vc = v - mu
    vn = vc * lax.rsqrt(jnp.mean(vc * vc, axis=-1, keepdims=True) + 1e-5) * g_ref[...] + b_ref[...]
    n_chunk = z.shape[0] // SGU_CHUNK
    half = LANE // 2
    lane = lax.broadcasted_iota(jnp.int32, (SGU_CHUNK, LANE), 1)
    for k in range(BR_W // LANE):
        cols = []
        for c in range(n_chunk):
            blk = vn[c * SGU_CHUNK:(c + 1) * SGU_CHUNK, k * LANE:(k + 1) * LANE]
            lo = jnp.where(lane < half, blk, 0.0)
            cols.append(jnp.concatenate([lo, blk - lo], axis=0))
        rhs = jnp.concatenate(cols, axis=1).astype(BF16)
        mixed = jnp.dot(w_ref[k], rhs, preferred_element_type=F32)
        for c in range(n_chunk):
            rows = slice(c * SGU_CHUNK, (c + 1) * SGU_CHUNK)
            m = mixed[:, c * LANE:(c + 1) * LANE] + bias_ref[:, k * LANE:(k + 1) * LANE]
            o_ref[rows, k * LANE:(k + 1) * LANE] = (u[rows, k * LANE:(k + 1) * LANE] * m).astype(o_ref.dtype)


def sgu(z, ln_g, ln_b, w_pair, bias):
    r = z.shape[0]
    return pl.pallas_call(
        _sgu_kernel, out_shape=jax.ShapeDtypeStruct((r, BR_W), BF16), grid=(r // TM,),
        in_specs=[pl.BlockSpec((TM, 2 * BR_W), lambda i: (i, 0)), _full((1, BR_W)), _full((1, BR_W)),
                  _full(w_pair.shape), _full(bias.shape)],
        out_specs=pl.BlockSpec((TM, BR_W), lambda i: (i, 0)),
        compiler_params=_cp("parallel"), name="sgu",
    )(z, ln_g, ln_b, w_pair, bias)


def _ssd_prep_kernel(cur_ref, prev_ref, next_ref, dt_ref, w_ref, b_ref, dtb_ref, xs_ref, bc_ref, dtp_ref,
                     *, seq_len):
    i = pl.program_id(0)
    x = cur_ref[:, BR_W:].astype(F32)
    tm = x.shape[0]
    row = lax.broadcasted_iota(jnp.int32, (tm, 1), 0)
    pos = lax.rem(row + i * tm, seq_len)
    prev_row = prev_ref[SUBLANE - 1:SUBLANE, BR_W:].astype(F32)
    next_row = next_ref[0:1, BR_W:].astype(F32)
    x_prev = jnp.where(row == 0, prev_row, pltpu.roll(x, 1, axis=0))
    x_prev = jnp.where(pos == 0, 0.0, x_prev)
    x_next = jnp.where(row == tm - 1, next_row, pltpu.roll(x, tm - 1, axis=0))
    x_next = jnp.where(pos == seq_len - 1, 0.0, x_next)
    y = _silu(w_ref[0:1, :] * x_prev + w_ref[1:2, :] * x + w_ref[2:3, :] * x_next + b_ref[...])
    xs_ref[...] = y[:, :BR_W]
    bc_ref[...] = y[:, BR_W:].astype(bc_ref.dtype)
    t = dt_ref[...] + dtb_ref[...]
    sp = jnp.maximum(t, 0.0) + jnp.log1p(jnp.exp(-jnp.abs(t)))
    lane = lax.broadcasted_iota(jnp.int32, t.shape, 1)
    dtp_ref[...] = jnp.where(lane < 2 * SSD_HEADS, sp, 0.0)


def ssd_prep(p_ssd, dt_raw, seq_len, conv_w, conv_b, dt_bias):
    r, w = p_ssd.shape
    nb = r // SUBLANE
    per = TM // SUBLANE
    return pl.pallas_call(
        functools.partial(_ssd_prep_kernel, seq_len=seq_len),
        out_shape=[jax.ShapeDtypeStruct((r, BR_W), F32), jax.ShapeDtypeStruct((r, 2 * SSD_GN), BF16),
                   jax.ShapeDtypeStruct((r, LANE), F32)],
        grid=(r // TM,),
        in_specs=[pl.BlockSpec((TM, w), lambda i: (i, 0)),
                  pl.BlockSpec((SUBLANE, w), lambda i: (jnp.maximum(i * per - 1, 0), 0)),
                  pl.BlockSpec((SUBLANE, w), lambda i: (jnp.minimum((i + 1) * per, nb - 1), 0)),
                  pl.BlockSpec((TM, LANE), lambda i: (i, 0)),
                  _full(conv_w.shape), _full(conv_b.shape), _full(dt_bias.shape)],
        out_specs=[pl.BlockSpec((TM, BR_W), lambda i: (i, 0)), pl.BlockSpec((TM, 2 * SSD_GN), lambda i: (i, 0)),
                   pl.BlockSpec((TM, LANE), lambda i: (i, 0))],
        compiler_params=_cp("parallel"), name="ssd_prep",
    )(p_ssd, p_ssd, p_ssd, dt_raw, conv_w, conv_b, dt_bias)


def _ssd_scan_kernel(xs_ref, bc_ref, dtp_ref, asel_ref, esel_ref, tri_ref, h0_ref, y_ref, hfin_ref, st_ref,
                     *, n_chunk):
    d = pl.program_id(1)
    j = pl.program_id(2)
    T = SSD_T

    @pl.when(j == 0)
    def _():
        st_ref[...] = h0_ref[...]

    tri = tri_ref[...]
    mask = tri > 0.5

    def chunk(ci, carry):
        c = ci + d * (n_chunk - 1 - 2 * ci)
        r0 = pl.multiple_of(c * T, T)
        dtp = dtp_ref[pl.ds(r0, T), :]
        d_a = jnp.dot(dtp, asel_ref[...], preferred_element_type=F32, precision=HIGHEST)
        a_cum = jnp.dot(tri, d_a, preferred_element_type=F32, precision=HIGHEST)
        a_cum_t = a_cum.T
        total = jnp.sum(d_a, axis=0, keepdims=True)
        dt_x = jnp.dot(dtp, esel_ref[...], preferred_element_type=F32, precision=HIGHEST)
        xt = (xs_ref[pl.ds(r0, T), :] * dt_x).astype(BF16)
        bc = bc_ref[pl.ds(r0, T), :]
        bm_t = bc[:, :SSD_GN].astype(F32).T
        cm = bc[:, SSD_GN:]
        rep = SSD_HEADS // SSD_GROUPS
        for g in range(SSD_GROUPS):
            gs = slice(g * SSD_STATE, (g + 1) * SSD_STATE)
            b_t = bm_t[gs, :]
            c_g = cm[:, gs]
            scores = jnp.dot(c_g, b_t.astype(BF16), preferred_element_type=F32)
            for hh in range(rep):
                h = g * rep + hh
                hs = slice(h * SSD_HEAD_DIM, (h + 1) * SSD_HEAD_DIM)
                col = a_cum[:, h:h + 1]
                rowv = a_cum_t[h:h + 1, :]
                tot = total[:, h:h + 1]
                decay = jnp.exp(jnp.where(mask, col - rowv, -1e30))
                p = (scores * decay).astype(BF16)
                x_h = xt[:, hs]
                s_old = st_ref[h]
                y_h = jnp.dot(p, x_h, preferred_element_type=F32)
                y_h += jnp.dot(c_g, s_old.astype(BF16), preferred_element_type=F32) * jnp.exp(col)
                bw = (b_t * jnp.exp(tot - rowv)).astype(BF16)
                st_ref[h] = s_old * jnp.exp(tot) + jnp.dot(bw, x_h, preferred_element_type=F32)
                y_ref[pl.ds(r0, T), hs] = y_h.astype(y_ref.dtype)
        return carry

    lax.fori_loop(0, n_chunk, chunk, 0)
    hfin_ref[...] = st_ref[...]


def ssd_scan(xs, bc, dtp, asel, esel, tri, l, bsz, h0):
    r = xs.shape[0]
    seq_len = r // bsz
    ts = min(seq_len, TM)
    nt = seq_len // ts

    def rows(width):
        return pl.BlockSpec((ts, width), lambda b, d, j: (b * nt + j + d * (nt - 1 - 2 * j), 0))

    st = pl.BlockSpec((None, None, SSD_HEADS, SSD_STATE, SSD_HEAD_DIM), lambda b, d, j: (b, d, 0, 0, 0))
    return pl.pallas_call(
        functools.partial(_ssd_scan_kernel, n_chunk=ts // SSD_T),
        out_shape=[jax.ShapeDtypeStruct((2, r, BR_W), BF16),
                   jax.ShapeDtypeStruct((bsz, 2, SSD_HEADS, SSD_STATE, SSD_HEAD_DIM), F32)],
        grid=(bsz, 2, nt),
        in_specs=[rows(BR_W), rows(2 * SSD_GN), rows(LANE),
                  pl.BlockSpec((None, None, LANE, LANE), lambda b, d, j: (l, d, 0, 0)),
                  pl.BlockSpec((None, LANE, BR_W), lambda b, d, j: (d, 0, 0)),
                  pl.BlockSpec((None, SSD_T, SSD_T), lambda b, d, j: (d, 0, 0)), st],
        out_specs=[pl.BlockSpec((None, ts, BR_W), lambda b, d, j: (d, b * nt + j + d * (nt - 1 - 2 * j), 0)), st],
        scratch_shapes=[pltpu.VMEM((SSD_HEADS, SSD_STATE, SSD_HEAD_DIM), F32)],
        compiler_params=_cp("arbitrary", "arbitrary", "arbitrary"), name="ssd_scan",
    )(xs, bc, dtp, asel, esel, tri, h0)


def _ssd_out_kernel(xs_ref, yf_ref, yb_ref, z_ref, d_ref, g_ref, o_ref):
    y = d_ref[...] * xs_ref[...] + yf_ref[...].astype(F32) + yb_ref[...].astype(F32)
    v = y * _silu(z_ref[...].astype(F32))
    o_ref[...] = (_rms(v, BR_W) * g_ref[...]).astype(o_ref.dtype)


def ssd_out(xs, y, p_ssd, d_skip, norm_g):
    r = xs.shape[0]
    row = pl.BlockSpec((TM, BR_W), lambda i: (i, 0))
    return pl.pallas_call(
        _ssd_out_kernel, out_shape=jax.ShapeDtypeStruct((r, BR_W), BF16), grid=(r // TM,),
        in_specs=[row, pl.BlockSpec((None, TM, BR_W), lambda i: (0, i, 0)),
                  pl.BlockSpec((None, TM, BR_W), lambda i: (1, i, 0)), row,
                  _full((1, BR_W)), _full((1, BR_W))],
        out_specs=row, compiler_params=_cp("parallel"), name="ssd_out",
    )(xs, y, y, p_ssd, d_skip, norm_g)


def _rope_table_kernel(ang_ref, cos_ref, sa_ref, sb_ref):
    ang = ang_ref[...]
    lane = lax.broadcasted_iota(jnp.int32, ang.shape, 1)
    quarter = MLA_ROPE // 2
    first = (lane >= MLA_NOPE) & (lane < MLA_NOPE + quarter)
    second = (lane >= MLA_NOPE + quarter) & (lane < MLA_QK)
    c, s = jnp.cos(ang), jnp.sin(ang)
    cos_ref[...] = jnp.where(first | second, c, jnp.where(lane < MLA_NOPE, 1.0, 0.0))
    sa_ref[...] = jnp.where(first, -s, 0.0)
    sb_ref[...] = jnp.where(second, s, 0.0)


def rope_tables(seq_len):
    n_rows = seq_len // GRID_W
    pairs = MLA_ROPE // 4
    pos_row = jnp.repeat(jnp.arange(n_rows, dtype=F32), GRID_W)
    pos_col = jnp.tile(jnp.arange(GRID_W, dtype=F32), n_rows)
    inv_freq = ROPE_BASE ** (-jnp.arange(pairs, dtype=F32) / pairs)
    ang = jnp.concatenate([pos_row[:, None] * inv_freq, pos_col[:, None] * inv_freq], axis=-1)
    ang_pad = jnp.concatenate([jnp.zeros((seq_len, MLA_NOPE), F32), ang, ang,
                               jnp.zeros((seq_len, LANE - MLA_QK), F32)], axis=-1)
    tr = min(seq_len, 1024)
    spec = pl.BlockSpec((tr, LANE), lambda i: (i, 0))
    return pl.pallas_call(
        _rope_table_kernel, out_shape=[jax.ShapeDtypeStruct((seq_len, LANE), F32)] * 3,
        grid=(seq_len // tr,), in_specs=[spec], out_specs=[spec] * 3,
        compiler_params=_cp("parallel"), name="rope_tables",
    )(ang_pad)


def _mla_prep_kernel(*refs, rope):
    if rope:
        p_ref, gq_ref, gkv_ref, wq_ref, wkv_ref, nq_ref, nk_ref, cos_ref, sa_ref, sb_ref, q_ref, k_ref, v_ref = refs
    else:
        p_ref, gq_ref, gkv_ref, wq_ref, wkv_ref, nq_ref, nk_ref, q_ref, k_ref, v_ref = refs
    p = p_ref[...].astype(F32)
    cq = _rms(p[:, :MLA_Q_LORA], MLA_Q_LORA) * gq_ref[...]
    ckv = _rms(p[:, MLA_Q_LORA:MLA_Q_LORA + MLA_KV_LORA], MLA_KV_LORA) * gkv_ref[...]
    kr = p[:, MLA_Q_LORA + MLA_KV_LORA:]
    q_all = jnp.dot(cq.astype(BF16), wq_ref[...], preferred_element_type=F32)
    kv_all = jnp.dot(ckv.astype(BF16), wkv_ref[...], preferred_element_type=F32)
    lane = lax.broadcasted_iota(jnp.int32, (1, LANE), 1)
    one_col = jnp.where(lane == MLA_V, 1.0, 0.0)
    scale = MLA_QK ** -0.5
    quarter = MLA_ROPE // 2

    def rot(t):
        if not rope:
            return t
        return (t * cos_ref[...] + pltpu.roll(t, LANE - quarter, axis=1) * sa_ref[...]
                + pltpu.roll(t, quarter, axis=1) * sb_ref[...])

    for h in range(MLA_HEADS):
        q = q_all[:, h * LANE:(h + 1) * LANE]
        q_ref[h] = (rot(_rms(q, MLA_QK) * nq_ref[...]) * scale).astype(q_ref.dtype)
        k = kv_all[:, 2 * h * LANE:(2 * h + 1) * LANE] + kr
        k_ref[h] = rot(_rms(k, MLA_QK) * nk_ref[...]).astype(k_ref.dtype)
        v_ref[h] = (kv_all[:, (2 * h + 1) * LANE:(2 * h + 2) * LANE] + one_col).astype(v_ref.dtype)


def mla_prep(p_mla, bsz, gq, gkv, wq, wkv, nq, nk, tables):
    r, w = p_mla.shape
    seq_len = r // bsz
    tr = min(seq_len, TM)
    nt = seq_len // tr
    rope = tables is not None
    in_specs = [pl.BlockSpec((tr, w), lambda b, i: (b * nt + i, 0)), _full(gq.shape), _full(gkv.shape),
                _full(wq.shape), _full(wkv.shape), _full(nq.shape), _full(nk.shape)]
    args = [p_mla, gq, gkv, wq, wkv, nq, nk]
    if rope:
        in_specs += [pl.BlockSpec((tr, LANE), lambda b, i: (i, 0))] * 3
        args += list(tables)
    head = pl.BlockSpec((None, MLA_HEADS, tr, LANE), lambda b, i: (b, 0, i, 0))
    return pl.pallas_call(
        functools.partial(_mla_prep_kernel, rope=rope),
        out_shape=[jax.ShapeDtypeStruct((bsz, MLA_HEADS, seq_len, LANE), BF16)] * 3,
        grid=(bsz, nt), in_specs=in_specs, out_specs=[head] * 3,
        compiler_params=_cp("parallel", "parallel"), name="mla_prep",
    )(*args)


def _attn_kernel(*refs, n_kv):
    q_ref = refs[0]
    kv_refs = refs[1:1 + 2 * n_kv]
    o_ref = refs[1 + 2 * n_kv]
    for hh in range(2):
        q = q_ref[hh]
        s = [lax.dot_general(q, kv_refs[2 * i][hh], (((1,), (1,)), ((), ())), preferred_element_type=F32)
             for i in range(n_kv)]
        m = s[0].max(axis=-1, keepdims=True)
        for si in s[1:]:
            m = jnp.maximum(m, si.max(axis=-1, keepdims=True))
        acc = None
        for i, si in enumerate(s):
            pv = jnp.dot(jnp.exp(si - m).astype(BF16), kv_refs[2 * i + 1][hh], preferred_element_type=F32)
            acc = pv if acc is None else acc + pv
        o = acc[:, :MLA_V] / acc[:, MLA_V:MLA_V + 1]
        o_ref[:, hh * MLA_V:(hh + 1) * MLA_V] = o.astype(o_ref.dtype)


def attention(q, kvs):
    bsz, nh, lq, _ = q.shape
    tq = min(lq, TQ)
    nq = lq // tq
    in_specs = [pl.BlockSpec((None, 2, tq, LANE), lambda b, hp, i: (b, hp, i, 0))]
    args = [q]
    for k, v in kvs:
        spec = pl.BlockSpec((None, 2, k.shape[2], LANE), lambda b, hp, i: (b, hp, 0, 0))
        in_specs += [spec, spec]
        args += [k, v]
    return pl.pallas_call(
        functools.partial(_attn_kernel, n_kv=len(kvs)),
        out_shape=jax.ShapeDtypeStruct((bsz * lq, nh * MLA_V), BF16),
        grid=(bsz, nh // 2, nq), in_specs=in_specs,
        out_specs=pl.BlockSpec((tq, 2 * MLA_V), lambda b, hp, i: (b * nq + i, hp)),
        compiler_params=_cp("parallel", "parallel", "arbitrary"), name="attention",
    )(*args)


def _merge_kernel(x_ref, h_ref, a_ref, b_ref, c_ref, d_ref, g1_ref, wg_ref, wb_ref, wo_ref, o_ref):
    h = h_ref[...]
    dm = x_ref.shape[-1]
    merged = None
    for i, br in enumerate((a_ref, b_ref, c_ref, d_ref)):
        gate = jax.nn.sigmoid(jnp.dot(h, wg_ref[:, i * dm:(i + 1) * dm], preferred_element_type=F32))
        term = gate * jnp.dot(br[...], wb_ref[i], preferred_element_type=F32)
        merged = term if merged is None else merged + term
    mix = jnp.dot(merged.astype(BF16), wo_ref[...], preferred_element_type=F32)
    o_ref[...] = x_ref[...] + g1_ref[...] * mix


def merge(x, h, branches, mod4, l, midx, wg, wb, wo):
    r, d = x.shape
    row = pl.BlockSpec((TM, d), lambda i: (i, 0))
    br = pl.BlockSpec((TM, BR_W), lambda i: (i, 0))
    return pl.pallas_call(
        _merge_kernel, out_shape=jax.ShapeDtypeStruct((r, d), F32), grid=(r // TM,),
        in_specs=[row, row, br, br, br, br, _mod_spec(l, midx, 2, d),
                  _full(wg.shape), _full(wb.shape), _full(wo.shape)],
        out_specs=row, compiler_params=_cp("parallel"), name="merge",
    )(x, h, *branches, mod4, wg, wb, wo)


def _ffn_kernel(x_ref, g_ref, sh_ref, sc_ref, g2_ref, wi_ref, wo_ref, o_ref, *, n_split):
    x = x_ref[...]
    y = _rms(x, x.shape[-1]) * g_ref[...]
    hb = (y * (1.0 + sc_ref[...]) + sh_ref[...]).astype(BF16)
    hid = wo_ref.shape[0]
    step = hid // n_split
    acc = None
    for c in range(n_split):
        gate = jnp.dot(hb, wi_ref[:, c * step:(c + 1) * step], preferred_element_type=F32)
        up = jnp.dot(hb, wi_ref[:, hid + c * step:hid + (c + 1) * step], preferred_element_type=F32)
        part = jnp.dot((_silu(gate) * up).astype(BF16), wo_ref[c * step:(c + 1) * step, :],
                       preferred_element_type=F32)
        acc = part if acc is None else acc + part
    o_ref[...] = x + g2_ref[...] * acc


def ffn(x, g, mod4, l, midx, wi, wo):
    r, d = x.shape
    row = pl.BlockSpec((TM, d), lambda i: (i, 0))
    n_split = 11 if wo.shape[0] % (11 * LANE) == 0 else 1
    return pl.pallas_call(
        functools.partial(_ffn_kernel, n_split=n_split),
        out_shape=jax.ShapeDtypeStruct((r, d), F32), grid=(r // TM,),
        in_specs=[row, _full((1, d)), _mod_spec(l, midx, 3, d), _mod_spec(l, midx, 4, d),
                  _mod_spec(l, midx, 5, d), _full(wi.shape), _full(wo.shape)],
        out_specs=row, compiler_params=_cp("parallel"), name="ffn",
    )(x, g, mod4, mod4, mod4, wi, wo)


def _pad_cols(w, n):
    return jnp.pad(w, ((0, 0),) * (w.ndim - 1) + ((0, n - w.shape[-1]),))


def kernel(x, c, ctx, c_ctx, w_ada, b_ada, norm1_g, norm2_g, w_in, s5_a_re, s5_a_im, s5_b_re, s5_b_im, s5_c_re, s5_c_im, s5_log_dt, s5_d, s5_w_glu, sgu_ln_g, sgu_ln_b, sgu_w_s, sgu_b_s, ssd_conv_w, ssd_conv_b, ssd_a_log, ssd_dt_bias, ssd_d, ssd_norm_g, mla_q_a_norm, mla_w_uq, mla_kv_a_norm, mla_w_ukv, mla_q_norm, mla_k_norm, w_branch, w_out, w_ffn_in, w_ffn_out):
    bsz, seq, dm = x.shape
    lc = ctx.shape[1]
    depth = w_ada.shape[0]
    assert seq % TM == 0 and (bsz * lc) % TM == 0 and seq % (S5_NSEG * SUBLANE) == 0
    assert lc % SSD_T == 0 and lc % (S5_NSEG * SUBLANE) == 0 and bsz + 1 <= 8

    cc8 = jnp.zeros((8, dm), F32).at[:bsz].set(c.astype(F32)).at[bsz].set(c_ctx.astype(F32))
    mod4 = ada_table(cc8, w_ada.astype(F32), b_ada.astype(F32)).reshape(depth, 8, 1, 6 * dm)
    lat_tiles = seq // TM
    midx_lat = lambda i: i // lat_tiles
    midx_ctx = lambda i: bsz

    off = np.cumsum([0, BR_W, 2 * BR_W, BR_W + SSD_CONV_CH + 2 * SSD_HEADS,
                     MLA_Q_LORA + MLA_KV_LORA + MLA_ROPE, N_BRANCH * dm])
    w_s5 = w_in[:, :, off[0]:off[1]].astype(BF16)
    w_sgu = w_in[:, :, off[1]:off[2]].astype(BF16)
    w_ssd = w_in[:, :, off[2]:off[2] + BR_W + SSD_CONV_CH].astype(BF16)
    w_dt = _pad_cols(w_in[:, :, off[2] + BR_W + SSD_CONV_CH:off[3]], LANE).astype(BF16)
    w_mla_main = w_in[:, :, off[3]:off[3] + MLA_Q_LORA + MLA_KV_LORA]
    w_kr = w_in[:, :, off[3] + MLA_Q_LORA + MLA_KV_LORA:off[4]]
    zeros = lambda n: jnp.zeros((depth, dm, n), w_in.dtype)
    w_mla = jnp.concatenate([w_mla_main, zeros(MLA_NOPE), w_kr, zeros(LANE - MLA_QK)], axis=-1).astype(BF16)
    w_gate = w_in[:, :, off[4]:off[5]].astype(BF16)

    lam_re, lam_im, bb_re, bb_im = s5_discretise(s5_a_re, s5_a_im, s5_b_re, s5_b_im, s5_log_dt)
    lam, bblk, cblk = s5_pack(lam_re, lam_im, bb_re, bb_im, s5_c_re, s5_c_im)
    s5_wg = s5_w_glu.astype(BF16)

    sgu_w = sgu_w_s.reshape(depth, SGU_HEADS // 2, 2, SGU_CHUNK, SGU_CHUNK)
    sgu_w = jnp.concatenate([sgu_w[:, :, 0], sgu_w[:, :, 1]], axis=-1).astype(BF16)
    sgu_b = jnp.repeat(jnp.swapaxes(sgu_b_s, 1, 2), BR_W // SGU_HEADS, axis=-1).astype(F32)

    a_neg = -jnp.exp(ssd_a_log.astype(F32))
    lane_ids = jnp.arange(LANE)
    head_of_lane = lane_ids[None, :, None] - SSD_HEADS * jnp.arange(2)[:, None, None]
    sel = (head_of_lane == jnp.arange(LANE)[None, None, :]) & (head_of_lane >= 0) & (head_of_lane < SSD_HEADS)
    a_lane = jnp.pad(a_neg, ((0, 0), (0, 0), (0, LANE - SSD_HEADS)))
    asel = sel[None].astype(F32) * a_lane[:, :, None, :]
    col_head = jnp.arange(BR_W) // SSD_HEAD_DIM
    esel = ((head_of_lane == col_head[None, None, :]) & (head_of_lane >= 0)).astype(F32)
    ti = jnp.arange(SSD_T)
    tri = jnp.stack([ti[None, :] <= ti[:, None], ti[None, :] >= ti[:, None]]).astype(F32)
    dt_bias = _pad_cols(ssd_dt_bias.astype(F32).reshape(depth, 1, 2 * SSD_HEADS), LANE)
    ssd_dskip = jnp.repeat(ssd_d.astype(F32), SSD_HEAD_DIM, axis=-1)[:, None, :]

    wq = mla_w_uq.reshape(depth, MLA_Q_LORA, MLA_HEADS, MLA_QK)
    wq = _pad_cols(wq, LANE).reshape(depth, MLA_Q_LORA, MLA_HEADS * LANE).astype(BF16)
    wkv = mla_w_ukv.reshape(depth, MLA_KV_LORA, MLA_HEADS, 2, MLA_NOPE)
    wkv = _pad_cols(wkv, LANE).reshape(depth, MLA_KV_LORA, MLA_HEADS * 2 * LANE).astype(BF16)
    nq = _pad_cols(mla_q_norm.astype(F32), LANE)[:, None, :]
    nk = _pad_cols(mla_k_norm.astype(F32), LANE)[:, None, :]
    tables = rope_tables(seq)

    wb = w_branch.astype(BF16)
    wo = w_out.astype(BF16)
    wfi = w_ffn_in.astype(BF16)
    wfo = w_ffn_out.astype(BF16)

    row = lambda v, l: v[l].astype(F32).reshape(1, -1)
    x_lat = x.astype(F32).reshape(bsz * seq, dm)
    x_ctx = ctx.astype(F32).reshape(bsz * lc, dm)
    s5_zero = jnp.zeros((bsz, 2, S5_JB, 1, 2 * S5_BW), F32)
    ssd_zero = jnp.zeros((bsz, 2, SSD_HEADS, SSD_STATE, SSD_HEAD_DIM), F32)

    for l in range(depth):
        need_ctx = l < depth - 1
        ws = (w_s5[l], w_sgu[l], w_ssd[l], w_dt[l], w_mla[l])
        dts = (BF16, BF16, BF16, F32, BF16)
        g1 = row(norm1_g, l)
        h_l, u_l, z_l, p_l, dtr_l, m_l = in_proj(x_lat, g1, mod4, l, midx_lat, ws, dts)
        h_c, u_c, z_c, p_c, dtr_c, m_c = in_proj(x_ctx, g1, mod4, l, midx_ctx, ws, dts)

        y_c, s5_h = s5_scan(u_c, bblk, lam, cblk, l, bsz, s5_zero)
        y_l, _ = s5_scan(u_l, bblk, lam, cblk, l, bsz, s5_h)
        a_l = s5_glu(u_l, y_l, row(s5_d, l), s5_wg[l])
        b_l = sgu(z_l, row(sgu_ln_g, l), row(sgu_ln_b, l), sgu_w[l], sgu_b[l])
        conv_w, conv_b = ssd_conv_w[l].astype(F32), row(ssd_conv_b, l)
        xs_c, bc_c, dtp_c = ssd_prep(p_c, dtr_c, lc, conv_w, conv_b, dt_bias[l])
        xs_l, bc_l, dtp_l = ssd_prep(p_l, dtr_l, seq, conv_w, conv_b, dt_bias[l])
        yc_c, ssd_h = ssd_scan(xs_c, bc_c, dtp_c, asel, esel, tri, l, bsz, ssd_zero)
        yc_l, _ = ssd_scan(xs_l, bc_l, dtp_l, asel, esel, tri, l, bsz, ssd_h)
        c_l = ssd_out(xs_l, yc_l, p_l, ssd_dskip[l], row(ssd_norm_g, l))
        mla_w = (row(mla_q_a_norm, l), row(mla_kv_a_norm, l), wq[l], wkv[l], nq[l], nk[l])
        q_c, k_c, v_c = mla_prep(m_c, bsz, *mla_w, None)
        q_l, k_l, v_l = mla_prep(m_l, bsz, *mla_w, tables)
        d_l = attention(q_l, [(k_l, v_l), (k_c, v_c)])

        x_lat = merge(x_lat, h_l, (a_l, b_l, c_l, d_l), mod4, l, midx_lat, w_gate[l], wb[l], wo[l])
        x_lat = ffn(x_lat, row(norm2_g, l), mod4, l, midx_lat, wfi[l], wfo[l])
        if need_ctx:
            a_c = s5_glu(u_c, y_c, row(s5_d, l), s5_wg[l])
            b_c = sgu(z_c, row(sgu_ln_g, l), row(sgu_ln_b, l), sgu_w[l], sgu_b[l])
            c_c = ssd_out(xs_c, yc_c, p_c, ssd_dskip[l], row(ssd_norm_g, l))
            d_c = attention(q_c, [(k_c, v_c)])
            x_ctx = merge(x_ctx, h_c, (a_c, b_c, c_c, d_c), mod4, l, midx_ctx, w_gate[l], wb[l], wo[l])
            x_ctx = ffn(x_ctx, row(norm2_g, l), mod4, l, midx_ctx, wfi[l], wfo[l])
    return x_lat.reshape(bsz, seq, dm).astype(x.dtype)
```

```python
import functools

import jax
import jax.numpy as jnp
import numpy as np
from jax import lax
from jax.experimental import pallas as pl
from jax.experimental.pallas import tpu as pltpu

F32 = jnp.float32
BF16 = jnp.bfloat16
HIGHEST = lax.Precision.HIGHEST

LANE = 128
SUBLANE = 8
VMEM_LIMIT = 56 * 1024 * 1024

GRID_W = 64
BR_W = 384
S5_GROUP = 16
S5_GROUPS = BR_W // S5_GROUP
S5_STATE = 64
S5_NSEG = SUBLANE
S5_JB = BR_W // LANE
S5_BW = (LANE // S5_GROUP) * S5_STATE
SGU_CHUNK = 128
SGU_HEADS = 6
SSD_HEADS = 6
SSD_HEAD_DIM = 64
SSD_GROUPS = 2
SSD_STATE = 64
SSD_GN = SSD_GROUPS * SSD_STATE
SSD_CONV_CH = BR_W + 2 * SSD_GN
SSD_T = 128
MLA_HEADS = 6
MLA_NOPE = 64
MLA_ROPE = 32
MLA_V = 64
MLA_QK = MLA_NOPE + MLA_ROPE
MLA_Q_LORA = 384
MLA_KV_LORA = 256
ROPE_BASE = 10000.0
LOG2E = 1.4426950408889634
N_BRANCH = 4

TM = 512
TQ = 256


def _cp(*sem):
    return pltpu.CompilerParams(dimension_semantics=sem, vmem_limit_bytes=VMEM_LIMIT)


def _full(shape):
    n = len(shape)
    return pl.BlockSpec(shape, lambda *_: (0,) * n)


def _silu(x):
    return x * jax.nn.sigmoid(x)


def _rms(x, n, eps=1e-6):
    return x * lax.rsqrt(jnp.sum(x * x, axis=-1, keepdims=True) * (1.0 / n) + eps)


def _ada_kernel(cc_ref, w_ref, b_ref, o_ref):
    s = _silu(cc_ref[...])
    o_ref[...] = jnp.dot(s, w_ref[...], preferred_element_type=F32, precision=HIGHEST) + b_ref[...]


def ada_table(cc8, w_ada, b_ada):
    depth, d, n = w_ada.shape
    tn = n // 4
    return pl.pallas_call(
        _ada_kernel,
        out_shape=jax.ShapeDtypeStruct((depth, 8, n), F32),
        grid=(depth, n // tn),
        in_specs=[pl.BlockSpec((8, d), lambda l, j: (0, 0)),
                  pl.BlockSpec((None, d, tn), lambda l, j: (l, 0, j)),
                  pl.BlockSpec((None, 1, tn), lambda l, j: (l, 0, j))],
        out_specs=pl.BlockSpec((None, 8, tn), lambda l, j: (l, 0, j)),
        compiler_params=_cp("arbitrary", "arbitrary"),
        name="ada_table",
    )(cc8, w_ada, b_ada.reshape(depth, 1, n))


def _mod_spec(l, midx, col, d):
    return pl.BlockSpec((None, None, 1, d), lambda i: (l, midx(i), 0, col))


def _in_kernel(x_ref, g_ref, sh_ref, sc_ref, w1, w2, w3, w4, w5, h_ref, o1, o2, o3, o4, o5):
    x = x_ref[...]
    y = _rms(x, x.shape[-1]) * g_ref[...]
    hb = (y * (1.0 + sc_ref[...]) + sh_ref[...]).astype(BF16)
    h_ref[...] = hb
    for w, o in ((w1, o1), (w2, o2), (w3, o3), (w4, o4), (w5, o5)):
        o[...] = jnp.dot(hb, w[...], preferred_element_type=F32).astype(o.dtype)


def in_proj(x, g, mod4, l, midx, ws, out_dtypes):
    r, d = x.shape
    in_specs = [pl.BlockSpec((TM, d), lambda i: (i, 0)), _full((1, d)),
                _mod_spec(l, midx, 0, d), _mod_spec(l, midx, 1, d)]
    in_specs += [_full(w.shape) for w in ws]
    out_shape = [jax.ShapeDtypeStruct((r, d), BF16)]
    out_specs = [pl.BlockSpec((TM, d), lambda i: (i, 0))]
    for w, dt in zip(ws, out_dtypes):
        out_shape.append(jax.ShapeDtypeStruct((r, w.shape[1]), dt))
        out_specs.append(pl.BlockSpec((TM, w.shape[1]), lambda i: (i, 0)))
    return pl.pallas_call(
        _in_kernel, out_shape=out_shape, grid=(r // TM,), in_specs=in_specs, out_specs=out_specs,
        compiler_params=_cp("parallel"), name="in_proj",
    )(x, g, mod4, mod4, *ws)


def _s5_disc_kernel(are, aim, ldt, bre, bim, lam_re, lam_im, bbre, bbim):
    a_re, a_im = are[...], aim[...]
    dt = jnp.exp(ldt[...])
    mag = jnp.exp(a_re * dt)
    ang = a_im * dt
    ab_re = mag * jnp.cos(ang)
    ab_im = mag * jnp.sin(ang)
    den = a_re * a_re + a_im * a_im
    f_re = ((ab_re - 1.0) * a_re + ab_im * a_im) / den
    f_im = (ab_im * a_re - (ab_re - 1.0) * a_im) / den
    lam_re[...] = ab_re
    lam_im[...] = ab_im
    for c in range(S5_GROUP):
        bbre[c] = f_re * bre[c] - f_im * bim[c]
        bbim[c] = f_re * bim[c] + f_im * bre[c]


def s5_discretise(a_re, a_im, b_re, b_im, log_dt):
    shp = a_re.shape
    rows = int(np.prod(shp)) // LANE
    are = a_re.astype(F32).reshape(rows, LANE)
    aim = a_im.astype(F32).reshape(rows, LANE)
    ldt = jnp.broadcast_to(log_dt.astype(F32)[..., None], shp).reshape(rows, LANE)
    bre = jnp.moveaxis(b_re.astype(F32), -1, 0).reshape(S5_GROUP, rows, LANE)
    bim = jnp.moveaxis(b_im.astype(F32), -1, 0).reshape(S5_GROUP, rows, LANE)
    outs = pl.pallas_call(
        _s5_disc_kernel,
        out_shape=[jax.ShapeDtypeStruct((rows, LANE), F32)] * 2
        + [jax.ShapeDtypeStruct((S5_GROUP, rows, LANE), F32)] * 2,
        name="s5_discretise",
    )(are, aim, ldt, bre, bim)
    lam_re, lam_im = outs[0].reshape(shp), outs[1].reshape(shp)
    bb_re = jnp.moveaxis(outs[2].reshape((S5_GROUP,) + shp), 0, -1)
    bb_im = jnp.moveaxis(outs[3].reshape((S5_GROUP,) + shp), 0, -1)
    return lam_re, lam_im, bb_re, bb_im


def s5_pack(lam_re, lam_im, bb_re, bb_im, c_re, c_im):
    depth = lam_re.shape[0]
    gpb = LANE // S5_GROUP
    eye = jnp.eye(gpb, dtype=F32)

    def lam_blocks(v):
        return v.reshape(depth, 2, S5_JB, 1, S5_BW)

    lam = jnp.concatenate([lam_blocks(lam_re), lam_blocks(lam_im)], axis=-1)
    lam = jnp.broadcast_to(lam, (depth, 2, S5_JB, SUBLANE, 2 * S5_BW))

    def b_blocks(bb):
        v = bb.reshape(depth, 2, S5_JB, gpb, S5_STATE, S5_GROUP)
        return jnp.einsum('ldjgpc,gh->ldjgchp', v, eye).reshape(depth, 2, S5_JB, LANE, S5_BW)

    bblk = jnp.concatenate([b_blocks(bb_re), b_blocks(bb_im)], axis=-1).astype(BF16)

    def c_blocks(cc):
        v = cc.astype(F32).reshape(depth, 2, S5_JB, gpb, S5_GROUP, S5_STATE)
        return jnp.einsum('ldjgcp,gh->ldjhpgc', v, eye).reshape(depth, 2, S5_JB, S5_BW, LANE)

    cblk = jnp.concatenate([c_blocks(c_re), -c_blocks(c_im)], axis=-2).astype(BF16)
    return lam, bblk, cblk


def _cpow(re, im, n):
    out = None
    while n:
        if n & 1:
            out = (re, im) if out is None else (out[0] * re - out[1] * im, out[0] * im + out[1] * re)
        n >>= 1
        if n:
            re, im = re * re - im * im, 2.0 * re * im
    return out


def _s5_pass_kernel(*refs, tt, lseg, with_y):
    if with_y:
        u_ref, bblk, lam, cblk, sloc, h0, y_ref, hfin, uperm, hst, yperm = refs[:11]
    else:
        u_ref, bblk, lam, s_out, uperm, hst = refs[:6]
    bus = refs[-S5_JB:]
    d = pl.program_id(1)
    j = pl.program_id(2)

    @pl.when(j == 0)
    def _():
        if not with_y:
            hst[...] = jnp.zeros(hst.shape, F32)
        else:
            for jb in range(S5_JB):
                pr, pi = _cpow(lam[jb, 0:1, :S5_BW], lam[jb, 0:1, S5_BW:], lseg)

                def chain(order, jb=jb, pr=pr, pi=pi):
                    cr, ci = h0[jb, :, :S5_BW], h0[jb, :, S5_BW:]
                    for s in order:
                        hst[jb, s:s + 1, :S5_BW] = cr
                        hst[jb, s:s + 1, S5_BW:] = ci
                        sr, si = sloc[jb, s:s + 1, :S5_BW], sloc[jb, s:s + 1, S5_BW:]
                        cr, ci = pr * cr - pi * ci + sr, pr * ci + pi * cr + si
                    hfin[jb, :, :S5_BW] = cr
                    hfin[jb, :, S5_BW:] = ci

                @pl.when(d == 0)
                def _():
                    chain(range(S5_NSEG))

                @pl.when(d == 1)
                def _():
                    chain(range(S5_NSEG - 1, -1, -1))

    for s in range(S5_NSEG):
        us = u_ref[s].astype(F32)
        for k in range(S5_JB):
            uperm[k, pl.ds(s, tt, stride=S5_NSEG), :] = us[:, k * LANE:(k + 1) * LANE]

    for jb in range(S5_JB):
        bus[jb][...] = jnp.dot(uperm[jb].astype(BF16), bblk[jb], preferred_element_type=F32)

    for jb in range(S5_JB):
        bu = bus[jb]
        lr, li = lam[jb, :, :S5_BW], lam[jb, :, S5_BW:]

        def step(i, carry, lr=lr, li=li, bu=bu):
            hr, hi = carry
            t = i + d * (tt - 1 - 2 * i)
            r0 = pl.multiple_of(t * S5_NSEG, S5_NSEG)
            nr = lr * hr - li * hi + bu[pl.ds(r0, S5_NSEG), :S5_BW]
            ni = lr * hi + li * hr + bu[pl.ds(r0, S5_NSEG), S5_BW:]
            if with_y:
                bu[pl.ds(r0, S5_NSEG), :S5_BW] = nr
                bu[pl.ds(r0, S5_NSEG), S5_BW:] = ni
            return nr, ni

        hr, hi = lax.fori_loop(0, tt, step, (hst[jb, :, :S5_BW], hst[jb, :, S5_BW:]), unroll=True)
        hst[jb, :, :S5_BW] = hr
        hst[jb, :, S5_BW:] = hi

    if with_y:
        for jb in range(S5_JB):
            yperm[jb] = jnp.dot(bus[jb][...].astype(BF16), cblk[jb], preferred_element_type=F32)
        for s in range(S5_NSEG):
            for k in range(S5_JB):
                y_ref[s, :, k * LANE:(k + 1) * LANE] = (
                    yperm[k, pl.ds(s, tt, stride=S5_NSEG), :].astype(y_ref.dtype))
    else:
        s_out[...] = hst[...]


def s5_scan(u, bblk, lam, cblk, l, bsz, h0):
    seq_len = u.shape[0] // bsz
    lseg = seq_len // S5_NSEG
    tt = min(lseg, 128)
    nt = lseg // tt
    u5 = u.reshape(bsz, S5_NSEG, lseg, BR_W)

    def tile(d, j):
        return j + d * (nt - 1 - 2 * j)

    u_spec = pl.BlockSpec((None, S5_NSEG, tt, BR_W), lambda b, d, j: (b, 0, tile(d, j), 0))
    y_spec = pl.BlockSpec((None, None, S5_NSEG, tt, BR_W), lambda b, d, j: (d, b, 0, tile(d, j), 0))
    w_b = pl.BlockSpec((None, None, S5_JB, LANE, 2 * S5_BW), lambda b, d, j: (l, d, 0, 0, 0))
    w_lam = pl.BlockSpec((None, None, S5_JB, SUBLANE, 2 * S5_BW), lambda b, d, j: (l, d, 0, 0, 0))
    w_c = pl.BlockSpec((None, None, S5_JB, 2 * S5_BW, LANE), lambda b, d, j: (l, d, 0, 0, 0))
    st8 = pl.BlockSpec((None, None, S5_JB, SUBLANE, 2 * S5_BW), lambda b, d, j: (b, d, 0, 0, 0))
    st1 = pl.BlockSpec((None, None, S5_JB, 1, 2 * S5_BW), lambda b, d, j: (b, d, 0, 0, 0))
    n_rows = S5_NSEG * tt
    scratch = [pltpu.VMEM((S5_JB, n_rows, LANE), F32), pltpu.VMEM((S5_JB, SUBLANE, 2 * S5_BW), F32)]
    bus = [pltpu.VMEM((n_rows, 2 * S5_BW), F32)] * S5_JB
    grid = (bsz, 2, nt)
    cp = _cp("arbitrary", "arbitrary", "arbitrary")

    sloc = pl.pallas_call(
        functools.partial(_s5_pass_kernel, tt=tt, lseg=lseg, with_y=False),
        out_shape=jax.ShapeDtypeStruct((bsz, 2, S5_JB, SUBLANE, 2 * S5_BW), F32),
        grid=grid, in_specs=[u_spec, w_b, w_lam], out_specs=st8,
        scratch_shapes=scratch + bus, compiler_params=cp, name="s5_local",
    )(u5, bblk, lam)

    y, hfin = pl.pallas_call(
        functools.partial(_s5_pass_kernel, tt=tt, lseg=lseg, with_y=True),
        out_shape=[jax.ShapeDtypeStruct((2, bsz, S5_NSEG, lseg, BR_W), BF16),
                   jax.ShapeDtypeStruct((bsz, 2, S5_JB, 1, 2 * S5_BW), F32)],
        grid=grid, in_specs=[u_spec, w_b, w_lam, w_c, st8, st1], out_specs=[y_spec, st1],
        scratch_shapes=scratch + [pltpu.VMEM((S5_JB, n_rows, LANE), F32)] + bus,
        compiler_params=cp, name="s5_emit",
    )(u5, bblk, lam, cblk, sloc, h0)
    return y.reshape(2, bsz * seq_len, BR_W), hfin


def _sgu_kernel(z_ref, g_ref, b_ref, w_ref, bias_ref, o_ref):
    z = jax.nn.gelu(z_ref[...].astype(F32))
    u, v = z[:, :BR_W], z[:, BR_W:]
    mu = jnp.mean(v, axis=-1, keepdims=True)
    vc = v - mu
    vn = vc * lax.rsqrt(jnp.mean(vc * vc, axis=-1, keepdims=True) + 1e-5) * g_ref[...] + b_ref[...]
    n_chunk = z.shape[0] // SGU_CHUNK
    half = LANE // 2
    lane = lax.broadcasted_iota(jnp.int32, (SGU_CHUNK, LANE), 1)
    for k in range(BR_W // LANE):
        cols = []
        for c in range(n_chunk):
            blk = vn[c * SGU_CHUNK:(c + 1) * SGU_CHUNK, k * LANE:(k + 1) * LANE]
            lo = jnp.where(lane < half, blk, 0.0)
            cols.append(jnp.concatenate([lo, blk - lo], axis=0))
        rhs = jnp.concatenate(cols, axis=1).astype(BF16)
        mixed = jnp.dot(w_ref[k], rhs, preferred_element_type=F32)
        for c in range(n_chunk):
            rows = slice(c * SGU_CHUNK, (c + 1) * SGU_CHUNK)
            m = mixed[:, c * LANE:(c + 1) * LANE] + bias_ref[:, k * LANE:(k + 1) * LANE]
            o_ref[rows, k * LANE:(k + 1) * LANE] = (u[rows, k * LANE:(k + 1) * LANE] * m).astype(o_ref.dtype)


def sgu(z, ln_g, ln_b, w_pair, bias):
    r = z.shape[0]
    return pl.pallas_call(
        _sgu_kernel, out_shape=jax.ShapeDtypeStruct((r, BR_W), BF16), grid=(r // TM,),
        in_specs=[pl.BlockSpec((TM, 2 * BR_W), lambda i: (i, 0)), _full((1, BR_W)), _full((1, BR_W)),
                  _full(w_pair.shape), _full(bias.shape)],
        out_specs=pl.BlockSpec((TM, BR_W), lambda i: (i, 0)),
        compiler_params=_cp("parallel"), name="sgu",
    )(z, ln_g, ln_b, w_pair, bias)


def _ssd_prep_kernel(cur_ref, prev_ref, next_ref, dt_ref, w_ref, b_ref, dtb_ref, alane_ref, esel_ref,
                     xs_ref, bc_ref, da_ref, xt_ref, *, seq_len):
    i = pl.program_id(0)
    x = cur_ref[:, BR_W:].astype(F32)
    tm = x.shape[0]
    row = lax.broadcasted_iota(jnp.int32, (tm, 1), 0)
    pos = lax.rem(row + i * tm, seq_len)
    prev_row = prev_ref[SUBLANE - 1:SUBLANE, BR_W:].astype(F32)
    next_row = next_ref[0:1, BR_W:].astype(F32)
    x_prev = jnp.where(row == 0, prev_row, pltpu.roll(x, 1, axis=0))
    x_prev = jnp.where(pos == 0, 0.0, x_prev)
    x_next = jnp.where(row == tm - 1, next_row, pltpu.roll(x, tm - 1, axis=0))
    x_next = jnp.where(pos == seq_len - 1, 0.0, x_next)
    y = _silu(w_ref[0:1, :] * x_prev + w_ref[1:2, :] * x + w_ref[2:3, :] * x_next + b_ref[...])
    xs = y[:, :BR_W]
    xs_ref[...] = xs
    bc_ref[...] = y[:, BR_W:].astype(bc_ref.dtype)
    t = dt_ref[...] + dtb_ref[...]
    sp = jnp.maximum(t, 0.0) + jnp.log1p(jnp.exp(-jnp.abs(t)))
    lane = lax.broadcasted_iota(jnp.int32, t.shape, 1)
    d_a = sp * alane_ref[...]
    da_ref[0] = jnp.where(lane < SSD_HEADS, d_a, 0.0)
    da_ref[1] = jnp.where(lane < SSD_HEADS, pltpu.roll(d_a, LANE - SSD_HEADS, axis=1), 0.0)
    sp_hi = sp.astype(BF16)
    sp_lo = (sp - sp_hi.astype(F32)).astype(BF16)
    for d in range(2):
        dt_x = (jnp.dot(sp_hi, esel_ref[d], preferred_element_type=F32)
                + jnp.dot(sp_lo, esel_ref[d], preferred_element_type=F32))
        xt_ref[d] = (xs * dt_x).astype(xt_ref.dtype)


def ssd_prep(p_ssd, dt_raw, seq_len, conv_w, conv_b, dt_bias, a_lane, esel):
    r, w = p_ssd.shape
    nb = r // SUBLANE
    per = TM // SUBLANE
    return pl.pallas_call(
        functools.partial(_ssd_prep_kernel, seq_len=seq_len),
        out_shape=[jax.ShapeDtypeStruct((r, BR_W), F32), jax.ShapeDtypeStruct((r, 2 * SSD_GN), BF16),
                   jax.ShapeDtypeStruct((2, r, LANE), F32), jax.ShapeDtypeStruct((2, r, BR_W), BF16)],
        grid=(r // TM,),
        in_specs=[pl.BlockSpec((TM, w), lambda i: (i, 0)),
                  pl.BlockSpec((SUBLANE, w), lambda i: (jnp.maximum(i * per - 1, 0), 0)),
                  pl.BlockSpec((SUBLANE, w), lambda i: (jnp.minimum((i + 1) * per, nb - 1), 0)),
                  pl.BlockSpec((TM, LANE), lambda i: (i, 0)),
                  _full(conv_w.shape), _full(conv_b.shape), _full(dt_bias.shape), _full(a_lane.shape),
                  _full(esel.shape)],
        out_specs=[pl.BlockSpec((TM, BR_W), lambda i: (i, 0)), pl.BlockSpec((TM, 2 * SSD_GN), lambda i: (i, 0)),
                   pl.BlockSpec((2, TM, LANE), lambda i: (0, i, 0)), pl.BlockSpec((2, TM, BR_W), lambda i: (0, i, 0))],
        compiler_params=_cp("parallel"), name="ssd_prep",
    )(p_ssd, p_ssd, p_ssd, dt_raw, conv_w, conv_b, dt_bias, a_lane, esel)


def _ssd_scan_kernel(xt_ref, bc_ref, da_ref, tri_ref, h0_ref, y_ref, hfin_ref, st_ref, *, n_chunk, bsz):
    d = pl.program_id(0)
    j = pl.program_id(1)
    T = SSD_T

    @pl.when(j == 0)
    def _():
        st_ref[...] = h0_ref[...]

    tri = tri_ref[...]
    mask = tri > 0.5
    rep = SSD_HEADS // SSD_GROUPS

    def chunk(ci, carry):
        c = ci + d * (n_chunk - 1 - 2 * ci)
        r0 = pl.multiple_of(c * T, T)
        for b in range(bsz):
            d_a = da_ref[b, pl.ds(r0, T), :]
            a_cum = jnp.dot(tri, d_a, preferred_element_type=F32, precision=HIGHEST)
            a_cum_t = a_cum.T
            total = jnp.sum(d_a, axis=0, keepdims=True)
            xt = xt_ref[b, pl.ds(r0, T), :]
            bc = bc_ref[b, pl.ds(r0, T), :]
            bm_t = bc[:, :SSD_GN].astype(F32).T
            cm = bc[:, SSD_GN:]
            for g in range(SSD_GROUPS):
                gs = slice(g * SSD_STATE, (g + 1) * SSD_STATE)
                b_t = bm_t[gs, :]
                c_g = cm[:, gs]
                scores = jnp.dot(c_g, b_t.astype(BF16), preferred_element_type=F32)
                for hh in range(rep):
                    h = g * rep + hh
                    hs = slice(h * SSD_HEAD_DIM, (h + 1) * SSD_HEAD_DIM)
                    col = a_cum[:, h:h + 1]
                    rowv = a_cum_t[h:h + 1, :]
                    tot = total[:, h:h + 1]
                    decay = jnp.exp(jnp.where(mask, col - rowv, -1e30))
                    p = (scores * decay).astype(BF16)
                    x_h = xt[:, hs]
                    s_old = st_ref[b, h]
                    y_h = jnp.dot(p, x_h, preferred_element_type=F32)
                    y_h += jnp.dot(c_g, s_old.astype(BF16), preferred_element_type=F32) * jnp.exp(col)
                    bw = (b_t * jnp.exp(tot - rowv)).astype(BF16)
                    st_ref[b, h] = s_old * jnp.exp(tot) + jnp.dot(bw, x_h, preferred_element_type=F32)
                    y_ref[b, pl.ds(r0, T), hs] = y_h.astype(y_ref.dtype)
        return carry

    lax.fori_loop(0, n_chunk, chunk, 0)
    hfin_ref[...] = st_ref[...]


def ssd_scan(xt, bc, d_a, tri, bsz, h0):
    r = bc.shape[0]
    seq_len = r // bsz
    ts = min(seq_len, TM)
    nt = seq_len // ts

    def tile(d, j):
        return j + d * (nt - 1 - 2 * j)

    st = pl.BlockSpec((None, bsz, SSD_HEADS, SSD_STATE, SSD_HEAD_DIM), lambda d, j: (d, 0, 0, 0, 0))
    y, hfin = pl.pallas_call(
        functools.partial(_ssd_scan_kernel, n_chunk=ts // SSD_T, bsz=bsz),
        out_shape=[jax.ShapeDtypeStruct((2, bsz, seq_len, BR_W), BF16),
                   jax.ShapeDtypeStruct((2, bsz, SSD_HEADS, SSD_STATE, SSD_HEAD_DIM), F32)],
        grid=(2, nt),
        in_specs=[pl.BlockSpec((None, bsz, ts, BR_W), lambda d, j: (d, 0, tile(d, j), 0)),
                  pl.BlockSpec((bsz, ts, 2 * SSD_GN), lambda d, j: (0, tile(d, j), 0)),
                  pl.BlockSpec((None, bsz, ts, LANE), lambda d, j: (d, 0, tile(d, j), 0)),
                  pl.BlockSpec((None, SSD_T, SSD_T), lambda d, j: (d, 0, 0)), st],
        out_specs=[pl.BlockSpec((None, bsz, ts, BR_W), lambda d, j: (d, 0, tile(d, j), 0)), st],
        scratch_shapes=[pltpu.VMEM((bsz, SSD_HEADS, SSD_STATE, SSD_HEAD_DIM), F32)],
        compiler_params=_cp("arbitrary", "arbitrary"), name="ssd_scan",
    )(xt.reshape(2, bsz, seq_len, BR_W), bc.reshape(bsz, seq_len, 2 * SSD_GN),
      d_a.reshape(2, bsz, seq_len, LANE), tri, h0)
    return y.reshape(2, r, BR_W), hfin


def _rope_table_kernel(ang_ref, cos_ref, sin_ref):
    ang = ang_ref[...]
    lane = lax.broadcasted_iota(jnp.int32, ang.shape, 1)
    quarter = MLA_ROPE // 2
    first = (lane >= MLA_NOPE) & (lane < MLA_NOPE + quarter)
    second = (lane >= MLA_NOPE + quarter) & (lane < MLA_QK)
    c, s = jnp.cos(ang), jnp.sin(ang)
    cos_ref[...] = jnp.where(first | second, c, jnp.where(lane < MLA_NOPE, 1.0, 0.0))
    sin_ref[...] = jnp.where(first, -s, jnp.where(second, s, 0.0))


def _rope_partner(v):
    quarter = MLA_ROPE // 2
    src = np.arange(LANE)
    src[MLA_NOPE:MLA_NOPE + quarter] += quarter
    src[MLA_NOPE + quarter:MLA_QK] -= quarter
    valid = (np.arange(LANE) >= MLA_NOPE) & (np.arange(LANE) < MLA_QK)
    return jnp.where(valid, v[..., src], 0)


def rope_tables(seq_len):
    n_rows = seq_len // GRID_W
    pairs = MLA_ROPE // 4
    pos_row = jnp.repeat(jnp.arange(n_rows, dtype=F32), GRID_W)
    pos_col = jnp.tile(jnp.arange(GRID_W, dtype=F32), n_rows)
    inv_freq = ROPE_BASE ** (-jnp.arange(pairs, dtype=F32) / pairs)
    ang = jnp.concatenate([pos_row[:, None] * inv_freq, pos_col[:, None] * inv_freq], axis=-1)
    ang_pad = jnp.concatenate([jnp.zeros((seq_len, MLA_NOPE), F32), ang, ang,
                               jnp.zeros((seq_len, LANE - MLA_QK), F32)], axis=-1)
    tr = min(seq_len, 1024)
    spec = pl.BlockSpec((tr, LANE), lambda i: (i, 0))
    return pl.pallas_call(
        _rope_table_kernel, out_shape=[jax.ShapeDtypeStruct((seq_len, LANE), F32)] * 2,
        grid=(seq_len // tr,), in_specs=[spec], out_specs=[spec] * 2,
        compiler_params=_cp("parallel"), name="rope_tables",
    )(ang_pad)


def _mla_prep_kernel(*refs, rope):
    if rope:
        (p_ref, gq_ref, gkv_ref, wq_ref, wkv_ref, nq_ref, nk_ref, wqs_ref, nqs_ref, nks_ref, cos_ref, sin_ref,
         q_ref, k_ref, v_ref) = refs
    else:
        p_ref, gq_ref, gkv_ref, wq_ref, wkv_ref, nq_ref, nk_ref, q_ref, k_ref, v_ref = refs
    p = p_ref[...].astype(F32)
    cq = (_rms(p[:, :MLA_Q_LORA], MLA_Q_LORA) * gq_ref[...]).astype(BF16)
    ckv = (_rms(p[:, MLA_Q_LORA:MLA_Q_LORA + MLA_KV_LORA], MLA_KV_LORA) * gkv_ref[...]).astype(BF16)
    kr = p[:, MLA_Q_LORA + MLA_KV_LORA:MLA_Q_LORA + MLA_KV_LORA + LANE]
    q_all = jnp.dot(cq, wq_ref[...], preferred_element_type=F32)
    kv_all = jnp.dot(ckv, wkv_ref[...], preferred_element_type=F32)
    lane = lax.broadcasted_iota(jnp.int32, (1, LANE), 1)
    one_col = jnp.where(lane == MLA_V, 1.0, 0.0)
    scale = MLA_QK ** -0.5 * LOG2E
    if rope:
        qs_all = jnp.dot(cq, wqs_ref[...], preferred_element_type=F32)
        q_cos, q_sin = nq_ref[...] * cos_ref[...], nqs_ref[...] * sin_ref[...]
        k_cos = nk_ref[...] * cos_ref[...]
        k_part = p[:, MLA_Q_LORA + MLA_KV_LORA + LANE:] * (nks_ref[...] * sin_ref[...])

    def inv_rms(t):
        return lax.rsqrt(jnp.sum(t * t, axis=-1, keepdims=True) * (1.0 / MLA_QK) + 1e-6)

    for h in range(MLA_HEADS):
        q = q_all[:, h * LANE:(h + 1) * LANE]
        k = kv_all[:, 2 * h * LANE:(2 * h + 1) * LANE] + kr
        if rope:
            q_out = (q * q_cos + qs_all[:, h * LANE:(h + 1) * LANE] * q_sin) * (inv_rms(q) * scale)
            k_out = (k * k_cos + k_part) * inv_rms(k)
        else:
            q_out = q * nq_ref[...] * (inv_rms(q) * scale)
            k_out = k * nk_ref[...] * inv_rms(k)
        q_ref[h] = q_out.astype(q_ref.dtype)
        k_ref[h] = k_out.astype(k_ref.dtype)
        v_ref[h] = (kv_all[:, (2 * h + 1) * LANE:(2 * h + 2) * LANE] + one_col).astype(v_ref.dtype)


def mla_prep(p_mla, bsz, gq, gkv, wq, wkv, nq, nk, rope_args):
    r, w = p_mla.shape
    seq_len = r // bsz
    tr = min(seq_len, TM)
    nt = seq_len // tr
    rope = rope_args is not None
    in_specs = [pl.BlockSpec((tr, w), lambda b, i: (b * nt + i, 0)), _full(gq.shape), _full(gkv.shape),
                _full(wq.shape), _full(wkv.shape), _full(nq.shape), _full(nk.shape)]
    args = [p_mla, gq, gkv, wq, wkv, nq, nk]
    if rope:
        in_specs += [_full(a.shape) for a in rope_args[:3]]
        in_specs += [pl.BlockSpec((tr, LANE), lambda b, i: (i, 0))] * 2
        args += list(rope_args)
    head = pl.BlockSpec((None, MLA_HEADS, tr, LANE), lambda b, i: (b, 0, i, 0))
    return pl.pallas_call(
        functools.partial(_mla_prep_kernel, rope=rope),
        out_shape=[jax.ShapeDtypeStruct((bsz, MLA_HEADS, seq_len, LANE), BF16)] * 3,
        grid=(bsz, nt), in_specs=in_specs, out_specs=[head] * 3,
        compiler_params=_cp("parallel", "parallel"), name="mla_prep",
    )(*args)


def _attn_kernel(*refs, n_kv):
    q_ref = refs[0]
    kv_refs = refs[1:1 + 2 * n_kv]
    o_ref = refs[1 + 2 * n_kv]
    scores = [[lax.dot_general(q_ref[hh], kv_refs[2 * i][hh], (((1,), (1,)), ((), ())),
                               preferred_element_type=F32) for i in range(n_kv)] for hh in range(2)]
    for hh in range(2):
        s = scores[hh]
        m = s[0].max(axis=-1, keepdims=True)
        for si in s[1:]:
            m = jnp.maximum(m, si.max(axis=-1, keepdims=True))
        acc = None
        for i, si in enumerate(s):
            pv = jnp.dot(jnp.exp2((si - m).astype(BF16)), kv_refs[2 * i + 1][hh], preferred_element_type=F32)
            acc = pv if acc is None else acc + pv
        o = acc[:, :MLA_V] / acc[:, MLA_V:MLA_V + 1]
        o_ref[:, hh * MLA_V:(hh + 1) * MLA_V] = o.astype(o_ref.dtype)


def attention(q, kvs):
    bsz, nh, lq, _ = q.shape
    tq = min(lq, TQ)
    nq = lq // tq
    in_specs = [pl.BlockSpec((None, 2, tq, LANE), lambda b, hp, i: (b, hp, i, 0))]
    args = [q]
    for k, v in kvs:
        spec = pl.BlockSpec((None, 2, k.shape[2], LANE), lambda b, hp, i: (b, hp, 0, 0))
        in_specs += [spec, spec]
        args += [k, v]
    return pl.pallas_call(
        functools.partial(_attn_kernel, n_kv=len(kvs)),
        out_shape=jax.ShapeDtypeStruct((bsz * lq, nh * MLA_V), BF16),
        grid=(bsz, nh // 2, nq), in_specs=in_specs,
        out_specs=pl.BlockSpec((tq, 2 * MLA_V), lambda b, hp, i: (b * nq + i, hp)),
        compiler_params=_cp("parallel", "parallel", "arbitrary"), name="attention",
    )(*args)


def _merge_kernel(x_ref, h_ref, u_ref, s5f_ref, s5b_ref, s5d_ref, wglu_ref, b_ref,
                  xs_ref, ssdf_ref, ssdb_ref, z_ref, ssdd_ref, ssdg_ref, d_ref,
                  g1_ref, wg_ref, wb_ref, wo_ref, o_ref):
    y = s5d_ref[...] * u_ref[...].astype(F32) + s5f_ref[...].astype(F32) + s5b_ref[...].astype(F32)
    g = jax.nn.gelu(y)
    a = (g * jax.nn.sigmoid(jnp.dot(g.astype(BF16), wglu_ref[...], preferred_element_type=F32))).astype(BF16)
    y = ssdd_ref[...] * xs_ref[...] + ssdf_ref[...].astype(F32) + ssdb_ref[...].astype(F32)
    c = (_rms(y * _silu(z_ref[...].astype(F32)), BR_W) * ssdg_ref[...]).astype(BF16)
    h = h_ref[...]
    dm = x_ref.shape[-1]
    merged = None
    for i, br in enumerate((a, b_ref[...], c, d_ref[...])):
        gate = jax.nn.sigmoid(jnp.dot(h, wg_ref[:, i * dm:(i + 1) * dm], preferred_element_type=F32))
        term = gate * jnp.dot(br, wb_ref[i], preferred_element_type=F32)
        merged = term if merged is None else merged + term
    mix = jnp.dot(merged.astype(BF16), wo_ref[...], preferred_element_type=F32)
    o_ref[...] = x_ref[...] + g1_ref[...] * mix


def merge(x, h, s5_in, b, ssd_in, d, mod4, l, midx, wg, wb, wo):
    r, dm = x.shape
    row = pl.BlockSpec((TM, dm), lambda i: (i, 0))
    br = pl.BlockSpec((TM, BR_W), lambda i: (i, 0))
    fwd = pl.BlockSpec((None, TM, BR_W), lambda i: (0, i, 0))
    bwd = pl.BlockSpec((None, TM, BR_W), lambda i: (1, i, 0))
    vec = _full((1, BR_W))
    u, y_s5, s5_d, w_glu = s5_in
    xs, y_ssd, p_ssd, ssd_d, ssd_g = ssd_in
    return pl.pallas_call(
        _merge_kernel, out_shape=jax.ShapeDtypeStruct((r, dm), F32), grid=(r // TM,),
        in_specs=[row, row, br, fwd, bwd, vec, _full(w_glu.shape), br,
                  br, fwd, bwd, br, vec, vec, br,
                  _mod_spec(l, midx, 2, dm), _full(wg.shape), _full(wb.shape), _full(wo.shape)],
        out_specs=row, compiler_params=_cp("parallel"), name="merge",
    )(x, h, u, y_s5, y_s5, s5_d, w_glu, b, xs, y_ssd, y_ssd, p_ssd, ssd_d, ssd_g, d, mod4, wg, wb, wo)


def _ffn_kernel(x_ref, g_ref, sh_ref, sc_ref, g2_ref, wi_ref, wo_ref, o_ref, *, n_split):
    x = x_ref[...]
    y = _rms(x, x.shape[-1]) * g_ref[...]
    hb = (y * (1.0 + sc_ref[...]) + sh_ref[...]).astype(BF16)
    hid = wo_ref.shape[0]
    step = hid // n_split
    acc = None
    for c in range(n_split):
        gate = jnp.dot(hb, wi_ref[:, c * step:(c + 1) * step], preferred_element_type=F32)
        up = jnp.dot(hb, wi_ref[:, hid + c * step:hid + (c + 1) * step], preferred_element_type=F32)
        part = jnp.dot((_silu(gate) * up).astype(BF16), wo_ref[c * step:(c + 1) * step, :],
                       preferred_element_type=F32)
        acc = part if acc is None else acc + part
    o_ref[...] = x + g2_ref[...] * acc


def ffn(x, g, mod4, l, midx, wi, wo):
    r, d = x.shape
    row = pl.BlockSpec((TM, d), lambda i: (i, 0))
    n_split = 11 if wo.shape[0] % (11 * LANE) == 0 else 1
    return pl.pallas_call(
        functools.partial(_ffn_kernel, n_split=n_split),
        out_shape=jax.ShapeDtypeStruct((r, d), F32), grid=(r // TM,),
        in_specs=[row, _full((1, d)), _mod_spec(l, midx, 3, d), _mod_spec(l, midx, 4, d),
                  _mod_spec(l, midx, 5, d), _full(wi.shape), _full(wo.shape)],
        out_specs=row, compiler_params=_cp("parallel"), name="ffn",
    )(x, g, mod4, mod4, mod4, wi, wo)


def _pad_cols(w, n):
    return jnp.pad(w, ((0, 0),) * (w.ndim - 1) + ((0, n - w.shape[-1]),))


def kernel(x, c, ctx, c_ctx, w_ada, b_ada, norm1_g, norm2_g, w_in, s5_a_re, s5_a_im, s5_b_re, s5_b_im, s5_c_re, s5_c_im, s5_log_dt, s5_d, s5_w_glu, sgu_ln_g, sgu_ln_b, sgu_w_s, sgu_b_s, ssd_conv_w, ssd_conv_b, ssd_a_log, ssd_dt_bias, ssd_d, ssd_norm_g, mla_q_a_norm, mla_w_uq, mla_kv_a_norm, mla_w_ukv, mla_q_norm, mla_k_norm, w_branch, w_out, w_ffn_in, w_ffn_out):
    bsz, seq, dm = x.shape
    lc = ctx.shape[1]
    depth = w_ada.shape[0]
    assert seq % TM == 0 and (bsz * lc) % TM == 0 and seq % (S5_NSEG * SUBLANE) == 0
    assert lc % SSD_T == 0 and lc % (S5_NSEG * SUBLANE) == 0 and bsz + 1 <= 8

    cc8 = jnp.zeros((8, dm), F32).at[:bsz].set(c.astype(F32)).at[bsz].set(c_ctx.astype(F32))
    mod4 = ada_table(cc8, w_ada.astype(F32), b_ada.astype(F32)).reshape(depth, 8, 1, 6 * dm)
    lat_tiles = seq // TM
    midx_lat = lambda i: i // lat_tiles
    midx_ctx = lambda i: bsz

    off = np.cumsum([0, BR_W, 2 * BR_W, BR_W + SSD_CONV_CH + 2 * SSD_HEADS,
                     MLA_Q_LORA + MLA_KV_LORA + MLA_ROPE, N_BRANCH * dm])
    w_s5 = w_in[:, :, off[0]:off[1]].astype(BF16)
    w_sgu = w_in[:, :, off[1]:off[2]].astype(BF16)
    w_ssd = w_in[:, :, off[2]:off[2] + BR_W + SSD_CONV_CH].astype(BF16)
    w_dt = _pad_cols(w_in[:, :, off[2] + BR_W + SSD_CONV_CH:off[3]], LANE).astype(BF16)
    w_mla_main = w_in[:, :, off[3]:off[3] + MLA_Q_LORA + MLA_KV_LORA]
    w_kr = w_in[:, :, off[3] + MLA_Q_LORA + MLA_KV_LORA:off[4]]
    zeros = lambda n: jnp.zeros((depth, dm, n), w_in.dtype)
    kr_block = jnp.concatenate([zeros(MLA_NOPE), w_kr, zeros(LANE - MLA_QK)], axis=-1)
    w_mla = jnp.concatenate([w_mla_main, kr_block, _rope_partner(kr_block)], axis=-1).astype(BF16)
    w_gate = w_in[:, :, off[4]:off[5]].astype(BF16)

    lam_re, lam_im, bb_re, bb_im = s5_discretise(s5_a_re, s5_a_im, s5_b_re, s5_b_im, s5_log_dt)
    lam, bblk, cblk = s5_pack(lam_re, lam_im, bb_re, bb_im, s5_c_re, s5_c_im)
    s5_wg = s5_w_glu.astype(BF16)

    sgu_w = sgu_w_s.reshape(depth, SGU_HEADS // 2, 2, SGU_CHUNK, SGU_CHUNK)
    sgu_w = jnp.concatenate([sgu_w[:, :, 0], sgu_w[:, :, 1]], axis=-1).astype(BF16)
    sgu_b = jnp.repeat(jnp.swapaxes(sgu_b_s, 1, 2), BR_W // SGU_HEADS, axis=-1).astype(F32)

    a_neg = -jnp.exp(ssd_a_log.astype(F32))
    a_lane = _pad_cols(a_neg.reshape(depth, 1, 2 * SSD_HEADS), LANE)
    head_of_lane = jnp.arange(LANE)[None, :, None] - SSD_HEADS * jnp.arange(2)[:, None, None]
    col_head = jnp.arange(BR_W) // SSD_HEAD_DIM
    esel = ((head_of_lane == col_head[None, None, :]) & (head_of_lane >= 0)).astype(BF16)
    ti = jnp.arange(SSD_T)
    tri = jnp.stack([ti[None, :] <= ti[:, None], ti[None, :] >= ti[:, None]]).astype(F32)
    dt_bias = _pad_cols(ssd_dt_bias.astype(F32).reshape(depth, 1, 2 * SSD_HEADS), LANE)
    ssd_dskip = jnp.repeat(ssd_d.astype(F32), SSD_HEAD_DIM, axis=-1)[:, None, :]

    wq = mla_w_uq.reshape(depth, MLA_Q_LORA, MLA_HEADS, MLA_QK)
    wq = _pad_cols(wq, LANE)
    wq_sw = _rope_partner(wq).reshape(depth, MLA_Q_LORA, MLA_HEADS * LANE).astype(BF16)
    wq = wq.reshape(depth, MLA_Q_LORA, MLA_HEADS * LANE).astype(BF16)
    wkv = mla_w_ukv.reshape(depth, MLA_KV_LORA, MLA_HEADS, 2, MLA_NOPE)
    wkv = _pad_cols(wkv, LANE).reshape(depth, MLA_KV_LORA, MLA_HEADS * 2 * LANE).astype(BF16)
    nq = _pad_cols(mla_q_norm.astype(F32), LANE)[:, None, :]
    nk = _pad_cols(mla_k_norm.astype(F32), LANE)[:, None, :]
    nq_sw, nk_sw = _rope_partner(nq), _rope_partner(nk)
    rope_cos, rope_sin = rope_tables(seq)

    wb = w_branch.astype(BF16)
    wo = w_out.astype(BF16)
    wfi = w_ffn_in.astype(BF16)
    wfo = w_ffn_out.astype(BF16)

    row = lambda v, l: v[l].astype(F32).reshape(1, -1)
    x_lat = x.astype(F32).reshape(bsz * seq, dm)
    x_ctx = ctx.astype(F32).reshape(bsz * lc, dm)
    s5_zero = jnp.zeros((bsz, 2, S5_JB, 1, 2 * S5_BW), F32)
    ssd_zero = jnp.zeros((2, bsz, SSD_HEADS, SSD_STATE, SSD_HEAD_DIM), F32)

    for l in range(depth):
        need_ctx = l < depth - 1
        ws = (w_s5[l], w_sgu[l], w_ssd[l], w_dt[l], w_mla[l])
        dts = (BF16, BF16, BF16, F32, BF16)
        g1 = row(norm1_g, l)
        h_l, u_l, z_l, p_l, dtr_l, m_l = in_proj(x_lat, g1, mod4, l, midx_lat, ws, dts)
        h_c, u_c, z_c, p_c, dtr_c, m_c = in_proj(x_ctx, g1, mod4, l, midx_ctx, ws, dts)

        y_c, s5_h = s5_scan(u_c, bblk, lam, cblk, l, bsz, s5_zero)
        y_l, _ = s5_scan(u_l, bblk, lam, cblk, l, bsz, s5_h)
        s5_tail = (row(s5_d, l), s5_wg[l])
        b_l = sgu(z_l, row(sgu_ln_g, l), row(sgu_ln_b, l), sgu_w[l], sgu_b[l])
        conv_w, conv_b = ssd_conv_w[l].astype(F32), row(ssd_conv_b, l)
        xs_c, bc_c, da_c, xt_c = ssd_prep(p_c, dtr_c, lc, conv_w, conv_b, dt_bias[l], a_lane[l], esel)
        xs_l, bc_l, da_l, xt_l = ssd_prep(p_l, dtr_l, seq, conv_w, conv_b, dt_bias[l], a_lane[l], esel)
        yc_c, ssd_h = ssd_scan(xt_c, bc_c, da_c, tri, bsz, ssd_zero)
        yc_l, _ = ssd_scan(xt_l, bc_l, da_l, tri, bsz, ssd_h)
        ssd_tail = (ssd_dskip[l], row(ssd_norm_g, l))
        mla_w = (row(mla_q_a_norm, l), row(mla_kv_a_norm, l), wq[l], wkv[l], nq[l], nk[l])
        q_c, k_c, v_c = mla_prep(m_c, bsz, *mla_w, None)
        q_l, k_l, v_l = mla_prep(m_l, bsz, *mla_w, (wq_sw[l], nq_sw[l], nk_sw[l], rope_cos, rope_sin))
        d_l = attention(q_l, [(k_l, v_l), (k_c, v_c)])

        x_lat = merge(x_lat, h_l, (u_l, y_l) + s5_tail, b_l, (xs_l, yc_l, p_l) + ssd_tail, d_l,
                      mod4, l, midx_lat, w_gate[l], wb[l], wo[l])
        x_lat = ffn(x_lat, row(norm2_g, l), mod4, l, midx_lat, wfi[l], wfo[l])
        if need_ctx:
            b_c = sgu(z_c, row(sgu_ln_g, l), row(sgu_ln_b, l), sgu_w[l], sgu_b[l])
            d_c = attention(q_c, [(k_c, v_c)])
            x_ctx = merge(x_ctx, h_c, (u_c, y_c) + s5_tail, b_c, (xs_c, yc_c, p_c) + ssd_tail, d_c,
                          mod4, l, midx_ctx, w_gate[l], wb[l], wo[l])
            x_ctx = ffn(x_ctx, row(norm2_g, l), mod4, l, midx_ctx, wfi[l], wfo[l])
    return x_lat.reshape(bsz, seq, dm).astype(x.dtype)
```

```python
import functools

import jax
import jax.numpy as jnp
import numpy as np
from jax import lax
from jax.experimental import pallas as pl
from jax.experimental.pallas import tpu as pltpu

F32 = jnp.float32
BF16 = jnp.bfloat16
HIGHEST = lax.Precision.HIGHEST

LANE = 128
SUBLANE = 8
VMEM_LIMIT = 56 * 1024 * 1024

GRID_W = 64
BR_W = 384
S5_GROUP = 16
S5_GROUPS = BR_W // S5_GROUP
S5_STATE = 64
S5_NSEG = SUBLANE
S5_JB = BR_W // LANE
S5_BW = (LANE // S5_GROUP) * S5_STATE
SGU_CHUNK = 128
SGU_HEADS = 6
SSD_HEADS = 6
SSD_HEAD_DIM = 64
SSD_GROUPS = 2
SSD_STATE = 64
SSD_GN = SSD_GROUPS * SSD_STATE
SSD_CONV_CH = BR_W + 2 * SSD_GN
SSD_T = 128
MLA_HEADS = 6
MLA_NOPE = 64
MLA_ROPE = 32
MLA_V = 64
MLA_QK = MLA_NOPE + MLA_ROPE
MLA_Q_LORA = 384
MLA_KV_LORA = 256
ROPE_BASE = 10000.0
LOG2E = 1.4426950408889634
N_BRANCH = 4

TM = 512
TQ = 256
ATTN_FAST_BOUND = 30.0


def _cp(*sem):
    return pltpu.CompilerParams(dimension_semantics=sem, vmem_limit_bytes=VMEM_LIMIT)


def _full(shape):
    n = len(shape)
    return pl.BlockSpec(shape, lambda *_: (0,) * n)


class _Layer:
    def __init__(self, arr, l):
        self.arr, self.l = arr, l

    @property
    def shape(self):
        return self.arr.shape[1:]


def _wspec(w):
    if not isinstance(w, _Layer):
        return _full(w.shape)
    n, l = len(w.shape), w.l
    return pl.BlockSpec((None,) + tuple(w.shape), lambda *_: (l,) + (0,) * n)


def _warg(w):
    return w.arr if isinstance(w, _Layer) else w


def _silu(x):
    return x * jax.nn.sigmoid(x)


def _rms(x, n, eps=1e-6):
    return x * lax.rsqrt(jnp.sum(x * x, axis=-1, keepdims=True) * (1.0 / n) + eps)


def _ada_kernel(cc_ref, w_ref, b_ref, o_ref):
    s = _silu(cc_ref[...])
    o_ref[...] = jnp.dot(s, w_ref[...], preferred_element_type=F32, precision=HIGHEST) + b_ref[...]


def ada_table(cc8, w_ada, b_ada):
    depth, d, n = w_ada.shape
    tn = n // 4
    return pl.pallas_call(
        _ada_kernel,
        out_shape=jax.ShapeDtypeStruct((depth, 8, n), F32),
        grid=(depth, n // tn),
        in_specs=[pl.BlockSpec((8, d), lambda l, j: (0, 0)),
                  pl.BlockSpec((None, d, tn), lambda l, j: (l, 0, j)),
                  pl.BlockSpec((None, 1, tn), lambda l, j: (l, 0, j))],
        out_specs=pl.BlockSpec((None, 8, tn), lambda l, j: (l, 0, j)),
        compiler_params=_cp("arbitrary", "arbitrary"),
        name="ada_table",
    )(cc8, w_ada, b_ada.reshape(depth, 1, n))


def _mod_spec(l, midx, col, d):
    return pl.BlockSpec((None, None, 1, d), lambda i: (l, midx(i), 0, col))


def _in_kernel(x_ref, g_ref, sh_ref, sc_ref, w1, w2, w3, w4, w5, h_ref, o1, o2, o3, o4, o5):
    x = x_ref[...]
    y = _rms(x, x.shape[-1]) * g_ref[...]
    hb = (y * (1.0 + sc_ref[...]) + sh_ref[...]).astype(BF16)
    h_ref[...] = hb
    for w, o in ((w1, o1), (w2, o2), (w3, o3), (w4, o4), (w5, o5)):
        o[...] = jnp.dot(hb, w[...], preferred_element_type=F32).astype(o.dtype)


def in_proj(x, g, mod4, l, midx, ws, out_dtypes):
    r, d = x.shape
    in_specs = [pl.BlockSpec((TM, d), lambda i: (i, 0)), _full((1, d)),
                _mod_spec(l, midx, 0, d), _mod_spec(l, midx, 1, d)]
    in_specs += [_wspec(w) for w in ws]
    out_shape = [jax.ShapeDtypeStruct((r, d), BF16)]
    out_specs = [pl.BlockSpec((TM, d), lambda i: (i, 0))]
    for w, dt in zip(ws, out_dtypes):
        out_shape.append(jax.ShapeDtypeStruct((r, w.shape[1]), dt))
        out_specs.append(pl.BlockSpec((TM, w.shape[1]), lambda i: (i, 0)))
    return pl.pallas_call(
        _in_kernel, out_shape=out_shape, grid=(r // TM,), in_specs=in_specs, out_specs=out_specs,
        compiler_params=_cp("parallel"), name="in_proj",
    )(x, g, mod4, mod4, *[_warg(w) for w in ws])


def _s5_disc_kernel(are, aim, ldt, bre, bim, lam_re, lam_im, bbre, bbim):
    a_re, a_im = are[...], aim[...]
    dt = jnp.exp(ldt[...])
    mag = jnp.exp(a_re * dt)
    ang = a_im * dt
    ab_re = mag * jnp.cos(ang)
    ab_im = mag * jnp.sin(ang)
    den = a_re * a_re + a_im * a_im
    f_re = ((ab_re - 1.0) * a_re + ab_im * a_im) / den
    f_im = (ab_im * a_re - (ab_re - 1.0) * a_im) / den
    lam_re[...] = ab_re
    lam_im[...] = ab_im
    for c in range(S5_GROUP):
        bbre[c] = f_re * bre[c] - f_im * bim[c]
        bbim[c] = f_re * bim[c] + f_im * bre[c]


def s5_discretise(a_re, a_im, b_re, b_im, log_dt):
    shp = a_re.shape
    rows = int(np.prod(shp)) // LANE
    are = a_re.astype(F32).reshape(rows, LANE)
    aim = a_im.astype(F32).reshape(rows, LANE)
    ldt = jnp.broadcast_to(log_dt.astype(F32)[..., None], shp).reshape(rows, LANE)
    bre = jnp.moveaxis(b_re.astype(F32), -1, 0).reshape(S5_GROUP, rows, LANE)
    bim = jnp.moveaxis(b_im.astype(F32), -1, 0).reshape(S5_GROUP, rows, LANE)
    outs = pl.pallas_call(
        _s5_disc_kernel,
        out_shape=[jax.ShapeDtypeStruct((rows, LANE), F32)] * 2
        + [jax.ShapeDtypeStruct((S5_GROUP, rows, LANE), F32)] * 2,
        name="s5_discretise",
    )(are, aim, ldt, bre, bim)
    lam_re, lam_im = outs[0].reshape(shp), outs[1].reshape(shp)
    bb_re = jnp.moveaxis(outs[2].reshape((S5_GROUP,) + shp), 0, -1)
    bb_im = jnp.moveaxis(outs[3].reshape((S5_GROUP,) + shp), 0, -1)
    return lam_re, lam_im, bb_re, bb_im


def s5_pack(lam_re, lam_im, bb_re, bb_im, c_re, c_im):
    depth = lam_re.shape[0]
    gpb = LANE // S5_GROUP
    eye = jnp.eye(gpb, dtype=F32)

    def lam_blocks(v):
        return v.reshape(depth, 2, S5_JB, 1, S5_BW)

    lam = jnp.concatenate([lam_blocks(lam_re), lam_blocks(lam_im)], axis=-1)
    lam = jnp.broadcast_to(lam, (depth, 2, S5_JB, SUBLANE, 2 * S5_BW))

    def b_blocks(bb):
        v = bb.reshape(depth, 2, S5_JB, gpb, S5_STATE, S5_GROUP)
        return jnp.einsum('ldjgpc,gh->ldjgchp', v, eye).reshape(depth, 2, S5_JB, LANE, S5_BW)

    bblk = jnp.concatenate([b_blocks(bb_re), b_blocks(bb_im)], axis=-1).astype(BF16)

    def c_blocks(cc):
        v = cc.astype(F32).reshape(depth, 2, S5_JB, gpb, S5_GROUP, S5_STATE)
        return jnp.einsum('ldjgcp,gh->ldjhpgc', v, eye).reshape(depth, 2, S5_JB, S5_BW, LANE)

    cblk = jnp.concatenate([c_blocks(c_re), -c_blocks(c_im)], axis=-2).astype(BF16)
    return lam, bblk, cblk


def _cpow(re, im, n):
    out = None
    while n:
        if n & 1:
            out = (re, im) if out is None else (out[0] * re - out[1] * im, out[0] * im + out[1] * re)
        n >>= 1
        if n:
            re, im = re * re - im * im, 2.0 * re * im
    return out


def _s5_pass_kernel(*refs, tt, lseg, with_y):
    if with_y:
        u_ref, bblk, lam, cblk, sloc, h0, y_ref, hfin, uperm, hst, yperm = refs[:11]
    else:
        u_ref, bblk, lam, s_out, uperm, hst = refs[:6]
    bus = refs[-S5_JB:]
    d = pl.program_id(1)
    j = pl.program_id(2)

    @pl.when(j == 0)
    def _():
        if not with_y:
            hst[...] = jnp.zeros(hst.shape, F32)
        else:
            for jb in range(S5_JB):
                pr, pi = _cpow(lam[jb, 0:1, :S5_BW], lam[jb, 0:1, S5_BW:], lseg)

                def chain(order, jb=jb, pr=pr, pi=pi):
                    cr, ci = h0[jb, :, :S5_BW], h0[jb, :, S5_BW:]
                    for s in order:
                        hst[jb, s:s + 1, :S5_BW] = cr
                        hst[jb, s:s + 1, S5_BW:] = ci
                        sr, si = sloc[jb, s:s + 1, :S5_BW], sloc[jb, s:s + 1, S5_BW:]
                        cr, ci = pr * cr - pi * ci + sr, pr * ci + pi * cr + si
                    hfin[jb, :, :S5_BW] = cr
                    hfin[jb, :, S5_BW:] = ci

                @pl.when(d == 0)
                def _():
                    chain(range(S5_NSEG))

                @pl.when(d == 1)
                def _():
                    chain(range(S5_NSEG - 1, -1, -1))

    for s in range(S5_NSEG):
        us = u_ref[s].astype(F32)
        for k in range(S5_JB):
            uperm[k, pl.ds(s, tt, stride=S5_NSEG), :] = us[:, k * LANE:(k + 1) * LANE]

    for jb in range(S5_JB):
        bus[jb][...] = jnp.dot(uperm[jb].astype(BF16), bblk[jb], preferred_element_type=F32)

    for jb in range(S5_JB):
        bu = bus[jb]
        lr, li = lam[jb, :, :S5_BW], lam[jb, :, S5_BW:]

        def step(i, carry, lr=lr, li=li, bu=bu):
            hr, hi = carry
            t = i + d * (tt - 1 - 2 * i)
            r0 = pl.multiple_of(t * S5_NSEG, S5_NSEG)
            nr = lr * hr - li * hi + bu[pl.ds(r0, S5_NSEG), :S5_BW]
            ni = lr * hi + li * hr + bu[pl.ds(r0, S5_NSEG), S5_BW:]
            if with_y:
                bu[pl.ds(r0, S5_NSEG), :S5_BW] = nr
                bu[pl.ds(r0, S5_NSEG), S5_BW:] = ni
            return nr, ni

        hr, hi = lax.fori_loop(0, tt, step, (hst[jb, :, :S5_BW], hst[jb, :, S5_BW:]), unroll=True)
        hst[jb, :, :S5_BW] = hr
        hst[jb, :, S5_BW:] = hi

    if with_y:
        for jb in range(S5_JB):
            yperm[jb] = jnp.dot(bus[jb][...].astype(BF16), cblk[jb], preferred_element_type=F32)
        for s in range(S5_NSEG):
            for k in range(S5_JB):
                y_ref[s, :, k * LANE:(k + 1) * LANE] = (
                    yperm[k, pl.ds(s, tt, stride=S5_NSEG), :].astype(y_ref.dtype))
    else:
        s_out[...] = hst[...]


def s5_scan(u, bblk, lam, cblk, l, bsz, h0):
    seq_len = u.shape[0] // bsz
    lseg = seq_len // S5_NSEG
    tt = min(lseg, 128)
    nt = lseg // tt
    u5 = u.reshape(bsz, S5_NSEG, lseg, BR_W)

    def tile(d, j):
        return j + d * (nt - 1 - 2 * j)

    u_spec = pl.BlockSpec((None, S5_NSEG, tt, BR_W), lambda b, d, j: (b, 0, tile(d, j), 0))
    y_spec = pl.BlockSpec((None, None, S5_NSEG, tt, BR_W), lambda b, d, j: (d, b, 0, tile(d, j), 0))
    w_b = pl.BlockSpec((None, None, S5_JB, LANE, 2 * S5_BW), lambda b, d, j: (l, d, 0, 0, 0))
    w_lam = pl.BlockSpec((None, None, S5_JB, SUBLANE, 2 * S5_BW), lambda b, d, j: (l, d, 0, 0, 0))
    w_c = pl.BlockSpec((None, None, S5_JB, 2 * S5_BW, LANE), lambda b, d, j: (l, d, 0, 0, 0))
    st8 = pl.BlockSpec((None, None, S5_JB, SUBLANE, 2 * S5_BW), lambda b, d, j: (b, d, 0, 0, 0))
    st1 = pl.BlockSpec((None, None, S5_JB, 1, 2 * S5_BW), lambda b, d, j: (b, d, 0, 0, 0))
    n_rows = S5_NSEG * tt
    scratch = [pltpu.VMEM((S5_JB, n_rows, LANE), F32), pltpu.VMEM((S5_JB, SUBLANE, 2 * S5_BW), F32)]
    bus = [pltpu.VMEM((n_rows, 2 * S5_BW), F32)] * S5_JB
    grid = (bsz, 2, nt)
    cp = _cp("arbitrary", "arbitrary", "arbitrary")

    sloc = pl.pallas_call(
        functools.partial(_s5_pass_kernel, tt=tt, lseg=lseg, with_y=False),
        out_shape=jax.ShapeDtypeStruct((bsz, 2, S5_JB, SUBLANE, 2 * S5_BW), F32),
        grid=grid, in_specs=[u_spec, w_b, w_lam], out_specs=st8,
        scratch_shapes=scratch + bus, compiler_params=cp, name="s5_local",
    )(u5, bblk, lam)

    y, hfin = pl.pallas_call(
        functools.partial(_s5_pass_kernel, tt=tt, lseg=lseg, with_y=True),
        out_shape=[jax.ShapeDtypeStruct((2, bsz, S5_NSEG, lseg, BR_W), BF16),
                   jax.ShapeDtypeStruct((bsz, 2, S5_JB, 1, 2 * S5_BW), F32)],
        grid=grid, in_specs=[u_spec, w_b, w_lam, w_c, st8, st1], out_specs=[y_spec, st1],
        scratch_shapes=scratch + [pltpu.VMEM((S5_JB, n_rows, LANE), F32)] + bus,
        compiler_params=cp, name="s5_emit",
    )(u5, bblk, lam, cblk, sloc, h0)
    return y.reshape(2, bsz * seq_len, BR_W), hfin


def _sgu_kernel(z_ref, g_ref, b_ref, w_ref, bias_ref, o_ref):
    z = jax.nn.gelu(z_ref[...].astype(F32))
    u, v = z[:, :BR_W], z[:, BR_W:]
    mu = jnp.mean(v, axis=-1, keepdims=True)
    vc = v - mu
    vn = vc * lax.rsqrt(jnp.mean(vc * vc, axis=-1, keepdims=True) + 1e-5) * g_ref[...] + b_ref[...]
    n_chunk = z.shape[0] // SGU_CHUNK
    half = LANE // 2
    lane = lax.broadcasted_iota(jnp.int32, (SGU_CHUNK, LANE), 1)
    for k in range(BR_W // LANE):
        cols = []
        for c in range(n_chunk):
            blk = vn[c * SGU_CHUNK:(c + 1) * SGU_CHUNK, k * LANE:(k + 1) * LANE]
            lo = jnp.where(lane < half, blk, 0.0)
            cols.append(jnp.concatenate([lo, blk - lo], axis=0))
        rhs = jnp.concatenate(cols, axis=1).astype(BF16)
        mixed = jnp.dot(w_ref[k], rhs, preferred_element_type=F32)
        for c in range(n_chunk):
            rows = slice(c * SGU_CHUNK, (c + 1) * SGU_CHUNK)
            m = mixed[:, c * LANE:(c + 1) * LANE] + bias_ref[:, k * LANE:(k + 1) * LANE]
            o_ref[rows, k * LANE:(k + 1) * LANE] = (u[rows, k * LANE:(k + 1) * LANE] * m).astype(o_ref.dtype)


def sgu(z, ln_g, ln_b, w_pair, bias):
    r = z.shape[0]
    return pl.pallas_call(
        _sgu_kernel, out_shape=jax.ShapeDtypeStruct((r, BR_W), BF16), grid=(r // TM,),
        in_specs=[pl.BlockSpec((TM, 2 * BR_W), lambda i: (i, 0)), _full((1, BR_W)), _full((1, BR_W)),
                  _wspec(w_pair), _wspec(bias)],
        out_specs=pl.BlockSpec((TM, BR_W), lambda i: (i, 0)),
        compiler_params=_cp("parallel"), name="sgu",
    )(z, ln_g, ln_b, _warg(w_pair), _warg(bias))


def _ssd_prep_kernel(cur_ref, prev_ref, next_ref, dt_ref, w_ref, b_ref, dtb_ref, alane_ref, esel_ref,
                     xs_ref, bc_ref, da_ref, xt_ref, *, seq_len):
    i = pl.program_id(0)
    x = cur_ref[:, BR_W:].astype(F32)
    tm = x.shape[0]
    row = lax.broadcasted_iota(jnp.int32, (tm, 1), 0)
    pos = lax.rem(row + i * tm, seq_len)
    prev_row = prev_ref[SUBLANE - 1:SUBLANE, BR_W:].astype(F32)
    next_row = next_ref[0:1, BR_W:].astype(F32)
    x_prev = jnp.where(row == 0, prev_row, pltpu.roll(x, 1, axis=0))
    x_prev = jnp.where(pos == 0, 0.0, x_prev)
    x_next = jnp.where(row == tm - 1, next_row, pltpu.roll(x, tm - 1, axis=0))
    x_next = jnp.where(pos == seq_len - 1, 0.0, x_next)
    y = _silu(w_ref[0:1, :] * x_prev + w_ref[1:2, :] * x + w_ref[2:3, :] * x_next + b_ref[...])
    xs = y[:, :BR_W]
    xs_ref[...] = xs
    bc_ref[...] = y[:, BR_W:].astype(bc_ref.dtype)
    t = dt_ref[...] + dtb_ref[...]
    sp = jnp.maximum(t, 0.0) + jnp.log1p(jnp.exp(-jnp.abs(t)))
    lane = lax.broadcasted_iota(jnp.int32, t.shape, 1)
    d_a = sp * alane_ref[...]
    da_ref[0] = jnp.where(lane < SSD_HEADS, d_a, 0.0)
    da_ref[1] = jnp.where(lane < SSD_HEADS, pltpu.roll(d_a, LANE - SSD_HEADS, axis=1), 0.0)
    sp_hi = sp.astype(BF16)
    sp_lo = (sp - sp_hi.astype(F32)).astype(BF16)
    for d in range(2):
        dt_x = (jnp.dot(sp_hi, esel_ref[d], preferred_element_type=F32)
                + jnp.dot(sp_lo, esel_ref[d], preferred_element_type=F32))
        xt_ref[d] = (xs * dt_x).astype(xt_ref.dtype)


def ssd_prep(p_ssd, dt_raw, seq_len, conv_w, conv_b, dt_bias, a_lane, esel):
    r, w = p_ssd.shape
    nb = r // SUBLANE
    per = TM // SUBLANE
    return pl.pallas_call(
        functools.partial(_ssd_prep_kernel, seq_len=seq_len),
        out_shape=[jax.ShapeDtypeStruct((r, BR_W), F32), jax.ShapeDtypeStruct((r, 2 * SSD_GN), BF16),
                   jax.ShapeDtypeStruct((2, r, LANE), F32), jax.ShapeDtypeStruct((2, r, BR_W), BF16)],
        grid=(r // TM,),
        in_specs=[pl.BlockSpec((TM, w), lambda i: (i, 0)),
                  pl.BlockSpec((SUBLANE, w), lambda i: (jnp.maximum(i * per - 1, 0), 0)),
                  pl.BlockSpec((SUBLANE, w), lambda i: (jnp.minimum((i + 1) * per, nb - 1), 0)),
                  pl.BlockSpec((TM, LANE), lambda i: (i, 0)),
                  _full(conv_w.shape), _full(conv_b.shape), _full(dt_bias.shape), _full(a_lane.shape),
                  _full(esel.shape)],
        out_specs=[pl.BlockSpec((TM, BR_W), lambda i: (i, 0)), pl.BlockSpec((TM, 2 * SSD_GN), lambda i: (i, 0)),
                   pl.BlockSpec((2, TM, LANE), lambda i: (0, i, 0)), pl.BlockSpec((2, TM, BR_W), lambda i: (0, i, 0))],
        compiler_params=_cp("parallel"), name="ssd_prep",
    )(p_ssd, p_ssd, p_ssd, dt_raw, conv_w, conv_b, dt_bias, a_lane, esel)


def _ssd_scan_kernel(xt_ref, bc_ref, da_ref, tri_ref, h0_ref, y_ref, hfin_ref, st_ref, *, n_chunk, bsz):
    d = pl.program_id(0)
    j = pl.program_id(1)
    T = SSD_T

    @pl.when(j == 0)
    def _():
        st_ref[...] = h0_ref[...]

    tri = tri_ref[...]
    mask = tri > 0.5
    rep = SSD_HEADS // SSD_GROUPS

    def chunk(ci, carry):
        c = ci + d * (n_chunk - 1 - 2 * ci)
        r0 = pl.multiple_of(c * T, T)
        for b in range(bsz):
            d_a = da_ref[b, pl.ds(r0, T), :]
            a_cum = jnp.dot(tri, d_a, preferred_element_type=F32, precision=HIGHEST)
            a_cum_t = a_cum.T
            total = jnp.sum(d_a, axis=0, keepdims=True)
            xt = xt_ref[b, pl.ds(r0, T), :]
            bc = bc_ref[b, pl.ds(r0, T), :]
            bm_t = bc[:, :SSD_GN].astype(F32).T
            cm = bc[:, SSD_GN:]
            for g in range(SSD_GROUPS):
                gs = slice(g * SSD_STATE, (g + 1) * SSD_STATE)
                b_t = bm_t[gs, :]
                c_g = cm[:, gs]
                scores = jnp.dot(c_g, b_t.astype(BF16), preferred_element_type=F32)
                for hh in range(rep):
                    h = g * rep + hh
                    hs = slice(h * SSD_HEAD_DIM, (h + 1) * SSD_HEAD_DIM)
                    col = a_cum[:, h:h + 1]
                    rowv = a_cum_t[h:h + 1, :]
                    tot = total[:, h:h + 1]
                    decay = jnp.exp(jnp.where(mask, col - rowv, -1e30))
                    p = (scores * decay).astype(BF16)
                    x_h = xt[:, hs]
                    s_old = st_ref[b, h]
                    y_h = jnp.dot(p, x_h, preferred_element_type=F32)
                    y_h += jnp.dot(c_g, s_old.astype(BF16), preferred_element_type=F32) * jnp.exp(col)
                    bw = (b_t * jnp.exp(tot - rowv)).astype(BF16)
                    st_ref[b, h] = s_old * jnp.exp(tot) + jnp.dot(bw, x_h, preferred_element_type=F32)
                    y_ref[b, pl.ds(r0, T), hs] = y_h.astype(y_ref.dtype)
        return carry

    lax.fori_loop(0, n_chunk, chunk, 0)
    hfin_ref[...] = st_ref[...]


def ssd_scan(xt, bc, d_a, tri, bsz, h0):
    r = bc.shape[0]
    seq_len = r // bsz
    ts = min(seq_len, TM)
    nt = seq_len // ts

    def tile(d, j):
        return j + d * (nt - 1 - 2 * j)

    st = pl.BlockSpec((None, bsz, SSD_HEADS, SSD_STATE, SSD_HEAD_DIM), lambda d, j: (d, 0, 0, 0, 0))
    y, hfin = pl.pallas_call(
        functools.partial(_ssd_scan_kernel, n_chunk=ts // SSD_T, bsz=bsz),
        out_shape=[jax.ShapeDtypeStruct((2, bsz, seq_len, BR_W), BF16),
                   jax.ShapeDtypeStruct((2, bsz, SSD_HEADS, SSD_STATE, SSD_HEAD_DIM), F32)],
        grid=(2, nt),
        in_specs=[pl.BlockSpec((None, bsz, ts, BR_W), lambda d, j: (d, 0, tile(d, j), 0)),
                  pl.BlockSpec((bsz, ts, 2 * SSD_GN), lambda d, j: (0, tile(d, j), 0)),
                  pl.BlockSpec((None, bsz, ts, LANE), lambda d, j: (d, 0, tile(d, j), 0)),
                  pl.BlockSpec((None, SSD_T, SSD_T), lambda d, j: (d, 0, 0)), st],
        out_specs=[pl.BlockSpec((None, bsz, ts, BR_W), lambda d, j: (d, 0, tile(d, j), 0)), st],
        scratch_shapes=[pltpu.VMEM((bsz, SSD_HEADS, SSD_STATE, SSD_HEAD_DIM), F32)],
        compiler_params=_cp("arbitrary", "arbitrary"), name="ssd_scan",
    )(xt.reshape(2, bsz, seq_len, BR_W), bc.reshape(bsz, seq_len, 2 * SSD_GN),
      d_a.reshape(2, bsz, seq_len, LANE), tri, h0)
    return y.reshape(2, r, BR_W), hfin


def _rope_table_kernel(ang_ref, cos_ref, sin_ref):
    ang = ang_ref[...]
    lane = lax.broadcasted_iota(jnp.int32, ang.shape, 1)
    quarter = MLA_ROPE // 2
    first = (lane >= MLA_NOPE) & (lane < MLA_NOPE + quarter)
    second = (lane >= MLA_NOPE + quarter) & (lane < MLA_QK)
    c, s = jnp.cos(ang), jnp.sin(ang)
    cos_ref[...] = jnp.where(first | second, c, jnp.where(lane < MLA_NOPE, 1.0, 0.0))
    sin_ref[...] = jnp.where(first, -s, jnp.where(second, s, 0.0))


def _rope_partner(v):
    quarter = MLA_ROPE // 2
    src = np.arange(LANE)
    src[MLA_NOPE:MLA_NOPE + quarter] += quarter
    src[MLA_NOPE + quarter:MLA_QK] -= quarter
    valid = (np.arange(LANE) >= MLA_NOPE) & (np.arange(LANE) < MLA_QK)
    return jnp.where(valid, v[..., src], 0)


def rope_tables(seq_len):
    n_rows = seq_len // GRID_W
    pairs = MLA_ROPE // 4
    pos_row = jnp.repeat(jnp.arange(n_rows, dtype=F32), GRID_W)
    pos_col = jnp.tile(jnp.arange(GRID_W, dtype=F32), n_rows)
    inv_freq = ROPE_BASE ** (-jnp.arange(pairs, dtype=F32) / pairs)
    ang = jnp.concatenate([pos_row[:, None] * inv_freq, pos_col[:, None] * inv_freq], axis=-1)
    ang_pad = jnp.concatenate([jnp.zeros((seq_len, MLA_NOPE), F32), ang, ang,
                               jnp.zeros((seq_len, LANE - MLA_QK), F32)], axis=-1)
    tr = min(seq_len, 1024)
    spec = pl.BlockSpec((tr, LANE), lambda i: (i, 0))
    return pl.pallas_call(
        _rope_table_kernel, out_shape=[jax.ShapeDtypeStruct((seq_len, LANE), F32)] * 2,
        grid=(seq_len // tr,), in_specs=[spec], out_specs=[spec] * 2,
        compiler_params=_cp("parallel"), name="rope_tables",
    )(ang_pad)


def _mla_prep_kernel(*refs, rope):
    if rope:
        (p_ref, gq_ref, gkv_ref, wq_ref, wkv_ref, nq_ref, nk_ref, wqs_ref, nqs_ref, nks_ref, cos_ref, sin_ref,
         q_ref, k_ref, v_ref) = refs
    else:
        p_ref, gq_ref, gkv_ref, wq_ref, wkv_ref, nq_ref, nk_ref, q_ref, k_ref, v_ref = refs
    p = p_ref[...].astype(F32)
    cq = (_rms(p[:, :MLA_Q_LORA], MLA_Q_LORA) * gq_ref[...]).astype(BF16)
    ckv = (_rms(p[:, MLA_Q_LORA:MLA_Q_LORA + MLA_KV_LORA], MLA_KV_LORA) * gkv_ref[...]).astype(BF16)
    kr = p[:, MLA_Q_LORA + MLA_KV_LORA:MLA_Q_LORA + MLA_KV_LORA + LANE]
    q_all = jnp.dot(cq, wq_ref[...], preferred_element_type=F32)
    kv_all = jnp.dot(ckv, wkv_ref[...], preferred_element_type=F32)
    lane = lax.broadcasted_iota(jnp.int32, (1, LANE), 1)
    one_col = jnp.where(lane == MLA_V, 1.0, 0.0)
    scale = MLA_QK ** -0.5 * LOG2E
    if rope:
        qs_all = jnp.dot(cq, wqs_ref[...], preferred_element_type=F32)
        q_cos, q_sin = nq_ref[...] * cos_ref[...], nqs_ref[...] * sin_ref[...]
        k_cos = nk_ref[...] * cos_ref[...]
        k_part = p[:, MLA_Q_LORA + MLA_KV_LORA + LANE:] * (nks_ref[...] * sin_ref[...])

    def inv_rms(t):
        return lax.rsqrt(jnp.sum(t * t, axis=-1, keepdims=True) * (1.0 / MLA_QK) + 1e-6)

    for h in range(MLA_HEADS):
        q = q_all[:, h * LANE:(h + 1) * LANE]
        k = kv_all[:, 2 * h * LANE:(2 * h + 1) * LANE] + kr
        if rope:
            q_out = (q * q_cos + qs_all[:, h * LANE:(h + 1) * LANE] * q_sin) * (inv_rms(q) * scale)
            k_out = (k * k_cos + k_part) * inv_rms(k)
        else:
            q_out = q * nq_ref[...] * (inv_rms(q) * scale)
            k_out = k * nk_ref[...] * inv_rms(k)
        q_ref[h] = q_out.astype(q_ref.dtype)
        k_ref[h] = k_out.astype(k_ref.dtype)
        v_ref[h] = (kv_all[:, (2 * h + 1) * LANE:(2 * h + 2) * LANE] + one_col).astype(v_ref.dtype)


def mla_prep(p_mla, bsz, gq, gkv, wq, wkv, nq, nk, rope_args):
    r, w = p_mla.shape
    seq_len = r // bsz
    tr = min(seq_len, TM)
    nt = seq_len // tr
    rope = rope_args is not None
    in_specs = [pl.BlockSpec((tr, w), lambda b, i: (b * nt + i, 0)), _full(gq.shape), _full(gkv.shape),
                _wspec(wq), _wspec(wkv), _full(nq.shape), _full(nk.shape)]
    args = [p_mla, gq, gkv, _warg(wq), _warg(wkv), nq, nk]
    if rope:
        in_specs += [_wspec(a) for a in rope_args[:3]]
        in_specs += [pl.BlockSpec((tr, LANE), lambda b, i: (i, 0))] * 2
        args += [_warg(a) for a in rope_args]
    head = pl.BlockSpec((None, MLA_HEADS, tr, LANE), lambda b, i: (b, 0, i, 0))
    return pl.pallas_call(
        functools.partial(_mla_prep_kernel, rope=rope),
        out_shape=[jax.ShapeDtypeStruct((bsz, MLA_HEADS, seq_len, LANE), BF16)] * 3,
        grid=(bsz, nt), in_specs=in_specs, out_specs=[head] * 3,
        compiler_params=_cp("parallel", "parallel"), name="mla_prep",
    )(*args)


def _attn_kernel(*refs, n_kv):
    bound_ref, q_ref = refs[0], refs[1]
    kv_refs = refs[2:2 + 2 * n_kv]
    o_ref = refs[2 + 2 * n_kv]
    m_ref = refs[3 + 2 * n_kv]
    scores = [[lax.dot_general(q_ref[hh], kv_refs[2 * i][hh], (((1,), (1,)), ((), ())),
                               preferred_element_type=F32) for i in range(n_kv)] for hh in range(2)]

    def row_max(first_set):
        for hh in range(2):
            m = scores[hh][n_kv - 1].max(axis=-1, keepdims=True)
            for si in scores[hh][first_set:n_kv - 1]:
                m = jnp.maximum(m, si.max(axis=-1, keepdims=True))
            m_ref[hh] = m

    if n_kv > 1:
        fast = bound_ref[0] <= ATTN_FAST_BOUND

        @pl.when(fast)
        def _():
            row_max(n_kv - 1)

        @pl.when(jnp.logical_not(fast))
        def _():
            row_max(0)
    else:
        row_max(0)

    for hh in range(2):
        m = m_ref[hh]
        acc = None
        for i, si in enumerate(scores[hh]):
            pv = jnp.dot(jnp.exp2((si - m).astype(BF16)), kv_refs[2 * i + 1][hh], preferred_element_type=F32)
            acc = pv if acc is None else acc + pv
        o = acc[:, :MLA_V] / acc[:, MLA_V:MLA_V + 1]
        o_ref[:, hh * MLA_V:(hh + 1) * MLA_V] = o.astype(o_ref.dtype)


def attention(q, kvs, score_bound):
    bsz, nh, lq, _ = q.shape
    tq = min(lq, TQ)
    nq = lq // tq
    in_specs = [pl.BlockSpec(memory_space=pltpu.SMEM),
                pl.BlockSpec((None, 2, tq, LANE), lambda b, hp, i: (b, hp, i, 0))]
    args = [score_bound, q]
    for k, v in kvs:
        spec = pl.BlockSpec((None, 2, k.shape[2], LANE), lambda b, hp, i: (b, hp, 0, 0))
        in_specs += [spec, spec]
        args += [k, v]
    return pl.pallas_call(
        functools.partial(_attn_kernel, n_kv=len(kvs)),
        out_shape=jax.ShapeDtypeStruct((bsz * lq, nh * MLA_V), BF16),
        grid=(bsz, nh // 2, nq), in_specs=in_specs,
        out_specs=pl.BlockSpec((tq, 2 * MLA_V), lambda b, hp, i: (b * nq + i, hp)),
        scratch_shapes=[pltpu.VMEM((2, tq, 1), F32)],
        compiler_params=_cp("parallel", "parallel", "arbitrary"), name="attention",
    )(*args)


def _merge_kernel(x_ref, h_ref, u_ref, s5f_ref, s5b_ref, s5d_ref, wglu_ref, b_ref,
                  xs_ref, ssdf_ref, ssdb_ref, z_ref, ssdd_ref, ssdg_ref, d_ref,
                  g1_ref, wg_ref, wb_ref, wo_ref, o_ref):
    y = s5d_ref[...] * u_ref[...].astype(F32) + s5f_ref[...].astype(F32) + s5b_ref[...].astype(F32)
    g = jax.nn.gelu(y)
    a = (g * jax.nn.sigmoid(jnp.dot(g.astype(BF16), wglu_ref[...], preferred_element_type=F32))).astype(BF16)
    y = ssdd_ref[...] * xs_ref[...] + ssdf_ref[...].astype(F32) + ssdb_ref[...].astype(F32)
    c = (_rms(y * _silu(z_ref[...].astype(F32)), BR_W) * ssdg_ref[...]).astype(BF16)
    h = h_ref[...]
    dm = x_ref.shape[-1]
    merged = None
    for i, br in enumerate((a, b_ref[...], c, d_ref[...])):
        gate = jax.nn.sigmoid(jnp.dot(h, wg_ref[:, i * dm:(i + 1) * dm], preferred_element_type=F32))
        term = gate * jnp.dot(br, wb_ref[i], preferred_element_type=F32)
        merged = term if merged is None else merged + term
    mix = jnp.dot(merged.astype(BF16), wo_ref[...], preferred_element_type=F32)
    o_ref[...] = x_ref[...] + g1_ref[...] * mix


def merge(x, h, s5_in, b, ssd_in, d, mod4, l, midx, wg, wb, wo):
    r, dm = x.shape
    row = pl.BlockSpec((TM, dm), lambda i: (i, 0))
    br = pl.BlockSpec((TM, BR_W), lambda i: (i, 0))
    fwd = pl.BlockSpec((None, TM, BR_W), lambda i: (0, i, 0))
    bwd = pl.BlockSpec((None, TM, BR_W), lambda i: (1, i, 0))
    vec = _full((1, BR_W))
    u, y_s5, s5_d, w_glu = s5_in
    xs, y_ssd, p_ssd, ssd_d, ssd_g = ssd_in
    return pl.pallas_call(
        _merge_kernel, out_shape=jax.ShapeDtypeStruct((r, dm), F32), grid=(r // TM,),
        in_specs=[row, row, br, fwd, bwd, vec, _wspec(w_glu), br,
                  br, fwd, bwd, br, vec, vec, br,
                  _mod_spec(l, midx, 2, dm), _wspec(wg), _wspec(wb), _wspec(wo)],
        out_specs=row, compiler_params=_cp("parallel"), name="merge",
    )(x, h, u, y_s5, y_s5, s5_d, _warg(w_glu), b, xs, y_ssd, y_ssd, p_ssd, ssd_d, ssd_g, d, mod4,
      _warg(wg), _warg(wb), _warg(wo))


def _ffn_kernel(x_ref, g_ref, sh_ref, sc_ref, g2_ref, wi_ref, wo_ref, o_ref, *, n_split):
    x = x_ref[...]
    y = _rms(x, x.shape[-1]) * g_ref[...]
    hb = (y * (1.0 + sc_ref[...]) + sh_ref[...]).astype(BF16)
    hid = wo_ref.shape[0]
    step = hid // n_split
    acc = None
    for c in range(n_split):
        gate = jnp.dot(hb, wi_ref[:, c * step:(c + 1) * step], preferred_element_type=F32)
        up = jnp.dot(hb, wi_ref[:, hid + c * step:hid + (c + 1) * step], preferred_element_type=F32)
        part = jnp.dot((_silu(gate) * up).astype(BF16), wo_ref[c * step:(c + 1) * step, :],
                       preferred_element_type=F32)
        acc = part if acc is None else acc + part
    o_ref[...] = x + g2_ref[...] * acc


def ffn(x, g, mod4, l, midx, wi, wo):
    r, d = x.shape
    row = pl.BlockSpec((TM, d), lambda i: (i, 0))
    n_split = 11 if wo.shape[0] % (11 * LANE) == 0 else 1
    return pl.pallas_call(
        functools.partial(_ffn_kernel, n_split=n_split),
        out_shape=jax.ShapeDtypeStruct((r, d), F32), grid=(r // TM,),
        in_specs=[row, _full((1, d)), _mod_spec(l, midx, 3, d), _mod_spec(l, midx, 4, d),
                  _mod_spec(l, midx, 5, d), _wspec(wi), _wspec(wo)],
        out_specs=row, compiler_params=_cp("parallel"), name="ffn",
    )(x, g, mod4, mod4, mod4, _warg(wi), _warg(wo))


def _pad_cols(w, n):
    return jnp.pad(w, ((0, 0),) * (w.ndim - 1) + ((0, n - w.shape[-1]),))


def kernel(x, c, ctx, c_ctx, w_ada, b_ada, norm1_g, norm2_g, w_in, s5_a_re, s5_a_im, s5_b_re, s5_b_im, s5_c_re, s5_c_im, s5_log_dt, s5_d, s5_w_glu, sgu_ln_g, sgu_ln_b, sgu_w_s, sgu_b_s, ssd_conv_w, ssd_conv_b, ssd_a_log, ssd_dt_bias, ssd_d, ssd_norm_g, mla_q_a_norm, mla_w_uq, mla_kv_a_norm, mla_w_ukv, mla_q_norm, mla_k_norm, w_branch, w_out, w_ffn_in, w_ffn_out):
    bsz, seq, dm = x.shape
    lc = ctx.shape[1]
    depth = w_ada.shape[0]
    assert seq % TM == 0 and (bsz * lc) % TM == 0 and seq % (S5_NSEG * SUBLANE) == 0
    assert lc % SSD_T == 0 and lc % (S5_NSEG * SUBLANE) == 0 and bsz + 1 <= 8

    cc8 = jnp.zeros((8, dm), F32).at[:bsz].set(c.astype(F32)).at[bsz].set(c_ctx.astype(F32))
    mod4 = ada_table(cc8, w_ada.astype(F32), b_ada.astype(F32)).reshape(depth, 8, 1, 6 * dm)
    lat_tiles = seq // TM
    midx_lat = lambda i: i // lat_tiles
    midx_ctx = lambda i: bsz

    off = np.cumsum([0, BR_W, 2 * BR_W, BR_W + SSD_CONV_CH + 2 * SSD_HEADS,
                     MLA_Q_LORA + MLA_KV_LORA + MLA_ROPE, N_BRANCH * dm])
    w_s5 = w_in[:, :, off[0]:off[1]].astype(BF16)
    w_sgu = w_in[:, :, off[1]:off[2]].astype(BF16)
    w_ssd = w_in[:, :, off[2]:off[2] + BR_W + SSD_CONV_CH].astype(BF16)
    w_dt = _pad_cols(w_in[:, :, off[2] + BR_W + SSD_CONV_CH:off[3]], LANE).astype(BF16)
    w_mla_main = w_in[:, :, off[3]:off[3] + MLA_Q_LORA + MLA_KV_LORA]
    w_kr = w_in[:, :, off[3] + MLA_Q_LORA + MLA_KV_LORA:off[4]]
    zeros = lambda n: jnp.zeros((depth, dm, n), w_in.dtype)
    kr_block = jnp.concatenate([zeros(MLA_NOPE), w_kr, zeros(LANE - MLA_QK)], axis=-1)
    w_mla = jnp.concatenate([w_mla_main, kr_block, _rope_partner(kr_block)], axis=-1).astype(BF16)
    w_gate = w_in[:, :, off[4]:off[5]].astype(BF16)

    lam_re, lam_im, bb_re, bb_im = s5_discretise(s5_a_re, s5_a_im, s5_b_re, s5_b_im, s5_log_dt)
    lam, bblk, cblk = s5_pack(lam_re, lam_im, bb_re, bb_im, s5_c_re, s5_c_im)
    s5_wg = s5_w_glu.astype(BF16)

    sgu_w = sgu_w_s.reshape(depth, SGU_HEADS // 2, 2, SGU_CHUNK, SGU_CHUNK)
    sgu_w = jnp.concatenate([sgu_w[:, :, 0], sgu_w[:, :, 1]], axis=-1).astype(BF16)
    sgu_b = jnp.repeat(jnp.swapaxes(sgu_b_s, 1, 2), BR_W // SGU_HEADS, axis=-1).astype(F32)

    a_neg = -jnp.exp(ssd_a_log.astype(F32))
    a_lane = _pad_cols(a_neg.reshape(depth, 1, 2 * SSD_HEADS), LANE)
    head_of_lane = jnp.arange(LANE)[None, :, None] - SSD_HEADS * jnp.arange(2)[:, None, None]
    col_head = jnp.arange(BR_W) // SSD_HEAD_DIM
    esel = ((head_of_lane == col_head[None, None, :]) & (head_of_lane >= 0)).astype(BF16)
    ti = jnp.arange(SSD_T)
    tri = jnp.stack([ti[None, :] <= ti[:, None], ti[None, :] >= ti[:, None]]).astype(F32)
    dt_bias = _pad_cols(ssd_dt_bias.astype(F32).reshape(depth, 1, 2 * SSD_HEADS), LANE)
    ssd_dskip = jnp.repeat(ssd_d.astype(F32), SSD_HEAD_DIM, axis=-1)[:, None, :]

    wq = mla_w_uq.reshape(depth, MLA_Q_LORA, MLA_HEADS, MLA_QK)
    wq = _pad_cols(wq, LANE)
    wq_sw = _rope_partner(wq).reshape(depth, MLA_Q_LORA, MLA_HEADS * LANE).astype(BF16)
    wq = wq.reshape(depth, MLA_Q_LORA, MLA_HEADS * LANE).astype(BF16)
    wkv = mla_w_ukv.reshape(depth, MLA_KV_LORA, MLA_HEADS, 2, MLA_NOPE)
    wkv = _pad_cols(wkv, LANE).reshape(depth, MLA_KV_LORA, MLA_HEADS * 2 * LANE).astype(BF16)
    nq = _pad_cols(mla_q_norm.astype(F32), LANE)[:, None, :]
    nk = _pad_cols(mla_k_norm.astype(F32), LANE)[:, None, :]
    nq_sw, nk_sw = _rope_partner(nq), _rope_partner(nk)
    rope_cos, rope_sin = rope_tables(seq)
    score_bound = (MLA_QK ** 0.5 * LOG2E * jnp.max(jnp.abs(mla_q_norm.astype(F32)), axis=-1)
                   * jnp.max(jnp.abs(mla_k_norm.astype(F32)), axis=-1))

    wb = w_branch.astype(BF16)
    wo = w_out.astype(BF16)
    wfi = w_ffn_in.astype(BF16)
    wfo = w_ffn_out.astype(BF16)

    row = lambda v, l: v[l].astype(F32).reshape(1, -1)
    x_lat = x.astype(F32).reshape(bsz * seq, dm)
    x_ctx = ctx.astype(F32).reshape(bsz * lc, dm)
    s5_zero = jnp.zeros((bsz, 2, S5_JB, 1, 2 * S5_BW), F32)
    ssd_zero = jnp.zeros((2, bsz, SSD_HEADS, SSD_STATE, SSD_HEAD_DIM), F32)

    for l in range(depth):
        need_ctx = l < depth - 1
        lay = lambda w: _Layer(w, l)
        ws = tuple(lay(w) for w in (w_s5, w_sgu, w_ssd, w_dt, w_mla))
        w_merge = (lay(w_gate), lay(wb), lay(wo))
        w_ffn = (lay(wfi), lay(wfo))
        dts = (BF16, BF16, BF16, F32, BF16)
        g1 = row(norm1_g, l)
        h_l, u_l, z_l, p_l, dtr_l, m_l = in_proj(x_lat, g1, mod4, l, midx_lat, ws, dts)
        h_c, u_c, z_c, p_c, dtr_c, m_c = in_proj(x_ctx, g1, mod4, l, midx_ctx, ws, dts)

        y_c, s5_h = s5_scan(u_c, bblk, lam, cblk, l, bsz, s5_zero)
        y_l, _ = s5_scan(u_l, bblk, lam, cblk, l, bsz, s5_h)
        s5_tail = (row(s5_d, l), lay(s5_wg))
        b_l = sgu(z_l, row(sgu_ln_g, l), row(sgu_ln_b, l), lay(sgu_w), lay(sgu_b))
        conv_w, conv_b = ssd_conv_w[l].astype(F32), row(ssd_conv_b, l)
        xs_c, bc_c, da_c, xt_c = ssd_prep(p_c, dtr_c, lc, conv_w, conv_b, dt_bias[l], a_lane[l], esel)
        xs_l, bc_l, da_l, xt_l = ssd_prep(p_l, dtr_l, seq, conv_w, conv_b, dt_bias[l], a_lane[l], esel)
        yc_c, ssd_h = ssd_scan(xt_c, bc_c, da_c, tri, bsz, ssd_zero)
        yc_l, _ = ssd_scan(xt_l, bc_l, da_l, tri, bsz, ssd_h)
        ssd_tail = (ssd_dskip[l], row(ssd_norm_g, l))
        mla_w = (row(mla_q_a_norm, l), row(mla_kv_a_norm, l), lay(wq), lay(wkv), nq[l], nk[l])
        q_c, k_c, v_c = mla_prep(m_c, bsz, *mla_w, None)
        q_l, k_l, v_l = mla_prep(m_l, bsz, *mla_w, (lay(wq_sw), nq_sw[l], nk_sw[l], rope_cos, rope_sin))
        d_l = attention(q_l, [(k_l, v_l), (k_c, v_c)], score_bound[l:l + 1])

        x_lat = merge(x_lat, h_l, (u_l, y_l) + s5_tail, b_l, (xs_l, yc_l, p_l) + ssd_tail, d_l,
                      mod4, l, midx_lat, *w_merge)
        x_lat = ffn(x_lat, row(norm2_g, l), mod4, l, midx_lat, *w_ffn)
        if need_ctx:
            b_c = sgu(z_c, row(sgu_ln_g, l), row(sgu_ln_b, l), lay(sgu_w), lay(sgu_b))
            d_c = attention(q_c, [(k_c, v_c)], score_bound[l:l + 1])
            x_ctx = merge(x_ctx, h_c, (u_c, y_c) + s5_tail, b_c, (xs_c, yc_c, p_c) + ssd_tail, d_c,
                          mod4, l, midx_ctx, *w_merge)
            x_ctx = ffn(x_ctx, row(norm2_g, l), mod4, l, midx_ctx, *w_ffn)
    return x_lat.reshape(bsz, seq, dm).astype(x.dtype)
```

```python
import functools

import jax
import jax.numpy as jnp
import numpy as np
from jax import lax
from jax.experimental import pallas as pl
from jax.experimental.pallas import tpu as pltpu

F32 = jnp.float32
BF16 = jnp.bfloat16
HIGHEST = lax.Precision.HIGHEST

LANE = 128
SUBLANE = 8
VMEM_LIMIT = 56 * 1024 * 1024

GRID_W = 64
BR_W = 384
S5_GROUP = 16
S5_GROUPS = BR_W // S5_GROUP
S5_STATE = 64
S5_NSEG = SUBLANE
S5_JB = BR_W // LANE
S5_BW = (LANE // S5_GROUP) * S5_STATE
SGU_CHUNK = 128
SGU_HEADS = 6
SSD_HEADS = 6
SSD_HEAD_DIM = 64
SSD_GROUPS = 2
SSD_STATE = 64
SSD_GN = SSD_GROUPS * SSD_STATE
SSD_CONV_CH = BR_W + 2 * SSD_GN
SSD_T = 128
MLA_HEADS = 6
MLA_NOPE = 64
MLA_ROPE = 32
MLA_V = 64
MLA_QK = MLA_NOPE + MLA_ROPE
MLA_Q_LORA = 384
MLA_KV_LORA = 256
ROPE_BASE = 10000.0
LOG2E = 1.4426950408889634
N_BRANCH = 4

TM = 512
TQ = 256
ATTN_FAST_BOUND = 30.0


def _cp(*sem):
    return pltpu.CompilerParams(dimension_semantics=sem, vmem_limit_bytes=VMEM_LIMIT)


def _full(shape):
    n = len(shape)
    return pl.BlockSpec(shape, lambda *_: (0,) * n)


class _Layer:
    def __init__(self, arr, l):
        self.arr, self.l = arr, l

    @property
    def shape(self):
        return self.arr.shape[1:]


def _wspec(w):
    if not isinstance(w, _Layer):
        return _full(w.shape)
    n, l = len(w.shape), w.l
    return pl.BlockSpec((None,) + tuple(w.shape), lambda *_: (l,) + (0,) * n, pipeline_mode=pl.Buffered(1))


def _warg(w):
    return w.arr if isinstance(w, _Layer) else w


def _silu(x):
    return x * jax.nn.sigmoid(x)


def _rms(x, n, eps=1e-6):
    return x * lax.rsqrt(jnp.sum(x * x, axis=-1, keepdims=True) * (1.0 / n) + eps)


def _ada_kernel(cc_ref, w_ref, b_ref, o_ref):
    s = _silu(cc_ref[...])
    o_ref[...] = jnp.dot(s, w_ref[...], preferred_element_type=F32, precision=HIGHEST) + b_ref[...]


def ada_table(cc8, w_ada, b_ada):
    depth, d, n = w_ada.shape
    tn = n // 4
    return pl.pallas_call(
        _ada_kernel,
        out_shape=jax.ShapeDtypeStruct((depth, 8, n), F32),
        grid=(depth, n // tn),
        in_specs=[pl.BlockSpec((8, d), lambda l, j: (0, 0)),
                  pl.BlockSpec((None, d, tn), lambda l, j: (l, 0, j)),
                  pl.BlockSpec((None, 1, tn), lambda l, j: (l, 0, j))],
        out_specs=pl.BlockSpec((None, 8, tn), lambda l, j: (l, 0, j)),
        compiler_params=_cp("arbitrary", "arbitrary"),
        name="ada_table",
    )(cc8, w_ada, b_ada.reshape(depth, 1, n))


def _mod_spec(l, midx, col, d):
    return pl.BlockSpec((None, None, 1, d), lambda i: (l, midx(i), 0, col))


def _in_kernel(x_ref, g_ref, sh_ref, sc_ref, w1, w2, w3, w4, w5, h_ref, o1, o2, o3, o4, o5):
    x = x_ref[...]
    y = _rms(x, x.shape[-1]) * g_ref[...]
    hb = (y * (1.0 + sc_ref[...]) + sh_ref[...]).astype(BF16)
    h_ref[...] = hb
    for w, o in ((w1, o1), (w2, o2), (w3, o3), (w4, o4), (w5, o5)):
        o[...] = jnp.dot(hb, w[...], preferred_element_type=F32).astype(o.dtype)


def in_proj(x, g, mod4, l, midx, ws, out_dtypes):
    r, d = x.shape
    in_specs = [pl.BlockSpec((TM, d), lambda i: (i, 0)), _full((1, d)),
                _mod_spec(l, midx, 0, d), _mod_spec(l, midx, 1, d)]
    in_specs += [_wspec(w) for w in ws]
    out_shape = [jax.ShapeDtypeStruct((r, d), BF16)]
    out_specs = [pl.BlockSpec((TM, d), lambda i: (i, 0))]
    for w, dt in zip(ws, out_dtypes):
        out_shape.append(jax.ShapeDtypeStruct((r, w.shape[1]), dt))
        out_specs.append(pl.BlockSpec((TM, w.shape[1]), lambda i: (i, 0)))
    return pl.pallas_call(
        _in_kernel, out_shape=out_shape, grid=(r // TM,), in_specs=in_specs, out_specs=out_specs,
        compiler_params=_cp("parallel"), name="in_proj",
    )(x, g, mod4, mod4, *[_warg(w) for w in ws])


def _s5_disc_kernel(are, aim, ldt, bre, bim, lam_re, lam_im, bbre, bbim):
    a_re, a_im = are[...], aim[...]
    dt = jnp.exp(ldt[...])
    mag = jnp.exp(a_re * dt)
    ang = a_im * dt
    ab_re = mag * jnp.cos(ang)
    ab_im = mag * jnp.sin(ang)
    den = a_re * a_re + a_im * a_im
    f_re = ((ab_re - 1.0) * a_re + ab_im * a_im) / den
    f_im = (ab_im * a_re - (ab_re - 1.0) * a_im) / den
    lam_re[...] = ab_re
    lam_im[...] = ab_im
    for c in range(S5_GROUP):
        bbre[c] = f_re * bre[c] - f_im * bim[c]
        bbim[c] = f_re * bim[c] + f_im * bre[c]


def s5_discretise(a_re, a_im, b_re, b_im, log_dt):
    shp = a_re.shape
    rows = int(np.prod(shp)) // LANE
    are = a_re.astype(F32).reshape(rows, LANE)
    aim = a_im.astype(F32).reshape(rows, LANE)
    ldt = jnp.broadcast_to(log_dt.astype(F32)[..., None], shp).reshape(rows, LANE)
    bre = jnp.moveaxis(b_re.astype(F32), -1, 0).reshape(S5_GROUP, rows, LANE)
    bim = jnp.moveaxis(b_im.astype(F32), -1, 0).reshape(S5_GROUP, rows, LANE)
    outs = pl.pallas_call(
        _s5_disc_kernel,
        out_shape=[jax.ShapeDtypeStruct((rows, LANE), F32)] * 2
        + [jax.ShapeDtypeStruct((S5_GROUP, rows, LANE), F32)] * 2,
        name="s5_discretise",
    )(are, aim, ldt, bre, bim)
    lam_re, lam_im = outs[0].reshape(shp), outs[1].reshape(shp)
    bb_re = jnp.moveaxis(outs[2].reshape((S5_GROUP,) + shp), 0, -1)
    bb_im = jnp.moveaxis(outs[3].reshape((S5_GROUP,) + shp), 0, -1)
    return lam_re, lam_im, bb_re, bb_im


def s5_pack(lam_re, lam_im, bb_re, bb_im, c_re, c_im):
    depth = lam_re.shape[0]
    gpb = LANE // S5_GROUP
    eye = jnp.eye(gpb, dtype=F32)

    def lam_blocks(v):
        return v.reshape(depth, 2, S5_JB, 1, S5_BW)

    lam = jnp.concatenate([lam_blocks(lam_re), lam_blocks(lam_im)], axis=-1)
    lam = jnp.broadcast_to(lam, (depth, 2, S5_JB, SUBLANE, 2 * S5_BW))

    def b_blocks(bb):
        v = bb.reshape(depth, 2, S5_JB, gpb, S5_STATE, S5_GROUP)
        return jnp.einsum('ldjgpc,gh->ldjgchp', v, eye).reshape(depth, 2, S5_JB, LANE, S5_BW)

    bblk = jnp.concatenate([b_blocks(bb_re), b_blocks(bb_im)], axis=-1).astype(BF16)

    def c_blocks(cc):
        v = cc.astype(F32).reshape(depth, 2, S5_JB, gpb, S5_GROUP, S5_STATE)
        return jnp.einsum('ldjgcp,gh->ldjhpgc', v, eye).reshape(depth, 2, S5_JB, S5_BW, LANE)

    cblk = jnp.concatenate([c_blocks(c_re), -c_blocks(c_im)], axis=-2).astype(BF16)
    return lam, bblk, cblk


def _cpow(re, im, n):
    out = None
    while n:
        if n & 1:
            out = (re, im) if out is None else (out[0] * re - out[1] * im, out[0] * im + out[1] * re)
        n >>= 1
        if n:
            re, im = re * re - im * im, 2.0 * re * im
    return out


def _s5_pass_kernel(*refs, tt, lseg, with_y):
    if with_y:
        u_ref, bblk, lam, cblk, sloc, h0, y_ref, hfin, uperm, hst, yperm = refs[:11]
    else:
        u_ref, bblk, lam, s_out, uperm, hst = refs[:6]
    bus = refs[-S5_JB:]
    d = pl.program_id(1)
    j = pl.program_id(2)

    @pl.when(j == 0)
    def _():
        if not with_y:
            hst[...] = jnp.zeros(hst.shape, F32)
        else:
            for jb in range(S5_JB):
                pr, pi = _cpow(lam[jb, 0:1, :S5_BW], lam[jb, 0:1, S5_BW:], lseg)

                def chain(order, jb=jb, pr=pr, pi=pi):
                    cr, ci = h0[jb, :, :S5_BW], h0[jb, :, S5_BW:]
                    for s in order:
                        hst[jb, s:s + 1, :S5_BW] = cr
                        hst[jb, s:s + 1, S5_BW:] = ci
                        sr, si = sloc[jb, s:s + 1, :S5_BW], sloc[jb, s:s + 1, S5_BW:]
                        cr, ci = pr * cr - pi * ci + sr, pr * ci + pi * cr + si
                    hfin[jb, :, :S5_BW] = cr
                    hfin[jb, :, S5_BW:] = ci

                @pl.when(d == 0)
                def _():
                    chain(range(S5_NSEG))

                @pl.when(d == 1)
                def _():
                    chain(range(S5_NSEG - 1, -1, -1))

    for s in range(S5_NSEG):
        us = u_ref[s].astype(F32)
        for k in range(S5_JB):
            uperm[k, pl.ds(s, tt, stride=S5_NSEG), :] = us[:, k * LANE:(k + 1) * LANE]

    for jb in range(S5_JB):
        bus[jb][...] = jnp.dot(uperm[jb].astype(BF16), bblk[jb], preferred_element_type=F32)

    for jb in range(S5_JB):
        bu = bus[jb]
        lr, li = lam[jb, :, :S5_BW], lam[jb, :, S5_BW:]

        def step(i, carry, lr=lr, li=li, bu=bu):
            hr, hi = carry
            t = i + d * (tt - 1 - 2 * i)
            r0 = pl.multiple_of(t * S5_NSEG, S5_NSEG)
            nr = lr * hr - li * hi + bu[pl.ds(r0, S5_NSEG), :S5_BW]
            ni = lr * hi + li * hr + bu[pl.ds(r0, S5_NSEG), S5_BW:]
            if with_y:
                bu[pl.ds(r0, S5_NSEG), :S5_BW] = nr
                bu[pl.ds(r0, S5_NSEG), S5_BW:] = ni
            return nr, ni

        hr, hi = lax.fori_loop(0, tt, step, (hst[jb, :, :S5_BW], hst[jb, :, S5_BW:]), unroll=True)
        hst[jb, :, :S5_BW] = hr
        hst[jb, :, S5_BW:] = hi

    if with_y:
        for jb in range(S5_JB):
            yperm[jb] = jnp.dot(bus[jb][...].astype(BF16), cblk[jb], preferred_element_type=F32)
        for s in range(S5_NSEG):
            for k in range(S5_JB):
                y_ref[s, :, k * LANE:(k + 1) * LANE] = (
                    yperm[k, pl.ds(s, tt, stride=S5_NSEG), :].astype(y_ref.dtype))
    else:
        s_out[...] = hst[...]


def s5_scan(u, bblk, lam, cblk, l, bsz, h0):
    seq_len = u.shape[0] // bsz
    lseg = seq_len // S5_NSEG
    tt = min(lseg, 128)
    nt = lseg // tt
    u5 = u.reshape(bsz, S5_NSEG, lseg, BR_W)

    def tile(d, j):
        return j + d * (nt - 1 - 2 * j)

    u_spec = pl.BlockSpec((None, S5_NSEG, tt, BR_W), lambda b, d, j: (b, 0, tile(d, j), 0))
    y_spec = pl.BlockSpec((None, None, S5_NSEG, tt, BR_W), lambda b, d, j: (d, b, 0, tile(d, j), 0))
    w_b = pl.BlockSpec((None, None, S5_JB, LANE, 2 * S5_BW), lambda b, d, j: (l, d, 0, 0, 0))
    w_lam = pl.BlockSpec((None, None, S5_JB, SUBLANE, 2 * S5_BW), lambda b, d, j: (l, d, 0, 0, 0))
    w_c = pl.BlockSpec((None, None, S5_JB, 2 * S5_BW, LANE), lambda b, d, j: (l, d, 0, 0, 0))
    st8 = pl.BlockSpec((None, None, S5_JB, SUBLANE, 2 * S5_BW), lambda b, d, j: (b, d, 0, 0, 0))
    st1 = pl.BlockSpec((None, None, S5_JB, 1, 2 * S5_BW), lambda b, d, j: (b, d, 0, 0, 0))
    n_rows = S5_NSEG * tt
    scratch = [pltpu.VMEM((S5_JB, n_rows, LANE), F32), pltpu.VMEM((S5_JB, SUBLANE, 2 * S5_BW), F32)]
    bus = [pltpu.VMEM((n_rows, 2 * S5_BW), F32)] * S5_JB
    grid = (bsz, 2, nt)
    cp = _cp("arbitrary", "arbitrary", "arbitrary")

    sloc = pl.pallas_call(
        functools.partial(_s5_pass_kernel, tt=tt, lseg=lseg, with_y=False),
        out_shape=jax.ShapeDtypeStruct((bsz, 2, S5_JB, SUBLANE, 2 * S5_BW), F32),
        grid=grid, in_specs=[u_spec, w_b, w_lam], out_specs=st8,
        scratch_shapes=scratch + bus, compiler_params=cp, name="s5_local",
    )(u5, bblk, lam)

    y, hfin = pl.pallas_call(
        functools.partial(_s5_pass_kernel, tt=tt, lseg=lseg, with_y=True),
        out_shape=[jax.ShapeDtypeStruct((2, bsz, S5_NSEG, lseg, BR_W), BF16),
                   jax.ShapeDtypeStruct((bsz, 2, S5_JB, 1, 2 * S5_BW), F32)],
        grid=grid, in_specs=[u_spec, w_b, w_lam, w_c, st8, st1], out_specs=[y_spec, st1],
        scratch_shapes=scratch + [pltpu.VMEM((S5_JB, n_rows, LANE), F32)] + bus,
        compiler_params=cp, name="s5_emit",
    )(u5, bblk, lam, cblk, sloc, h0)
    return y.reshape(2, bsz * seq_len, BR_W), hfin


def _sgu_kernel(z_ref, g_ref, b_ref, w_ref, bias_ref, o_ref):
    z = jax.nn.gelu(z_ref[...].astype(F32))
    u, v = z[:, :BR_W], z[:, BR_W:]
    mu = jnp.mean(v, axis=-1, keepdims=True)
    vc = v - mu
    vn = vc * lax.rsqrt(jnp.mean(vc * vc, axis=-1, keepdims=True) + 1e-5) * g_ref[...] + b_ref[...]
    n_chunk = z.shape[0] // SGU_CHUNK
    half = LANE // 2
    lane = lax.broadcasted_iota(jnp.int32, (SGU_CHUNK, LANE), 1)
    for k in range(BR_W // LANE):
        cols = []
        for c in range(n_chunk):
            blk = vn[c * SGU_CHUNK:(c + 1) * SGU_CHUNK, k * LANE:(k + 1) * LANE]
            lo = jnp.where(lane < half, blk, 0.0)
            cols.append(jnp.concatenate([lo, blk - lo], axis=0))
        rhs = jnp.concatenate(cols, axis=1).astype(BF16)
        mixed = jnp.dot(w_ref[k], rhs, preferred_element_type=F32)
        for c in range(n_chunk):
            rows = slice(c * SGU_CHUNK, (c + 1) * SGU_CHUNK)
            m = mixed[:, c * LANE:(c + 1) * LANE] + bias_ref[:, k * LANE:(k + 1) * LANE]
            o_ref[rows, k * LANE:(k + 1) * LANE] = (u[rows, k * LANE:(k + 1) * LANE] * m).astype(o_ref.dtype)


def sgu(z, ln_g, ln_b, w_pair, bias):
    r = z.shape[0]
    return pl.pallas_call(
        _sgu_kernel, out_shape=jax.ShapeDtypeStruct((r, BR_W), BF16), grid=(r // TM,),
        in_specs=[pl.BlockSpec((TM, 2 * BR_W), lambda i: (i, 0)), _full((1, BR_W)), _full((1, BR_W)),
                  _wspec(w_pair), _wspec(bias)],
        out_specs=pl.BlockSpec((TM, BR_W), lambda i: (i, 0)),
        compiler_params=_cp("parallel"), name="sgu",
    )(z, ln_g, ln_b, _warg(w_pair), _warg(bias))


def _ssd_prep_kernel(cur_ref, prev_ref, next_ref, dt_ref, w_ref, b_ref, dtb_ref, alane_ref, esel_ref,
                     xs_ref, bc_ref, da_ref, xt_ref, *, seq_len):
    i = pl.program_id(0)
    x = cur_ref[:, BR_W:].astype(F32)
    tm = x.shape[0]
    row = lax.broadcasted_iota(jnp.int32, (tm, 1), 0)
    pos = lax.rem(row + i * tm, seq_len)
    prev_row = prev_ref[SUBLANE - 1:SUBLANE, BR_W:].astype(F32)
    next_row = next_ref[0:1, BR_W:].astype(F32)
    x_prev = jnp.where(row == 0, prev_row, pltpu.roll(x, 1, axis=0))
    x_prev = jnp.where(pos == 0, 0.0, x_prev)
    x_next = jnp.where(row == tm - 1, next_row, pltpu.roll(x, tm - 1, axis=0))
    x_next = jnp.where(pos == seq_len - 1, 0.0, x_next)
    y = _silu(w_ref[0:1, :] * x_prev + w_ref[1:2, :] * x + w_ref[2:3, :] * x_next + b_ref[...])
    xs = y[:, :BR_W]
    xs_ref[...] = xs
    bc_ref[...] = y[:, BR_W:].astype(bc_ref.dtype)
    t = dt_ref[...] + dtb_ref[...]
    sp = jnp.maximum(t, 0.0) + jnp.log1p(jnp.exp(-jnp.abs(t)))
    lane = lax.broadcasted_iota(jnp.int32, t.shape, 1)
    d_a = sp * alane_ref[...]
    da_ref[0] = jnp.where(lane < SSD_HEADS, d_a, 0.0)
    da_ref[1] = jnp.where(lane < SSD_HEADS, pltpu.roll(d_a, LANE - SSD_HEADS, axis=1), 0.0)
    sp_hi = sp.astype(BF16)
    sp_lo = (sp - sp_hi.astype(F32)).astype(BF16)
    for d in range(2):
        dt_x = (jnp.dot(sp_hi, esel_ref[d], preferred_element_type=F32)
                + jnp.dot(sp_lo, esel_ref[d], preferred_element_type=F32))
        xt_ref[d] = (xs * dt_x).astype(xt_ref.dtype)


def ssd_prep(p_ssd, dt_raw, seq_len, conv_w, conv_b, dt_bias, a_lane, esel):
    r, w = p_ssd.shape
    nb = r // SUBLANE
    per = TM // SUBLANE
    return pl.pallas_call(
        functools.partial(_ssd_prep_kernel, seq_len=seq_len),
        out_shape=[jax.ShapeDtypeStruct((r, BR_W), F32), jax.ShapeDtypeStruct((r, 2 * SSD_GN), BF16),
                   jax.ShapeDtypeStruct((2, r, LANE), F32), jax.ShapeDtypeStruct((2, r, BR_W), BF16)],
        grid=(r // TM,),
        in_specs=[pl.BlockSpec((TM, w), lambda i: (i, 0)),
                  pl.BlockSpec((SUBLANE, w), lambda i: (jnp.maximum(i * per - 1, 0), 0)),
                  pl.BlockSpec((SUBLANE, w), lambda i: (jnp.minimum((i + 1) * per, nb - 1), 0)),
                  pl.BlockSpec((TM, LANE), lambda i: (i, 0)),
                  _full(conv_w.shape), _full(conv_b.shape), _full(dt_bias.shape), _full(a_lane.shape),
                  _full(esel.shape)],
        out_specs=[pl.BlockSpec((TM, BR_W), lambda i: (i, 0)), pl.BlockSpec((TM, 2 * SSD_GN), lambda i: (i, 0)),
                   pl.BlockSpec((2, TM, LANE), lambda i: (0, i, 0)), pl.BlockSpec((2, TM, BR_W), lambda i: (0, i, 0))],
        compiler_params=_cp("parallel"), name="ssd_prep",
    )(p_ssd, p_ssd, p_ssd, dt_raw, conv_w, conv_b, dt_bias, a_lane, esel)


def _ssd_scan_kernel(xt_ref, bc_ref, da_ref, tri_ref, h0_ref, y_ref, hfin_ref, st_ref, *, n_chunk, bsz):
    d = pl.program_id(0)
    j = pl.program_id(1)
    T = SSD_T

    @pl.when(j == 0)
    def _():
        st_ref[...] = h0_ref[...]

    tri = tri_ref[...]
    mask = tri > 0.5
    rep = SSD_HEADS // SSD_GROUPS

    def chunk(ci, carry):
        c = ci + d * (n_chunk - 1 - 2 * ci)
        r0 = pl.multiple_of(c * T, T)
        for b in range(bsz):
            d_a = da_ref[b, pl.ds(r0, T), :]
            a_cum = jnp.dot(tri, d_a, preferred_element_type=F32, precision=HIGHEST)
            a_cum_t = a_cum.T
            total = jnp.sum(d_a, axis=0, keepdims=True)
            xt = xt_ref[b, pl.ds(r0, T), :]
            bc = bc_ref[b, pl.ds(r0, T), :]
            bm_t = bc[:, :SSD_GN].astype(F32).T
            cm = bc[:, SSD_GN:]
            for g in range(SSD_GROUPS):
                gs = slice(g * SSD_STATE, (g + 1) * SSD_STATE)
                b_t = bm_t[gs, :]
                c_g = cm[:, gs]
                scores = jnp.dot(c_g, b_t.astype(BF16), preferred_element_type=F32)
                for hh in range(rep):
                    h = g * rep + hh
                    hs = slice(h * SSD_HEAD_DIM, (h + 1) * SSD_HEAD_DIM)
                    col = a_cum[:, h:h + 1]
                    rowv = a_cum_t[h:h + 1, :]
                    tot = total[:, h:h + 1]
                    decay = jnp.exp(jnp.where(mask, col - rowv, -1e30))
                    p = (scores * decay).astype(BF16)
                    x_h = xt[:, hs]
                    s_old = st_ref[b, h]
                    y_h = jnp.dot(p, x_h, preferred_element_type=F32)
                    y_h += jnp.dot(c_g, s_old.astype(BF16), preferred_element_type=F32) * jnp.exp(col)
                    bw = (b_t * jnp.exp(tot - rowv)).astype(BF16)
                    st_ref[b, h] = s_old * jnp.exp(tot) + jnp.dot(bw, x_h, preferred_element_type=F32)
                    y_ref[b, pl.ds(r0, T), hs] = y_h.astype(y_ref.dtype)
        return carry

    lax.fori_loop(0, n_chunk, chunk, 0, unroll=True)
    hfin_ref[...] = st_ref[...]


def ssd_scan(xt, bc, d_a, tri, bsz, h0):
    r = bc.shape[0]
    seq_len = r // bsz
    ts = min(seq_len, TM)
    nt = seq_len // ts

    def tile(d, j):
        return j + d * (nt - 1 - 2 * j)

    st = pl.BlockSpec((None, bsz, SSD_HEADS, SSD_STATE, SSD_HEAD_DIM), lambda d, j: (d, 0, 0, 0, 0))
    y, hfin = pl.pallas_call(
        functools.partial(_ssd_scan_kernel, n_chunk=ts // SSD_T, bsz=bsz),
        out_shape=[jax.ShapeDtypeStruct((2, bsz, seq_len, BR_W), BF16),
                   jax.ShapeDtypeStruct((2, bsz, SSD_HEADS, SSD_STATE, SSD_HEAD_DIM), F32)],
        grid=(2, nt),
        in_specs=[pl.BlockSpec((None, bsz, ts, BR_W), lambda d, j: (d, 0, tile(d, j), 0)),
                  pl.BlockSpec((bsz, ts, 2 * SSD_GN), lambda d, j: (0, tile(d, j), 0)),
                  pl.BlockSpec((None, bsz, ts, LANE), lambda d, j: (d, 0, tile(d, j), 0)),
                  pl.BlockSpec((None, SSD_T, SSD_T), lambda d, j: (d, 0, 0)), st],
        out_specs=[pl.BlockSpec((None, bsz, ts, BR_W), lambda d, j: (d, 0, tile(d, j), 0)), st],
        scratch_shapes=[pltpu.VMEM((bsz, SSD_HEADS, SSD_STATE, SSD_HEAD_DIM), F32)],
        compiler_params=_cp("arbitrary", "arbitrary"), name="ssd_scan",
    )(xt.reshape(2, bsz, seq_len, BR_W), bc.reshape(bsz, seq_len, 2 * SSD_GN),
      d_a.reshape(2, bsz, seq_len, LANE), tri, h0)
    return y.reshape(2, r, BR_W), hfin


def _rope_table_kernel(ang_ref, cos_ref, sin_ref):
    ang = ang_ref[...]
    lane = lax.broadcasted_iota(jnp.int32, ang.shape, 1)
    quarter = MLA_ROPE // 2
    first = (lane >= MLA_NOPE) & (lane < MLA_NOPE + quarter)
    second = (lane >= MLA_NOPE + quarter) & (lane < MLA_QK)
    c, s = jnp.cos(ang), jnp.sin(ang)
    cos_ref[...] = jnp.where(first | second, c, jnp.where(lane < MLA_NOPE, 1.0, 0.0))
    sin_ref[...] = jnp.where(first, -s, jnp.where(second, s, 0.0))


def _rope_partner(v):
    quarter = MLA_ROPE // 2
    src = np.arange(LANE)
    src[MLA_NOPE:MLA_NOPE + quarter] += quarter
    src[MLA_NOPE + quarter:MLA_QK] -= quarter
    valid = (np.arange(LANE) >= MLA_NOPE) & (np.arange(LANE) < MLA_QK)
    return jnp.where(valid, v[..., src], 0)


def rope_tables(seq_len):
    n_rows = seq_len // GRID_W
    pairs = MLA_ROPE // 4
    pos_row = jnp.repeat(jnp.arange(n_rows, dtype=F32), GRID_W)
    pos_col = jnp.tile(jnp.arange(GRID_W, dtype=F32), n_rows)
    inv_freq = ROPE_BASE ** (-jnp.arange(pairs, dtype=F32) / pairs)
    ang = jnp.concatenate([pos_row[:, None] * inv_freq, pos_col[:, None] * inv_freq], axis=-1)
    ang_pad = jnp.concatenate([jnp.zeros((seq_len, MLA_NOPE), F32), ang, ang,
                               jnp.zeros((seq_len, LANE - MLA_QK), F32)], axis=-1)
    tr = min(seq_len, 1024)
    spec = pl.BlockSpec((tr, LANE), lambda i: (i, 0))
    return pl.pallas_call(
        _rope_table_kernel, out_shape=[jax.ShapeDtypeStruct((seq_len, LANE), F32)] * 2,
        grid=(seq_len // tr,), in_specs=[spec], out_specs=[spec] * 2,
        compiler_params=_cp("parallel"), name="rope_tables",
    )(ang_pad)


def _mla_prep_kernel(*refs, rope):
    if rope:
        (p_ref, gq_ref, gkv_ref, wq_ref, wkv_ref, nq_ref, nk_ref, wqs_ref, nqs_ref, nks_ref, cos_ref, sin_ref,
         q_ref, k_ref, v_ref) = refs
    else:
        p_ref, gq_ref, gkv_ref, wq_ref, wkv_ref, nq_ref, nk_ref, q_ref, k_ref, v_ref = refs
    p = p_ref[...].astype(F32)
    cq = (_rms(p[:, :MLA_Q_LORA], MLA_Q_LORA) * gq_ref[...]).astype(BF16)
    ckv = (_rms(p[:, MLA_Q_LORA:MLA_Q_LORA + MLA_KV_LORA], MLA_KV_LORA) * gkv_ref[...]).astype(BF16)
    kr = p[:, MLA_Q_LORA + MLA_KV_LORA:MLA_Q_LORA + MLA_KV_LORA + LANE]
    q_all = jnp.dot(cq, wq_ref[...], preferred_element_type=F32)
    kv_all = jnp.dot(ckv, wkv_ref[...], preferred_element_type=F32)
    lane = lax.broadcasted_iota(jnp.int32, (1, LANE), 1)
    one_col = jnp.where(lane == MLA_V, 1.0, 0.0)
    scale = MLA_QK ** -0.5 * LOG2E
    if rope:
        qs_all = jnp.dot(cq, wqs_ref[...], preferred_element_type=F32)
        q_cos, q_sin = nq_ref[...] * cos_ref[...], nqs_ref[...] * sin_ref[...]
        k_cos = nk_ref[...] * cos_ref[...]
        k_part = p[:, MLA_Q_LORA + MLA_KV_LORA + LANE:] * (nks_ref[...] * sin_ref[...])

    def inv_rms(t):
        return lax.rsqrt(jnp.sum(t * t, axis=-1, keepdims=True) * (1.0 / MLA_QK) + 1e-6)

    for h in range(MLA_HEADS):
        q = q_all[:, h * LANE:(h + 1) * LANE]
        k = kv_all[:, 2 * h * LANE:(2 * h + 1) * LANE] + kr
        if rope:
            q_out = (q * q_cos + qs_all[:, h * LANE:(h + 1) * LANE] * q_sin) * (inv_rms(q) * scale)
            k_out = (k * k_cos + k_part) * inv_rms(k)
        else:
            q_out = q * nq_ref[...] * (inv_rms(q) * scale)
            k_out = k * nk_ref[...] * inv_rms(k)
        q_ref[h] = q_out.astype(q_ref.dtype)
        k_ref[h] = k_out.astype(k_ref.dtype)
        v_ref[h] = (kv_all[:, (2 * h + 1) * LANE:(2 * h + 2) * LANE] + one_col).astype(v_ref.dtype)


def mla_prep(p_mla, bsz, gq, gkv, wq, wkv, nq, nk, rope_args):
    r, w = p_mla.shape
    seq_len = r // bsz
    tr = min(seq_len, TM)
    nt = seq_len // tr
    rope = rope_args is not None
    in_specs = [pl.BlockSpec((tr, w), lambda b, i: (b * nt + i, 0)), _full(gq.shape), _full(gkv.shape),
                _wspec(wq), _wspec(wkv), _full(nq.shape), _full(nk.shape)]
    args = [p_mla, gq, gkv, _warg(wq), _warg(wkv), nq, nk]
    if rope:
        in_specs += [_wspec(a) for a in rope_args[:3]]
        in_specs += [pl.BlockSpec((tr, LANE), lambda b, i: (i, 0))] * 2
        args += [_warg(a) for a in rope_args]
    head = pl.BlockSpec((None, MLA_HEADS, tr, LANE), lambda b, i: (b, 0, i, 0))
    return pl.pallas_call(
        functools.partial(_mla_prep_kernel, rope=rope),
        out_shape=[jax.ShapeDtypeStruct((bsz, MLA_HEADS, seq_len, LANE), BF16)] * 3,
        grid=(bsz, nt), in_specs=in_specs, out_specs=[head] * 3,
        compiler_params=_cp("parallel", "parallel"), name="mla_prep",
    )(*args)


def _attn_kernel(*refs, n_kv):
    bound_ref, q_ref = refs[0], refs[1]
    kv_refs = refs[2:2 + 2 * n_kv]
    o_ref = refs[2 + 2 * n_kv]
    m_ref = refs[3 + 2 * n_kv]
    scores = [[lax.dot_general(q_ref[hh], kv_refs[2 * i][hh], (((1,), (1,)), ((), ())),
                               preferred_element_type=F32) for i in range(n_kv)] for hh in range(2)]

    def row_max(first_set):
        for hh in range(2):
            m = scores[hh][n_kv - 1].max(axis=-1, keepdims=True)
            for si in scores[hh][first_set:n_kv - 1]:
                m = jnp.maximum(m, si.max(axis=-1, keepdims=True))
            m_ref[hh] = m

    if n_kv > 1:
        fast = bound_ref[0] <= ATTN_FAST_BOUND

        @pl.when(fast)
        def _():
            row_max(n_kv - 1)

        @pl.when(jnp.logical_not(fast))
        def _():
            row_max(0)
    else:
        row_max(0)

    for hh in range(2):
        m = m_ref[hh]
        acc = None
        for i, si in enumerate(scores[hh]):
            pv = jnp.dot(jnp.exp2((si - m).astype(BF16)), kv_refs[2 * i + 1][hh], preferred_element_type=F32)
            acc = pv if acc is None else acc + pv
        o = acc[:, :MLA_V] / acc[:, MLA_V:MLA_V + 1]
        o_ref[:, hh * MLA_V:(hh + 1) * MLA_V] = o.astype(o_ref.dtype)


def attention(q, kvs, score_bound):
    bsz, nh, lq, _ = q.shape
    tq = min(lq, TQ)
    nq = lq // tq
    in_specs = [pl.BlockSpec(memory_space=pltpu.SMEM),
                pl.BlockSpec((None, 2, tq, LANE), lambda b, hp, i: (b, hp, i, 0))]
    args = [score_bound, q]
    for k, v in kvs:
        spec = pl.BlockSpec((None, 2, k.shape[2], LANE), lambda b, hp, i: (b, hp, 0, 0))
        in_specs += [spec, spec]
        args += [k, v]
    return pl.pallas_call(
        functools.partial(_attn_kernel, n_kv=len(kvs)),
        out_shape=jax.ShapeDtypeStruct((bsz * lq, nh * MLA_V), BF16),
        grid=(bsz, nh // 2, nq), in_specs=in_specs,
        out_specs=pl.BlockSpec((tq, 2 * MLA_V), lambda b, hp, i: (b * nq + i, hp)),
        scratch_shapes=[pltpu.VMEM((2, tq, 1), F32)],
        compiler_params=_cp("parallel", "parallel", "arbitrary"), name="attention",
    )(*args)


def _merge_kernel(x_ref, h_ref, u_ref, s5f_ref, s5b_ref, s5d_ref, wglu_ref, b_ref,
                  xs_ref, ssdf_ref, ssdb_ref, z_ref, ssdd_ref, ssdg_ref, d_ref,
                  g1_ref, wg_ref, wb_ref, wo_ref,
                  n2_ref, sh2_ref, sc2_ref, g2_ref, wi_ref, wfo_ref, o_ref, *, n_split):
    y = s5d_ref[...] * u_ref[...].astype(F32) + s5f_ref[...].astype(F32) + s5b_ref[...].astype(F32)
    g = jax.nn.gelu(y)
    a = (g * jax.nn.sigmoid(jnp.dot(g.astype(BF16), wglu_ref[...], preferred_element_type=F32))).astype(BF16)
    y = ssdd_ref[...] * xs_ref[...] + ssdf_ref[...].astype(F32) + ssdb_ref[...].astype(F32)
    c = (_rms(y * _silu(z_ref[...].astype(F32)), BR_W) * ssdg_ref[...]).astype(BF16)
    h = h_ref[...]
    dm = x_ref.shape[-1]
    merged = None
    for i, br in enumerate((a, b_ref[...], c, d_ref[...])):
        gate = jax.nn.sigmoid(jnp.dot(h, wg_ref[:, i * dm:(i + 1) * dm], preferred_element_type=F32))
        term = gate * jnp.dot(br, wb_ref[i], preferred_element_type=F32)
        merged = term if merged is None else merged + term
    mix = jnp.dot(merged.astype(BF16), wo_ref[...], preferred_element_type=F32)
    x = x_ref[...] + g1_ref[...] * mix

    y = _rms(x, dm) * n2_ref[...]
    hb = (y * (1.0 + sc2_ref[...]) + sh2_ref[...]).astype(BF16)
    hid = wfo_ref.shape[0]
    step = hid // n_split
    acc = None
    for c in range(n_split):
        gate = jnp.dot(hb, wi_ref[:, c * step:(c + 1) * step], preferred_element_type=F32)
        up = jnp.dot(hb, wi_ref[:, hid + c * step:hid + (c + 1) * step], preferred_element_type=F32)
        part = jnp.dot((_silu(gate) * up).astype(BF16), wfo_ref[c * step:(c + 1) * step, :],
                       preferred_element_type=F32)
        acc = part if acc is None else acc + part
    o_ref[...] = x + g2_ref[...] * acc


def merge_ffn(x, h, s5_in, b, ssd_in, d, mod4, l, midx, wg, wb, wo, norm2, wi, wfo):
    r, dm = x.shape
    n_split = 11 if wfo.shape[0] % (11 * LANE) == 0 else 1
    row = pl.BlockSpec((TM, dm), lambda i: (i, 0))
    br = pl.BlockSpec((TM, BR_W), lambda i: (i, 0))
    fwd = pl.BlockSpec((None, TM, BR_W), lambda i: (0, i, 0))
    bwd = pl.BlockSpec((None, TM, BR_W), lambda i: (1, i, 0))
    vec = _full((1, BR_W))
    u, y_s5, s5_d, w_glu = s5_in
    xs, y_ssd, p_ssd, ssd_d, ssd_g = ssd_in
    return pl.pallas_call(
        functools.partial(_merge_kernel, n_split=n_split),
        out_shape=jax.ShapeDtypeStruct((r, dm), F32), grid=(r // TM,),
        in_specs=[row, row, br, fwd, bwd, vec, _wspec(w_glu), br,
                  br, fwd, bwd, br, vec, vec, br,
                  _mod_spec(l, midx, 2, dm), _wspec(wg), _wspec(wb), _wspec(wo),
                  _full((1, dm)), _mod_spec(l, midx, 3, dm), _mod_spec(l, midx, 4, dm),
                  _mod_spec(l, midx, 5, dm), _wspec(wi), _wspec(wfo)],
        out_specs=row, compiler_params=_cp("parallel"), name="merge_ffn",
    )(x, h, u, y_s5, y_s5, s5_d, _warg(w_glu), b, xs, y_ssd, y_ssd, p_ssd, ssd_d, ssd_g, d, mod4,
      _warg(wg), _warg(wb), _warg(wo), norm2, mod4, mod4, mod4, _warg(wi), _warg(wfo))


def _pad_cols(w, n):
    return jnp.pad(w, ((0, 0),) * (w.ndim - 1) + ((0, n - w.shape[-1]),))


def kernel(x, c, ctx, c_ctx, w_ada, b_ada, norm1_g, norm2_g, w_in, s5_a_re, s5_a_im, s5_b_re, s5_b_im, s5_c_re, s5_c_im, s5_log_dt, s5_d, s5_w_glu, sgu_ln_g, sgu_ln_b, sgu_w_s, sgu_b_s, ssd_conv_w, ssd_conv_b, ssd_a_log, ssd_dt_bias, ssd_d, ssd_norm_g, mla_q_a_norm, mla_w_uq, mla_kv_a_norm, mla_w_ukv, mla_q_norm, mla_k_norm, w_branch, w_out, w_ffn_in, w_ffn_out):
    bsz, seq, dm = x.shape
    lc = ctx.shape[1]
    depth = w_ada.shape[0]
    assert seq % TM == 0 and (bsz * lc) % TM == 0 and seq % (S5_NSEG * SUBLANE) == 0
    assert lc % SSD_T == 0 and lc % (S5_NSEG * SUBLANE) == 0 and bsz + 1 <= 8

    cc8 = jnp.zeros((8, dm), F32).at[:bsz].set(c.astype(F32)).at[bsz].set(c_ctx.astype(F32))
    mod4 = ada_table(cc8, w_ada.astype(F32), b_ada.astype(F32)).reshape(depth, 8, 1, 6 * dm)
    lat_tiles = seq // TM
    midx_lat = lambda i: i // lat_tiles
    midx_ctx = lambda i: bsz

    off = np.cumsum([0, BR_W, 2 * BR_W, BR_W + SSD_CONV_CH + 2 * SSD_HEADS,
                     MLA_Q_LORA + MLA_KV_LORA + MLA_ROPE, N_BRANCH * dm])
    w_s5 = w_in[:, :, off[0]:off[1]].astype(BF16)
    w_sgu = w_in[:, :, off[1]:off[2]].astype(BF16)
    w_ssd = w_in[:, :, off[2]:off[2] + BR_W + SSD_CONV_CH].astype(BF16)
    w_dt = _pad_cols(w_in[:, :, off[2] + BR_W + SSD_CONV_CH:off[3]], LANE).astype(BF16)
    w_mla_main = w_in[:, :, off[3]:off[3] + MLA_Q_LORA + MLA_KV_LORA]
    w_kr = w_in[:, :, off[3] + MLA_Q_LORA + MLA_KV_LORA:off[4]]
    zeros = lambda n: jnp.zeros((depth, dm, n), w_in.dtype)
    kr_block = jnp.concatenate([zeros(MLA_NOPE), w_kr, zeros(LANE - MLA_QK)], axis=-1)
    w_mla = jnp.concatenate([w_mla_main, kr_block, _rope_partner(kr_block)], axis=-1).astype(BF16)
    w_gate = w_in[:, :, off[4]:off[5]].astype(BF16)

    lam_re, lam_im, bb_re, bb_im = s5_discretise(s5_a_re, s5_a_im, s5_b_re, s5_b_im, s5_log_dt)
    lam, bblk, cblk = s5_pack(lam_re, lam_im, bb_re, bb_im, s5_c_re, s5_c_im)
    s5_wg = s5_w_glu.astype(BF16)

    sgu_w = sgu_w_s.reshape(depth, SGU_HEADS // 2, 2, SGU_CHUNK, SGU_CHUNK)
    sgu_w = jnp.concatenate([sgu_w[:, :, 0], sgu_w[:, :, 1]], axis=-1).astype(BF16)
    sgu_b = jnp.repeat(jnp.swapaxes(sgu_b_s, 1, 2), BR_W // SGU_HEADS, axis=-1).astype(F32)

    a_neg = -jnp.exp(ssd_a_log.astype(F32))
    a_lane = _pad_cols(a_neg.reshape(depth, 1, 2 * SSD_HEADS), LANE)
    head_of_lane = jnp.arange(LANE)[None, :, None] - SSD_HEADS * jnp.arange(2)[:, None, None]
    col_head = jnp.arange(BR_W) // SSD_HEAD_DIM
    esel = ((head_of_lane == col_head[None, None, :]) & (head_of_lane >= 0)).astype(BF16)
    ti = jnp.arange(SSD_T)
    tri = jnp.stack([ti[None, :] <= ti[:, None], ti[None, :] >= ti[:, None]]).astype(F32)
    dt_bias = _pad_cols(ssd_dt_bias.astype(F32).reshape(depth, 1, 2 * SSD_HEADS), LANE)
    ssd_dskip = jnp.repeat(ssd_d.astype(F32), SSD_HEAD_DIM, axis=-1)[:, None, :]

    wq = mla_w_uq.reshape(depth, MLA_Q_LORA, MLA_HEADS, MLA_QK)
    wq = _pad_cols(wq, LANE)
    wq_sw = _rope_partner(wq).reshape(depth, MLA_Q_LORA, MLA_HEADS * LANE).astype(BF16)
    wq = wq.reshape(depth, MLA_Q_LORA, MLA_HEADS * LANE).astype(BF16)
    wkv = mla_w_ukv.reshape(depth, MLA_KV_LORA, MLA_HEADS, 2, MLA_NOPE)
    wkv = _pad_cols(wkv, LANE).reshape(depth, MLA_KV_LORA, MLA_HEADS * 2 * LANE).astype(BF16)
    nq = _pad_cols(mla_q_norm.astype(F32), LANE)[:, None, :]
    nk = _pad_cols(mla_k_norm.astype(F32), LANE)[:, None, :]
    nq_sw, nk_sw = _rope_partner(nq), _rope_partner(nk)
    rope_cos, rope_sin = rope_tables(seq)
    score_bound = (MLA_QK ** 0.5 * LOG2E * jnp.max(jnp.abs(mla_q_norm.astype(F32)), axis=-1)
                   * jnp.max(jnp.abs(mla_k_norm.astype(F32)), axis=-1))

    wb = w_branch.astype(BF16)
    wo = w_out.astype(BF16)
    wfi = w_ffn_in.astype(BF16)
    wfo = w_ffn_out.astype(BF16)

    row = lambda v, l: v[l].astype(F32).reshape(1, -1)
    x_lat = x.astype(F32).reshape(bsz * seq, dm)
    x_ctx = ctx.astype(F32).reshape(bsz * lc, dm)
    s5_zero = jnp.zeros((bsz, 2, S5_JB, 1, 2 * S5_BW), F32)
    ssd_zero = jnp.zeros((2, bsz, SSD_HEADS, SSD_STATE, SSD_HEAD_DIM), F32)

    for l in range(depth):
        need_ctx = l < depth - 1
        lay = lambda w: _Layer(w, l)
        ws = tuple(lay(w) for w in (w_s5, w_sgu, w_ssd, w_dt, w_mla))
        w_merge = (lay(w_gate), lay(wb), lay(wo))
        w_ffn = (lay(wfi), lay(wfo))
        dts = (BF16, BF16, BF16, F32, BF16)
        g1 = row(norm1_g, l)
        h_l, u_l, z_l, p_l, dtr_l, m_l = in_proj(x_lat, g1, mod4, l, midx_lat, ws, dts)
        h_c, u_c, z_c, p_c, dtr_c, m_c = in_proj(x_ctx, g1, mod4, l, midx_ctx, ws, dts)

        y_c, s5_h = s5_scan(u_c, bblk, lam, cblk, l, bsz, s5_zero)
        y_l, _ = s5_scan(u_l, bblk, lam, cblk, l, bsz, s5_h)
        s5_tail = (row(s5_d, l), lay(s5_wg))
        b_l = sgu(z_l, row(sgu_ln_g, l), row(sgu_ln_b, l), lay(sgu_w), lay(sgu_b))
        conv_w, conv_b = ssd_conv_w[l].astype(F32), row(ssd_conv_b, l)
        xs_c, bc_c, da_c, xt_c = ssd_prep(p_c, dtr_c, lc, conv_w, conv_b, dt_bias[l], a_lane[l], esel)
        xs_l, bc_l, da_l, xt_l = ssd_prep(p_l, dtr_l, seq, conv_w, conv_b, dt_bias[l], a_lane[l], esel)
        yc_c, ssd_h = ssd_scan(xt_c, bc_c, da_c, tri, bsz, ssd_zero)
        yc_l, _ = ssd_scan(xt_l, bc_l, da_l, tri, bsz, ssd_h)
        ssd_tail = (ssd_dskip[l], row(ssd_norm_g, l))
        mla_w = (row(mla_q_a_norm, l), row(mla_kv_a_norm, l), lay(wq), lay(wkv), nq[l], nk[l])
        q_c, k_c, v_c = mla_prep(m_c, bsz, *mla_w, None)
        q_l, k_l, v_l = mla_prep(m_l, bsz, *mla_w, (lay(wq_sw), nq_sw[l], nk_sw[l], rope_cos, rope_sin))
        d_l = attention(q_l, [(k_l, v_l), (k_c, v_c)], score_bound[l:l + 1])

        x_lat = merge_ffn(x_lat, h_l, (u_l, y_l) + s5_tail, b_l, (xs_l, yc_l, p_l) + ssd_tail, d_l,
                          mod4, l, midx_lat, *w_merge, row(norm2_g, l), *w_ffn)
        if need_ctx:
            b_c = sgu(z_c, row(sgu_ln_g, l), row(sgu_ln_b, l), lay(sgu_w), lay(sgu_b))
            d_c = attention(q_c, [(k_c, v_c)], score_bound[l:l + 1])
            x_ctx = merge_ffn(x_ctx, h_c, (u_c, y_c) + s5_tail, b_c, (xs_c, yc_c, p_c) + ssd_tail, d_c,
                              mod4, l, midx_ctx, *w_merge, row(norm2_g, l), *w_ffn)
    return x_lat.reshape(bsz, seq, dm).astype(x.dtype)
```

```python
import functools

import jax
import jax.numpy as jnp
import numpy as np
from jax import lax
from jax.experimental import pallas as pl
from jax.experimental.pallas import tpu as pltpu

F32 = jnp.float32
BF16 = jnp.bfloat16
HIGHEST = lax.Precision.HIGHEST

LANE = 128
SUBLANE = 8
VMEM_LIMIT = 56 * 1024 * 1024

GRID_W = 64
BR_W = 384
S5_GROUP = 16
S5_GROUPS = BR_W // S5_GROUP
S5_STATE = 64
S5_NSEG = SUBLANE
S5_JB = BR_W // LANE
S5_BW = (LANE // S5_GROUP) * S5_STATE
SGU_CHUNK = 128
SGU_HEADS = 6
SSD_HEADS = 6
SSD_HEAD_DIM = 64
SSD_GROUPS = 2
SSD_STATE = 64
SSD_GN = SSD_GROUPS * SSD_STATE
SSD_CONV_CH = BR_W + 2 * SSD_GN
SSD_T = 128
MLA_HEADS = 6
MLA_NOPE = 64
MLA_ROPE = 32
MLA_V = 64
MLA_QK = MLA_NOPE + MLA_ROPE
MLA_Q_LORA = 384
MLA_KV_LORA = 256
ROPE_BASE = 10000.0
LOG2E = 1.4426950408889634
N_BRANCH = 4

TM = 512
TQ = 256
ATTN_FAST_BOUND = 30.0


def _cp(*sem):
    return pltpu.CompilerParams(dimension_semantics=sem, vmem_limit_bytes=VMEM_LIMIT)


def _full(shape):
    n = len(shape)
    return pl.BlockSpec(shape, lambda *_: (0,) * n)


class _Layer:
    def __init__(self, arr, l):
        self.arr, self.l = arr, l

    @property
    def shape(self):
        return self.arr.shape[1:]


def _wspec(w):
    if not isinstance(w, _Layer):
        return _full(w.shape)
    n, l = len(w.shape), w.l
    return pl.BlockSpec((None,) + tuple(w.shape), lambda *_: (l,) + (0,) * n, pipeline_mode=pl.Buffered(1))


def _warg(w):
    return w.arr if isinstance(w, _Layer) else w


def _silu(x):
    return x * jax.nn.sigmoid(x)


def _rms(x, n, eps=1e-6):
    return x * lax.rsqrt(jnp.sum(x * x, axis=-1, keepdims=True) * (1.0 / n) + eps)


def _ada_kernel(cc_ref, w_ref, b_ref, o_ref):
    s = _silu(cc_ref[...])
    o_ref[...] = jnp.dot(s, w_ref[...], preferred_element_type=F32, precision=HIGHEST) + b_ref[...]


def ada_table(cc8, w_ada, b_ada):
    depth, d, n = w_ada.shape
    tn = n // 4
    return pl.pallas_call(
        _ada_kernel,
        out_shape=jax.ShapeDtypeStruct((depth, 8, n), F32),
        grid=(depth, n // tn),
        in_specs=[pl.BlockSpec((8, d), lambda l, j: (0, 0)),
                  pl.BlockSpec((None, d, tn), lambda l, j: (l, 0, j)),
                  pl.BlockSpec((None, 1, tn), lambda l, j: (l, 0, j))],
        out_specs=pl.BlockSpec((None, 8, tn), lambda l, j: (l, 0, j)),
        compiler_params=_cp("arbitrary", "arbitrary"),
        name="ada_table",
    )(cc8, w_ada, b_ada.reshape(depth, 1, n))


def _mod_spec(l, midx, col, d):
    return pl.BlockSpec((None, None, 1, d), lambda i: (l, midx(i), 0, col))


def _in_kernel(x_ref, g_ref, sh_ref, sc_ref, w1, w2, w3, w4, w5, h_ref, o1, o2, o3, o4, o5):
    x = x_ref[...]
    y = _rms(x, x.shape[-1]) * g_ref[...]
    hb = (y * (1.0 + sc_ref[...]) + sh_ref[...]).astype(BF16)
    h_ref[...] = hb
    for w, o in ((w1, o1), (w2, o2), (w3, o3), (w4, o4), (w5, o5)):
        o[...] = jnp.dot(hb, w[...], preferred_element_type=F32).astype(o.dtype)


def in_proj(x, g, mod4, l, midx, ws, out_dtypes):
    r, d = x.shape
    in_specs = [pl.BlockSpec((TM, d), lambda i: (i, 0)), _full((1, d)),
                _mod_spec(l, midx, 0, d), _mod_spec(l, midx, 1, d)]
    in_specs += [_wspec(w) for w in ws]
    out_shape = [jax.ShapeDtypeStruct((r, d), BF16)]
    out_specs = [pl.BlockSpec((TM, d), lambda i: (i, 0))]
    for w, dt in zip(ws, out_dtypes):
        out_shape.append(jax.ShapeDtypeStruct((r, w.shape[1]), dt))
        out_specs.append(pl.BlockSpec((TM, w.shape[1]), lambda i: (i, 0)))
    return pl.pallas_call(
        _in_kernel, out_shape=out_shape, grid=(r // TM,), in_specs=in_specs, out_specs=out_specs,
        compiler_params=_cp("parallel"), name="in_proj",
    )(x, g, mod4, mod4, *[_warg(w) for w in ws])


def _s5_disc_kernel(are, aim, ldt, bre, bim, lam_re, lam_im, bbre, bbim):
    a_re, a_im = are[...], aim[...]
    dt = jnp.exp(ldt[...])
    mag = jnp.exp(a_re * dt)
    ang = a_im * dt
    ab_re = mag * jnp.cos(ang)
    ab_im = mag * jnp.sin(ang)
    den = a_re * a_re + a_im * a_im
    f_re = ((ab_re - 1.0) * a_re + ab_im * a_im) / den
    f_im = (ab_im * a_re - (ab_re - 1.0) * a_im) / den
    lam_re[...] = ab_re
    lam_im[...] = ab_im
    for c in range(S5_GROUP):
        bbre[c] = f_re * bre[c] - f_im * bim[c]
        bbim[c] = f_re * bim[c] + f_im * bre[c]


def s5_discretise(a_re, a_im, b_re, b_im, log_dt):
    shp = a_re.shape
    rows = int(np.prod(shp)) // LANE
    are = a_re.astype(F32).reshape(rows, LANE)
    aim = a_im.astype(F32).reshape(rows, LANE)
    ldt = jnp.broadcast_to(log_dt.astype(F32)[..., None], shp).reshape(rows, LANE)
    bre = jnp.moveaxis(b_re.astype(F32), -1, 0).reshape(S5_GROUP, rows, LANE)
    bim = jnp.moveaxis(b_im.astype(F32), -1, 0).reshape(S5_GROUP, rows, LANE)
    outs = pl.pallas_call(
        _s5_disc_kernel,
        out_shape=[jax.ShapeDtypeStruct((rows, LANE), F32)] * 2
        + [jax.ShapeDtypeStruct((S5_GROUP, rows, LANE), F32)] * 2,
        name="s5_discretise",
    )(are, aim, ldt, bre, bim)
    lam_re, lam_im = outs[0].reshape(shp), outs[1].reshape(shp)
    bb_re = jnp.moveaxis(outs[2].reshape((S5_GROUP,) + shp), 0, -1)
    bb_im = jnp.moveaxis(outs[3].reshape((S5_GROUP,) + shp), 0, -1)
    return lam_re, lam_im, bb_re, bb_im


def s5_pack(lam_re, lam_im, bb_re, bb_im, c_re, c_im):
    depth = lam_re.shape[0]
    gpb = LANE // S5_GROUP
    eye = jnp.eye(gpb, dtype=F32)

    def lam_blocks(v):
        return v.reshape(depth, 2, S5_JB, 1, S5_BW)

    lam = jnp.concatenate([lam_blocks(lam_re), lam_blocks(lam_im)], axis=-1)
    lam = jnp.broadcast_to(lam, (depth, 2, S5_JB, SUBLANE, 2 * S5_BW))

    def b_blocks(bb):
        v = bb.reshape(depth, 2, S5_JB, gpb, S5_STATE, S5_GROUP)
        return jnp.einsum('ldjgpc,gh->ldjgchp', v, eye).reshape(depth, 2, S5_JB, LANE, S5_BW)

    bblk = jnp.concatenate([b_blocks(bb_re), b_blocks(bb_im)], axis=-1).astype(BF16)

    def c_blocks(cc):
        v = cc.astype(F32).reshape(depth, 2, S5_JB, gpb, S5_GROUP, S5_STATE)
        return jnp.einsum('ldjgcp,gh->ldjhpgc', v, eye).reshape(depth, 2, S5_JB, S5_BW, LANE)

    cblk = jnp.concatenate([c_blocks(c_re), -c_blocks(c_im)], axis=-2).astype(BF16)
    return lam, bblk, cblk


def _cpow(re, im, n):
    out = None
    while n:
        if n & 1:
            out = (re, im) if out is None else (out[0] * re - out[1] * im, out[0] * im + out[1] * re)
        n >>= 1
        if n:
            re, im = re * re - im * im, 2.0 * re * im
    return out


def _s5_pass_kernel(*refs, tt, lseg, with_y):
    if with_y:
        u_ref, bblk, lam, cblk, sloc, h0, y_ref, hfin, uperm, hst, yperm = refs[:11]
    else:
        u_ref, bblk, lam, s_out, uperm, hst = refs[:6]
    bus = refs[-S5_JB:]
    d = pl.program_id(1)
    j = pl.program_id(2)

    @pl.when(j == 0)
    def _():
        if not with_y:
            hst[...] = jnp.zeros(hst.shape, F32)
        else:
            for jb in range(S5_JB):
                pr, pi = _cpow(lam[jb, 0:1, :S5_BW], lam[jb, 0:1, S5_BW:], lseg)

                def chain(order, jb=jb, pr=pr, pi=pi):
                    cr, ci = h0[jb, :, :S5_BW], h0[jb, :, S5_BW:]
                    for s in order:
                        hst[jb, s:s + 1, :S5_BW] = cr
                        hst[jb, s:s + 1, S5_BW:] = ci
                        sr, si = sloc[jb, s:s + 1, :S5_BW], sloc[jb, s:s + 1, S5_BW:]
                        cr, ci = pr * cr - pi * ci + sr, pr * ci + pi * cr + si
                    hfin[jb, :, :S5_BW] = cr
                    hfin[jb, :, S5_BW:] = ci

                @pl.when(d == 0)
                def _():
                    chain(range(S5_NSEG))

                @pl.when(d == 1)
                def _():
                    chain(range(S5_NSEG - 1, -1, -1))

    for s in range(S5_NSEG):
        us = u_ref[s].astype(F32)
        for k in range(S5_JB):
            uperm[k, pl.ds(s, tt, stride=S5_NSEG), :] = us[:, k * LANE:(k + 1) * LANE]

    for jb in range(S5_JB):
        bus[jb][...] = jnp.dot(uperm[jb].astype(BF16), bblk[jb], preferred_element_type=F32)

    for jb in range(S5_JB):
        bu = bus[jb]
        lr, li = lam[jb, :, :S5_BW], lam[jb, :, S5_BW:]

        def step(i, carry, lr=lr, li=li, bu=bu):
            hr, hi = carry
            t = i + d * (tt - 1 - 2 * i)
            r0 = pl.multiple_of(t * S5_NSEG, S5_NSEG)
            nr = lr * hr - li * hi + bu[pl.ds(r0, S5_NSEG), :S5_BW]
            ni = lr * hi + li * hr + bu[pl.ds(r0, S5_NSEG), S5_BW:]
            if with_y:
                bu[pl.ds(r0, S5_NSEG), :S5_BW] = nr
                bu[pl.ds(r0, S5_NSEG), S5_BW:] = ni
            return nr, ni

        hr, hi = lax.fori_loop(0, tt, step, (hst[jb, :, :S5_BW], hst[jb, :, S5_BW:]), unroll=True)
        hst[jb, :, :S5_BW] = hr
        hst[jb, :, S5_BW:] = hi

    if with_y:
        for jb in range(S5_JB):
            yperm[jb] = jnp.dot(bus[jb][...].astype(BF16), cblk[jb], preferred_element_type=F32)
        for s in range(S5_NSEG):
            for k in range(S5_JB):
                y_ref[s, :, k * LANE:(k + 1) * LANE] = (
                    yperm[k, pl.ds(s, tt, stride=S5_NSEG), :].astype(y_ref.dtype))
    else:
        s_out[...] = hst[...]


def s5_scan(u, bblk, lam, cblk, l, bsz, h0):
    seq_len = u.shape[0] // bsz
    lseg = seq_len // S5_NSEG
    tt = min(lseg, 128)
    nt = lseg // tt
    u5 = u.reshape(bsz, S5_NSEG, lseg, BR_W)

    def tile(d, j):
        return j + d * (nt - 1 - 2 * j)

    u_spec = pl.BlockSpec((None, S5_NSEG, tt, BR_W), lambda b, d, j: (b, 0, tile(d, j), 0))
    y_spec = pl.BlockSpec((None, None, S5_NSEG, tt, BR_W), lambda b, d, j: (d, b, 0, tile(d, j), 0))
    w_b = pl.BlockSpec((None, None, S5_JB, LANE, 2 * S5_BW), lambda b, d, j: (l, d, 0, 0, 0))
    w_lam = pl.BlockSpec((None, None, S5_JB, SUBLANE, 2 * S5_BW), lambda b, d, j: (l, d, 0, 0, 0))
    w_c = pl.BlockSpec((None, None, S5_JB, 2 * S5_BW, LANE), lambda b, d, j: (l, d, 0, 0, 0))
    st8 = pl.BlockSpec((None, None, S5_JB, SUBLANE, 2 * S5_BW), lambda b, d, j: (b, d, 0, 0, 0))
    st1 = pl.BlockSpec((None, None, S5_JB, 1, 2 * S5_BW), lambda b, d, j: (b, d, 0, 0, 0))
    n_rows = S5_NSEG * tt
    scratch = [pltpu.VMEM((S5_JB, n_rows, LANE), F32), pltpu.VMEM((S5_JB, SUBLANE, 2 * S5_BW), F32)]
    bus = [pltpu.VMEM((n_rows, 2 * S5_BW), F32)] * S5_JB
    grid = (bsz, 2, nt)
    cp = _cp("arbitrary", "arbitrary", "arbitrary")

    sloc = pl.pallas_call(
        functools.partial(_s5_pass_kernel, tt=tt, lseg=lseg, with_y=False),
        out_shape=jax.ShapeDtypeStruct((bsz, 2, S5_JB, SUBLANE, 2 * S5_BW), F32),
        grid=grid, in_specs=[u_spec, w_b, w_lam], out_specs=st8,
        scratch_shapes=scratch + bus, compiler_params=cp, name="s5_local",
    )(u5, bblk, lam)

    y, hfin = pl.pallas_call(
        functools.partial(_s5_pass_kernel, tt=tt, lseg=lseg, with_y=True),
        out_shape=[jax.ShapeDtypeStruct((2, bsz, S5_NSEG, lseg, BR_W), BF16),
                   jax.ShapeDtypeStruct((bsz, 2, S5_JB, 1, 2 * S5_BW), F32)],
        grid=grid, in_specs=[u_spec, w_b, w_lam, w_c, st8, st1], out_specs=[y_spec, st1],
        scratch_shapes=scratch + [pltpu.VMEM((S5_JB, n_rows, LANE), F32)] + bus,
        compiler_params=cp, name="s5_emit",
    )(u5, bblk, lam, cblk, sloc, h0)
    return y.reshape(2, bsz * seq_len, BR_W), hfin


def _sgu_kernel(z_ref, g_ref, b_ref, w_ref, bias_ref, o_ref):
    z = jax.nn.gelu(z_ref[...].astype(F32))
    u, v = z[:, :BR_W], z[:, BR_W:]
    mu = jnp.mean(v, axis=-1, keepdims=True)
    vc = v - mu
    vn = vc * lax.rsqrt(jnp.mean(vc * vc, axis=-1, keepdims=True) + 1e-5) * g_ref[...] + b_ref[...]
    n_chunk = z.shape[0] // SGU_CHUNK
    half = LANE // 2
    lane = lax.broadcasted_iota(jnp.int32, (SGU_CHUNK, LANE), 1)
    for k in range(BR_W // LANE):
        cols = []
        for c in range(n_chunk):
            blk = vn[c * SGU_CHUNK:(c + 1) * SGU_CHUNK, k * LANE:(k + 1) * LANE]
            lo = jnp.where(lane < half, blk, 0.0)
            cols.append(jnp.concatenate([lo, blk - lo], axis=0))
        rhs = jnp.concatenate(cols, axis=1).astype(BF16)
        mixed = jnp.dot(w_ref[k], rhs, preferred_element_type=F32)
        for c in range(n_chunk):
            rows = slice(c * SGU_CHUNK, (c + 1) * SGU_CHUNK)
            m = mixed[:, c * LANE:(c + 1) * LANE] + bias_ref[:, k * LANE:(k + 1) * LANE]
            o_ref[rows, k * LANE:(k + 1) * LANE] = (u[rows, k * LANE:(k + 1) * LANE] * m).astype(o_ref.dtype)


def sgu(z, ln_g, ln_b, w_pair, bias):
    r = z.shape[0]
    return pl.pallas_call(
        _sgu_kernel, out_shape=jax.ShapeDtypeStruct((r, BR_W), BF16), grid=(r // TM,),
        in_specs=[pl.BlockSpec((TM, 2 * BR_W), lambda i: (i, 0)), _full((1, BR_W)), _full((1, BR_W)),
                  _wspec(w_pair), _wspec(bias)],
        out_specs=pl.BlockSpec((TM, BR_W), lambda i: (i, 0)),
        compiler_params=_cp("parallel"), name="sgu",
    )(z, ln_g, ln_b, _warg(w_pair), _warg(bias))


def _ssd_prep_kernel(cur_ref, prev_ref, next_ref, dt_ref, w_ref, b_ref, dtb_ref, alane_ref, esel_ref, tri_ref,
                     xs_ref, bc_ref, ac_ref, act_ref, xt_ref, *, seq_len):
    i = pl.program_id(0)
    x = cur_ref[:, BR_W:].astype(F32)
    tm = x.shape[0]
    row = lax.broadcasted_iota(jnp.int32, (tm, 1), 0)
    pos = lax.rem(row + i * tm, seq_len)
    prev_row = prev_ref[SUBLANE - 1:SUBLANE, BR_W:].astype(F32)
    next_row = next_ref[0:1, BR_W:].astype(F32)
    x_prev = jnp.where(row == 0, prev_row, pltpu.roll(x, 1, axis=0))
    x_prev = jnp.where(pos == 0, 0.0, x_prev)
    x_next = jnp.where(row == tm - 1, next_row, pltpu.roll(x, tm - 1, axis=0))
    x_next = jnp.where(pos == seq_len - 1, 0.0, x_next)
    y = _silu(w_ref[0:1, :] * x_prev + w_ref[1:2, :] * x + w_ref[2:3, :] * x_next + b_ref[...])
    xs = y[:, :BR_W]
    xs_ref[...] = xs
    bc_ref[...] = y[:, BR_W:].astype(bc_ref.dtype)
    t = dt_ref[...] + dtb_ref[...]
    sp = jnp.maximum(t, 0.0) + jnp.log1p(jnp.exp(-jnp.abs(t)))
    lane = lax.broadcasted_iota(jnp.int32, t.shape, 1)
    d_a = sp * alane_ref[...]
    d_a = (jnp.where(lane < SSD_HEADS, d_a, 0.0),
           jnp.where(lane < SSD_HEADS, pltpu.roll(d_a, LANE - SSD_HEADS, axis=1), 0.0))
    for d in range(2):
        for c in range(tm // SSD_T):
            rows = slice(c * SSD_T, (c + 1) * SSD_T)
            a_cum = jnp.dot(tri_ref[d], d_a[d][rows], preferred_element_type=F32, precision=HIGHEST)
            ac_ref[d, rows, :] = a_cum
            act_ref[d, c] = a_cum.T[:SUBLANE, :]
    sp_hi = sp.astype(BF16)
    sp_lo = (sp - sp_hi.astype(F32)).astype(BF16)
    for d in range(2):
        dt_x = (jnp.dot(sp_hi, esel_ref[d], preferred_element_type=F32)
                + jnp.dot(sp_lo, esel_ref[d], preferred_element_type=F32))
        xt_ref[d] = (xs * dt_x).astype(xt_ref.dtype)


def ssd_prep(p_ssd, dt_raw, seq_len, conv_w, conv_b, dt_bias, a_lane, esel, tri):
    r, w = p_ssd.shape
    nb = r // SUBLANE
    per = TM // SUBLANE
    cpt = TM // SSD_T
    return pl.pallas_call(
        functools.partial(_ssd_prep_kernel, seq_len=seq_len),
        out_shape=[jax.ShapeDtypeStruct((r, BR_W), F32), jax.ShapeDtypeStruct((r, 2 * SSD_GN), BF16),
                   jax.ShapeDtypeStruct((2, r, LANE), F32),
                   jax.ShapeDtypeStruct((2, r // SSD_T, SUBLANE, SSD_T), F32),
                   jax.ShapeDtypeStruct((2, r, BR_W), BF16)],
        grid=(r // TM,),
        in_specs=[pl.BlockSpec((TM, w), lambda i: (i, 0)),
                  pl.BlockSpec((SUBLANE, w), lambda i: (jnp.maximum(i * per - 1, 0), 0)),
                  pl.BlockSpec((SUBLANE, w), lambda i: (jnp.minimum((i + 1) * per, nb - 1), 0)),
                  pl.BlockSpec((TM, LANE), lambda i: (i, 0)),
                  _full(conv_w.shape), _full(conv_b.shape), _full(dt_bias.shape), _full(a_lane.shape),
                  _full(esel.shape), _full(tri.shape)],
        out_specs=[pl.BlockSpec((TM, BR_W), lambda i: (i, 0)), pl.BlockSpec((TM, 2 * SSD_GN), lambda i: (i, 0)),
                   pl.BlockSpec((2, TM, LANE), lambda i: (0, i, 0)),
                   pl.BlockSpec((2, cpt, SUBLANE, SSD_T), lambda i: (0, i, 0, 0)),
                   pl.BlockSpec((2, TM, BR_W), lambda i: (0, i, 0))],
        compiler_params=_cp("parallel"), name="ssd_prep",
    )(p_ssd, p_ssd, p_ssd, dt_raw, conv_w, conv_b, dt_bias, a_lane, esel, tri)


def _ssd_scan_kernel(xt_ref, bc_ref, ac_ref, act_ref, tri_ref, h0_ref, y_ref, hfin_ref, st_ref, *, n_chunk, bsz):
    d = pl.program_id(0)
    j = pl.program_id(1)
    T = SSD_T

    @pl.when(j == 0)
    def _():
        st_ref[...] = h0_ref[...]

    tri = tri_ref[...]
    mask = tri > 0.5
    rep = SSD_HEADS // SSD_GROUPS

    def chunk(ci, carry):
        c = ci + d * (n_chunk - 1 - 2 * ci)
        r0 = pl.multiple_of(c * T, T)
        for b in range(bsz):
            a_cum = ac_ref[b, pl.ds(r0, T), :]
            a_cum_t = act_ref[b, c]
            total = jnp.where(d == 0, a_cum[T - 1:T, :], a_cum[0:1, :])
            xt = xt_ref[b, pl.ds(r0, T), :]
            bc = bc_ref[b, pl.ds(r0, T), :]
            bm_t = bc[:, :SSD_GN].astype(F32).T
            cm = bc[:, SSD_GN:]
            for g in range(SSD_GROUPS):
                gs = slice(g * SSD_STATE, (g + 1) * SSD_STATE)
                b_t = bm_t[gs, :]
                c_g = cm[:, gs]
                scores = jnp.dot(c_g, b_t.astype(BF16), preferred_element_type=F32)
                for hh in range(rep):
                    h = g * rep + hh
                    hs = slice(h * SSD_HEAD_DIM, (h + 1) * SSD_HEAD_DIM)
                    col = a_cum[:, h:h + 1]
                    rowv = a_cum_t[h:h + 1, :]
                    tot = total[:, h:h + 1]
                    decay = jnp.exp(jnp.where(mask, col - rowv, -1e30))
                    p = (scores * decay).astype(BF16)
                    x_h = xt[:, hs]
                    s_old = st_ref[b, h]
                    y_h = jnp.dot(p, x_h, preferred_element_type=F32)
                    y_h += jnp.dot(c_g, s_old.astype(BF16), preferred_element_type=F32) * jnp.exp(col)
                    bw = (b_t * jnp.exp(tot - rowv)).astype(BF16)
                    st_ref[b, h] = s_old * jnp.exp(tot) + jnp.dot(bw, x_h, preferred_element_type=F32)
                    y_ref[b, pl.ds(r0, T), hs] = y_h.astype(y_ref.dtype)
        return carry

    lax.fori_loop(0, n_chunk, chunk, 0, unroll=True)
    hfin_ref[...] = st_ref[...]


def ssd_scan(xt, bc, a_cum, a_cum_t, tri, bsz, h0):
    r = bc.shape[0]
    seq_len = r // bsz
    ts = min(seq_len, TM)
    nt = seq_len // ts

    def tile(d, j):
        return j + d * (nt - 1 - 2 * j)

    st = pl.BlockSpec((None, bsz, SSD_HEADS, SSD_STATE, SSD_HEAD_DIM), lambda d, j: (d, 0, 0, 0, 0))
    y, hfin = pl.pallas_call(
        functools.partial(_ssd_scan_kernel, n_chunk=ts // SSD_T, bsz=bsz),
        out_shape=[jax.ShapeDtypeStruct((2, bsz, seq_len, BR_W), BF16),
                   jax.ShapeDtypeStruct((2, bsz, SSD_HEADS, SSD_STATE, SSD_HEAD_DIM), F32)],
        grid=(2, nt),
        in_specs=[pl.BlockSpec((None, bsz, ts, BR_W), lambda d, j: (d, 0, tile(d, j), 0)),
                  pl.BlockSpec((bsz, ts, 2 * SSD_GN), lambda d, j: (0, tile(d, j), 0)),
                  pl.BlockSpec((None, bsz, ts, LANE), lambda d, j: (d, 0, tile(d, j), 0)),
                  pl.BlockSpec((None, bsz, ts // SSD_T, SUBLANE, SSD_T), lambda d, j: (d, 0, tile(d, j), 0, 0)),
                  pl.BlockSpec((None, SSD_T, SSD_T), lambda d, j: (d, 0, 0)), st],
        out_specs=[pl.BlockSpec((None, bsz, ts, BR_W), lambda d, j: (d, 0, tile(d, j), 0)), st],
        scratch_shapes=[pltpu.VMEM((bsz, SSD_HEADS, SSD_STATE, SSD_HEAD_DIM), F32)],
        compiler_params=_cp("arbitrary", "arbitrary"), name="ssd_scan",
    )(xt.reshape(2, bsz, seq_len, BR_W), bc.reshape(bsz, seq_len, 2 * SSD_GN),
      a_cum.reshape(2, bsz, seq_len, LANE), a_cum_t.reshape(2, bsz, seq_len // SSD_T, SUBLANE, SSD_T), tri, h0)
    return y.reshape(2, r, BR_W), hfin


def _rope_table_kernel(frow_ref, fcol_ref, cos_ref, sin_ref):
    tr = cos_ref.shape[0]
    pos = (lax.broadcasted_iota(jnp.int32, (tr, 1), 0) + pl.program_id(0) * tr).astype(F32)
    pos_row = jnp.floor((pos + 0.5) * (1.0 / GRID_W))
    pos_col = pos - pos_row * GRID_W
    ang = pos_row * frow_ref[...] + pos_col * fcol_ref[...]
    lane = lax.broadcasted_iota(jnp.int32, ang.shape, 1)
    quarter = MLA_ROPE // 2
    first = (lane >= MLA_NOPE) & (lane < MLA_NOPE + quarter)
    second = (lane >= MLA_NOPE + quarter) & (lane < MLA_QK)
    c, s = jnp.cos(ang), jnp.sin(ang)
    cos_ref[...] = jnp.where(first | second, c, jnp.where(lane < MLA_NOPE, 1.0, 0.0))
    sin_ref[...] = jnp.where(first, -s, jnp.where(second, s, 0.0))


def _rope_partner(v):
    quarter = MLA_ROPE // 2
    src = np.arange(LANE)
    src[MLA_NOPE:MLA_NOPE + quarter] += quarter
    src[MLA_NOPE + quarter:MLA_QK] -= quarter
    valid = (np.arange(LANE) >= MLA_NOPE) & (np.arange(LANE) < MLA_QK)
    return jnp.where(valid, v[..., src], 0)


def rope_tables(seq_len):
    pairs = MLA_ROPE // 4
    inv_freq = ROPE_BASE ** (-jnp.arange(pairs, dtype=F32) / pairs)
    zeros = lambda n: jnp.zeros((n,), F32)
    half = jnp.concatenate([inv_freq, zeros(pairs)])
    f_row = jnp.concatenate([zeros(MLA_NOPE), half, half, zeros(LANE - MLA_QK)]).reshape(1, LANE)
    half = jnp.concatenate([zeros(pairs), inv_freq])
    f_col = jnp.concatenate([zeros(MLA_NOPE), half, half, zeros(LANE - MLA_QK)]).reshape(1, LANE)
    tr = min(seq_len, 1024)
    spec = pl.BlockSpec((tr, LANE), lambda i: (i, 0))
    return pl.pallas_call(
        _rope_table_kernel, out_shape=[jax.ShapeDtypeStruct((seq_len, LANE), F32)] * 2,
        grid=(seq_len // tr,), in_specs=[_full((1, LANE))] * 2, out_specs=[spec] * 2,
        compiler_params=_cp("parallel"), name="rope_tables",
    )(f_row, f_col)


def _mla_prep_kernel(*refs, rope):
    if rope:
        (p_ref, gq_ref, gkv_ref, wq_ref, wkv_ref, nq_ref, nk_ref, wqs_ref, nqs_ref, nks_ref, cos_ref, sin_ref,
         q_ref, k_ref, v_ref) = refs
    else:
        p_ref, gq_ref, gkv_ref, wq_ref, wkv_ref, nq_ref, nk_ref, q_ref, k_ref, v_ref = refs
    p = p_ref[...].astype(F32)
    cq = (_rms(p[:, :MLA_Q_LORA], MLA_Q_LORA) * gq_ref[...]).astype(BF16)
    ckv = (_rms(p[:, MLA_Q_LORA:MLA_Q_LORA + MLA_KV_LORA], MLA_KV_LORA) * gkv_ref[...]).astype(BF16)
    kr = p[:, MLA_Q_LORA + MLA_KV_LORA:MLA_Q_LORA + MLA_KV_LORA + LANE]
    q_all = jnp.dot(cq, wq_ref[...], preferred_element_type=F32)
    kv_all = jnp.dot(ckv, wkv_ref[...], preferred_element_type=F32)
    lane = lax.broadcasted_iota(jnp.int32, (1, LANE), 1)
    one_col = jnp.where(lane == MLA_V, 1.0, 0.0)
    scale = MLA_QK ** -0.5 * LOG2E
    if rope:
        qs_all = jnp.dot(cq, wqs_ref[...], preferred_element_type=F32)
        q_cos, q_sin = nq_ref[...] * cos_ref[...], nqs_ref[...] * sin_ref[...]
        k_cos = nk_ref[...] * cos_ref[...]
        k_part = p[:, MLA_Q_LORA + MLA_KV_LORA + LANE:] * (nks_ref[...] * sin_ref[...])

    def inv_rms(t):
        return lax.rsqrt(jnp.sum(t * t, axis=-1, keepdims=True) * (1.0 / MLA_QK) + 1e-6)

    for h in range(MLA_HEADS):
        q = q_all[:, h * LANE:(h + 1) * LANE]
        k = kv_all[:, 2 * h * LANE:(2 * h + 1) * LANE] + kr
        if rope:
            q_out = (q * q_cos + qs_all[:, h * LANE:(h + 1) * LANE] * q_sin) * (inv_rms(q) * scale)
            k_out = (k * k_cos + k_part) * inv_rms(k)
        else:
            q_out = q * nq_ref[...] * (inv_rms(q) * scale)
            k_out = k * nk_ref[...] * inv_rms(k)
        q_ref[h] = q_out.astype(q_ref.dtype)
        k_ref[h] = k_out.astype(k_ref.dtype)
        v_ref[h] = (kv_all[:, (2 * h + 1) * LANE:(2 * h + 2) * LANE] + one_col).astype(v_ref.dtype)


def mla_prep(p_mla, bsz, gq, gkv, wq, wkv, nq, nk, rope_args):
    r, w = p_mla.shape
    seq_len = r // bsz
    tr = min(seq_len, TM)
    nt = seq_len // tr
    rope = rope_args is not None
    in_specs = [pl.BlockSpec((tr, w), lambda b, i: (b * nt + i, 0)), _full(gq.shape), _full(gkv.shape),
                _wspec(wq), _wspec(wkv), _full(nq.shape), _full(nk.shape)]
    args = [p_mla, gq, gkv, _warg(wq), _warg(wkv), nq, nk]
    if rope:
        in_specs += [_wspec(a) for a in rope_args[:3]]
        in_specs += [pl.BlockSpec((tr, LANE), lambda b, i: (i, 0))] * 2
        args += [_warg(a) for a in rope_args]
    head = pl.BlockSpec((None, MLA_HEADS, tr, LANE), lambda b, i: (b, 0, i, 0))
    return pl.pallas_call(
        functools.partial(_mla_prep_kernel, rope=rope),
        out_shape=[jax.ShapeDtypeStruct((bsz, MLA_HEADS, seq_len, LANE), BF16)] * 3,
        grid=(bsz, nt), in_specs=in_specs, out_specs=[head] * 3,
        compiler_params=_cp("parallel", "parallel"), name="mla_prep",
    )(*args)


def _attn_kernel(*refs, n_kv):
    bound_ref, q_ref = refs[0], refs[1]
    kv_refs = refs[2:2 + 2 * n_kv]
    o_ref = refs[2 + 2 * n_kv]
    m_ref = refs[3 + 2 * n_kv]
    scores = [[lax.dot_general(q_ref[hh], kv_refs[2 * i][hh], (((1,), (1,)), ((), ())),
                               preferred_element_type=F32) for i in range(n_kv)] for hh in range(2)]

    def row_max(first_set):
        for hh in range(2):
            m = scores[hh][n_kv - 1].max(axis=-1, keepdims=True)
            for si in scores[hh][first_set:n_kv - 1]:
                m = jnp.maximum(m, si.max(axis=-1, keepdims=True))
            m_ref[hh] = m

    if n_kv > 1:
        fast = bound_ref[0] <= ATTN_FAST_BOUND

        @pl.when(fast)
        def _():
            row_max(n_kv - 1)

        @pl.when(jnp.logical_not(fast))
        def _():
            row_max(0)
    else:
        row_max(0)

    for hh in range(2):
        m = m_ref[hh]
        acc = None
        for i, si in enumerate(scores[hh]):
            pv = jnp.dot(jnp.exp2((si - m).astype(BF16)), kv_refs[2 * i + 1][hh], preferred_element_type=F32)
            acc = pv if acc is None else acc + pv
        o = acc[:, :MLA_V] / acc[:, MLA_V:MLA_V + 1]
        o_ref[:, hh * MLA_V:(hh + 1) * MLA_V] = o.astype(o_ref.dtype)


def attention(q, kvs, score_bound):
    bsz, nh, lq, _ = q.shape
    tq = min(lq, TQ)
    nq = lq // tq
    in_specs = [pl.BlockSpec(memory_space=pltpu.SMEM),
                pl.BlockSpec((None, 2, tq, LANE), lambda b, hp, i: (b, hp, i, 0))]
    args = [score_bound, q]
    for k, v in kvs:
        spec = pl.BlockSpec((None, 2, k.shape[2], LANE), lambda b, hp, i: (b, hp, 0, 0))
        in_specs += [spec, spec]
        args += [k, v]
    return pl.pallas_call(
        functools.partial(_attn_kernel, n_kv=len(kvs)),
        out_shape=jax.ShapeDtypeStruct((bsz * lq, nh * MLA_V), BF16),
        grid=(bsz, nh // 2, nq), in_specs=in_specs,
        out_specs=pl.BlockSpec((tq, 2 * MLA_V), lambda b, hp, i: (b * nq + i, hp)),
        scratch_shapes=[pltpu.VMEM((2, tq, 1), F32)],
        compiler_params=_cp("parallel", "parallel", "arbitrary"), name="attention",
    )(*args)


def _merge_kernel(x_ref, h_ref, u_ref, s5f_ref, s5b_ref, s5d_ref, wglu_ref, b_ref,
                  xs_ref, ssdf_ref, ssdb_ref, z_ref, ssdd_ref, ssdg_ref, d_ref,
                  g1_ref, wg_ref, wb_ref, wo_ref,
                  n2_ref, sh2_ref, sc2_ref, g2_ref, wi_ref, wfo_ref, o_ref, *, n_split):
    y = s5d_ref[...] * u_ref[...].astype(F32) + s5f_ref[...].astype(F32) + s5b_ref[...].astype(F32)
    g = jax.nn.gelu(y)
    a = (g * jax.nn.sigmoid(jnp.dot(g.astype(BF16), wglu_ref[...], preferred_element_type=F32))).astype(BF16)
    y = ssdd_ref[...] * xs_ref[...] + ssdf_ref[...].astype(F32) + ssdb_ref[...].astype(F32)
    c = (_rms(y * _silu(z_ref[...].astype(F32)), BR_W) * ssdg_ref[...]).astype(BF16)
    h = h_ref[...]
    dm = x_ref.shape[-1]
    merged = None
    for i, br in enumerate((a, b_ref[...], c, d_ref[...])):
        gate = jax.nn.sigmoid(jnp.dot(h, wg_ref[:, i * dm:(i + 1) * dm], preferred_element_type=F32))
        term = gate * jnp.dot(br, wb_ref[i], preferred_element_type=F32)
        merged = term if merged is None else merged + term
    mix = jnp.dot(merged.astype(BF16), wo_ref[...], preferred_element_type=F32)
    x = x_ref[...] + g1_ref[...] * mix

    y = _rms(x, dm) * n2_ref[...]
    hb = (y * (1.0 + sc2_ref[...]) + sh2_ref[...]).astype(BF16)
    hid = wfo_ref.shape[0]
    step = hid // n_split
    acc = None
    for c in range(n_split):
        gate = jnp.dot(hb, wi_ref[:, c * step:(c + 1) * step], preferred_element_type=F32)
        up = jnp.dot(hb, wi_ref[:, hid + c * step:hid + (c + 1) * step], preferred_element_type=F32)
        part = jnp.dot((_silu(gate) * up).astype(BF16), wfo_ref[c * step:(c + 1) * step, :],
                       preferred_element_type=F32)
        acc = part if acc is None else acc + part
    o_ref[...] = x + g2_ref[...] * acc


def merge_ffn(x, h, s5_in, b, ssd_in, d, mod4, l, midx, wg, wb, wo, norm2, wi, wfo):
    r, dm = x.shape
    n_split = 11 if wfo.shape[0] % (11 * LANE) == 0 else 1
    row = pl.BlockSpec((TM, dm), lambda i: (i, 0))
    br = pl.BlockSpec((TM, BR_W), lambda i: (i, 0))
    fwd = pl.BlockSpec((None, TM, BR_W), lambda i: (0, i, 0))
    bwd = pl.BlockSpec((None, TM, BR_W), lambda i: (1, i, 0))
    vec = _full((1, BR_W))
    u, y_s5, s5_d, w_glu = s5_in
    xs, y_ssd, p_ssd, ssd_d, ssd_g = ssd_in
    return pl.pallas_call(
        functools.partial(_merge_kernel, n_split=n_split),
        out_shape=jax.ShapeDtypeStruct((r, dm), F32), grid=(r // TM,),
        in_specs=[row, row, br, fwd, bwd, vec, _wspec(w_glu), br,
                  br, fwd, bwd, br, vec, vec, br,
                  _mod_spec(l, midx, 2, dm), _wspec(wg), _wspec(wb), _wspec(wo),
                  _full((1, dm)), _mod_spec(l, midx, 3, dm), _mod_spec(l, midx, 4, dm),
                  _mod_spec(l, midx, 5, dm), _wspec(wi), _wspec(wfo)],
        out_specs=row, compiler_params=_cp("parallel"), name="merge_ffn",
    )(x, h, u, y_s5, y_s5, s5_d, _warg(w_glu), b, xs, y_ssd, y_ssd, p_ssd, ssd_d, ssd_g, d, mod4,
      _warg(wg), _warg(wb), _warg(wo), norm2, mod4, mod4, mod4, _warg(wi), _warg(wfo))


def _pad_cols(w, n):
    return jnp.pad(w, ((0, 0),) * (w.ndim - 1) + ((0, n - w.shape[-1]),))


def kernel(x, c, ctx, c_ctx, w_ada, b_ada, norm1_g, norm2_g, w_in, s5_a_re, s5_a_im, s5_b_re, s5_b_im, s5_c_re, s5_c_im, s5_log_dt, s5_d, s5_w_glu, sgu_ln_g, sgu_ln_b, sgu_w_s, sgu_b_s, ssd_conv_w, ssd_conv_b, ssd_a_log, ssd_dt_bias, ssd_d, ssd_norm_g, mla_q_a_norm, mla_w_uq, mla_kv_a_norm, mla_w_ukv, mla_q_norm, mla_k_norm, w_branch, w_out, w_ffn_in, w_ffn_out):
    bsz, seq, dm = x.shape
    lc = ctx.shape[1]
    depth = w_ada.shape[0]
    assert seq % TM == 0 and (bsz * lc) % TM == 0 and seq % (S5_NSEG * SUBLANE) == 0
    assert lc % SSD_T == 0 and lc % (S5_NSEG * SUBLANE) == 0 and bsz + 1 <= 8

    cc8 = jnp.zeros((8, dm), F32).at[:bsz].set(c.astype(F32)).at[bsz].set(c_ctx.astype(F32))
    mod4 = ada_table(cc8, w_ada.astype(F32), b_ada.astype(F32)).reshape(depth, 8, 1, 6 * dm)
    lat_tiles = seq // TM
    midx_lat = lambda i: i // lat_tiles
    midx_ctx = lambda i: bsz

    off = np.cumsum([0, BR_W, 2 * BR_W, BR_W + SSD_CONV_CH + 2 * SSD_HEADS,
                     MLA_Q_LORA + MLA_KV_LORA + MLA_ROPE, N_BRANCH * dm])
    w_s5 = w_in[:, :, off[0]:off[1]].astype(BF16)
    w_sgu = w_in[:, :, off[1]:off[2]].astype(BF16)
    w_ssd = w_in[:, :, off[2]:off[2] + BR_W + SSD_CONV_CH].astype(BF16)
    w_dt = _pad_cols(w_in[:, :, off[2] + BR_W + SSD_CONV_CH:off[3]], LANE).astype(BF16)
    w_mla_main = w_in[:, :, off[3]:off[3] + MLA_Q_LORA + MLA_KV_LORA]
    w_kr = w_in[:, :, off[3] + MLA_Q_LORA + MLA_KV_LORA:off[4]]
    zeros = lambda n: jnp.zeros((depth, dm, n), w_in.dtype)
    kr_block = jnp.concatenate([zeros(MLA_NOPE), w_kr, zeros(LANE - MLA_QK)], axis=-1)
    w_mla = jnp.concatenate([w_mla_main, kr_block, _rope_partner(kr_block)], axis=-1).astype(BF16)
    w_gate = w_in[:, :, off[4]:off[5]].astype(BF16)

    lam_re, lam_im, bb_re, bb_im = s5_discretise(s5_a_re, s5_a_im, s5_b_re, s5_b_im, s5_log_dt)
    lam, bblk, cblk = s5_pack(lam_re, lam_im, bb_re, bb_im, s5_c_re, s5_c_im)
    s5_wg = s5_w_glu.astype(BF16)

    sgu_w = sgu_w_s.reshape(depth, SGU_HEADS // 2, 2, SGU_CHUNK, SGU_CHUNK)
    sgu_w = jnp.concatenate([sgu_w[:, :, 0], sgu_w[:, :, 1]], axis=-1).astype(BF16)
    sgu_b = jnp.repeat(jnp.swapaxes(sgu_b_s, 1, 2), BR_W // SGU_HEADS, axis=-1).astype(F32)

    a_neg = -jnp.exp(ssd_a_log.astype(F32))
    a_lane = _pad_cols(a_neg.reshape(depth, 1, 2 * SSD_HEADS), LANE)
    head_of_lane = jnp.arange(LANE)[None, :, None] - SSD_HEADS * jnp.arange(2)[:, None, None]
    col_head = jnp.arange(BR_W) // SSD_HEAD_DIM
    esel = ((head_of_lane == col_head[None, None, :]) & (head_of_lane >= 0)).astype(BF16)
    ti = jnp.arange(SSD_T)
    tri = jnp.stack([ti[None, :] <= ti[:, None], ti[None, :] >= ti[:, None]]).astype(F32)
    dt_bias = _pad_cols(ssd_dt_bias.astype(F32).reshape(depth, 1, 2 * SSD_HEADS), LANE)
    ssd_dskip = jnp.repeat(ssd_d.astype(F32), SSD_HEAD_DIM, axis=-1)[:, None, :]

    wq = mla_w_uq.reshape(depth, MLA_Q_LORA, MLA_HEADS, MLA_QK)
    wq = _pad_cols(wq, LANE)
    wq_sw = _rope_partner(wq).reshape(depth, MLA_Q_LORA, MLA_HEADS * LANE).astype(BF16)
    wq = wq.reshape(depth, MLA_Q_LORA, MLA_HEADS * LANE).astype(BF16)
    wkv = mla_w_ukv.reshape(depth, MLA_KV_LORA, MLA_HEADS, 2, MLA_NOPE)
    wkv = _pad_cols(wkv, LANE).reshape(depth, MLA_KV_LORA, MLA_HEADS * 2 * LANE).astype(BF16)
    nq = _pad_cols(mla_q_norm.astype(F32), LANE)[:, None, :]
    nk = _pad_cols(mla_k_norm.astype(F32), LANE)[:, None, :]
    nq_sw, nk_sw = _rope_partner(nq), _rope_partner(nk)
    rope_cos, rope_sin = rope_tables(seq)
    score_bound = (MLA_QK ** 0.5 * LOG2E * jnp.max(jnp.abs(mla_q_norm.astype(F32)), axis=-1)
                   * jnp.max(jnp.abs(mla_k_norm.astype(F32)), axis=-1))

    wb = w_branch.astype(BF16)
    wo = w_out.astype(BF16)
    wfi = w_ffn_in.astype(BF16)
    wfo = w_ffn_out.astype(BF16)

    row = lambda v, l: v[l].astype(F32).reshape(1, -1)
    x_lat = x.astype(F32).reshape(bsz * seq, dm)
    x_ctx = ctx.astype(F32).reshape(bsz * lc, dm)
    s5_zero = jnp.zeros((bsz, 2, S5_JB, 1, 2 * S5_BW), F32)
    ssd_zero = jnp.zeros((2, bsz, SSD_HEADS, SSD_STATE, SSD_HEAD_DIM), F32)

    for l in range(depth):
        need_ctx = l < depth - 1
        lay = lambda w: _Layer(w, l)
        ws = tuple(lay(w) for w in (w_s5, w_sgu, w_ssd, w_dt, w_mla))
        w_merge = (lay(w_gate), lay(wb), lay(wo))
        w_ffn = (lay(wfi), lay(wfo))
        dts = (BF16, BF16, BF16, F32, BF16)
        g1 = row(norm1_g, l)
        h_l, u_l, z_l, p_l, dtr_l, m_l = in_proj(x_lat, g1, mod4, l, midx_lat, ws, dts)
        h_c, u_c, z_c, p_c, dtr_c, m_c = in_proj(x_ctx, g1, mod4, l, midx_ctx, ws, dts)

        y_c, s5_h = s5_scan(u_c, bblk, lam, cblk, l, bsz, s5_zero)
        y_l, _ = s5_scan(u_l, bblk, lam, cblk, l, bsz, s5_h)
        s5_tail = (row(s5_d, l), lay(s5_wg))
        b_l = sgu(z_l, row(sgu_ln_g, l), row(sgu_ln_b, l), lay(sgu_w), lay(sgu_b))
        conv_w, conv_b = ssd_conv_w[l].astype(F32), row(ssd_conv_b, l)
        ssd_w = (conv_w, conv_b, dt_bias[l], a_lane[l], esel, tri)
        xs_c, bc_c, ac_c, act_c, xt_c = ssd_prep(p_c, dtr_c, lc, *ssd_w)
        xs_l, bc_l, ac_l, act_l, xt_l = ssd_prep(p_l, dtr_l, seq, *ssd_w)
        yc_c, ssd_h = ssd_scan(xt_c, bc_c, ac_c, act_c, tri, bsz, ssd_zero)
        yc_l, _ = ssd_scan(xt_l, bc_l, ac_l, act_l, tri, bsz, ssd_h)
        ssd_tail = (ssd_dskip[l], row(ssd_norm_g, l))
        mla_w = (row(mla_q_a_norm, l), row(mla_kv_a_norm, l), lay(wq), lay(wkv), nq[l], nk[l])
        q_c, k_c, v_c = mla_prep(m_c, bsz, *mla_w, None)
        q_l, k_l, v_l = mla_prep(m_l, bsz, *mla_w, (lay(wq_sw), nq_sw[l], nk_sw[l], rope_cos, rope_sin))
        d_l = attention(q_l, [(k_l, v_l), (k_c, v_c)], score_bound[l:l + 1])

        x_lat = merge_ffn(x_lat, h_l, (u_l, y_l) + s5_tail, b_l, (xs_l, yc_l, p_l) + ssd_tail, d_l,
                          mod4, l, midx_lat, *w_merge, row(norm2_g, l), *w_ffn)
        if need_ctx:
            b_c = sgu(z_c, row(sgu_ln_g, l), row(sgu_ln_b, l), lay(sgu_w), lay(sgu_b))
            d_c = attention(q_c, [(k_c, v_c)], score_bound[l:l + 1])
            x_ctx = merge_ffn(x_ctx, h_c, (u_c, y_c) + s5_tail, b_c, (xs_c, yc_c, p_c) + ssd_tail, d_c,
                              mod4, l, midx_ctx, *w_merge, row(norm2_g, l), *w_ffn)
    return x_lat.reshape(bsz, seq, dm).astype(x.dtype)
```

```python
import functools

import jax
import jax.numpy as jnp
import numpy as np
from jax import lax
from jax.experimental import pallas as pl
from jax.experimental.pallas import tpu as pltpu

F32 = jnp.float32
BF16 = jnp.bfloat16
HIGHEST = lax.Precision.HIGHEST

LANE = 128
SUBLANE = 8
VMEM_LIMIT = 56 * 1024 * 1024

GRID_W = 64
BR_W = 384
S5_GROUP = 16
S5_GROUPS = BR_W // S5_GROUP
S5_STATE = 64
S5_NSEG = SUBLANE
S5_JB = BR_W // LANE
S5_BW = (LANE // S5_GROUP) * S5_STATE
SGU_CHUNK = 128
SGU_HEADS = 6
SSD_HEADS = 6
SSD_HEAD_DIM = 64
SSD_GROUPS = 2
SSD_STATE = 64
SSD_GN = SSD_GROUPS * SSD_STATE
SSD_CONV_CH = BR_W + 2 * SSD_GN
SSD_T = 128
MLA_HEADS = 6
MLA_NOPE = 64
MLA_ROPE = 32
MLA_V = 64
MLA_QK = MLA_NOPE + MLA_ROPE
MLA_Q_LORA = 384
MLA_KV_LORA = 256
ROPE_BASE = 10000.0
LOG2E = 1.4426950408889634
N_BRANCH = 4

TM = 512
TQ = 256
ATTN_FAST_BOUND = 30.0


def _cp(*sem):
    return pltpu.CompilerParams(dimension_semantics=sem, vmem_limit_bytes=VMEM_LIMIT)


def _full(shape):
    n = len(shape)
    return pl.BlockSpec(shape, lambda *_: (0,) * n)


class _Layer:
    def __init__(self, arr, l):
        self.arr, self.l = arr, l

    @property
    def shape(self):
        return self.arr.shape[1:]


def _wspec(w):
    if not isinstance(w, _Layer):
        return _full(w.shape)
    n, l = len(w.shape), w.l
    return pl.BlockSpec((None,) + tuple(w.shape), lambda *_: (l,) + (0,) * n, pipeline_mode=pl.Buffered(1))


def _warg(w):
    return w.arr if isinstance(w, _Layer) else w


def _silu(x):
    return x * jax.nn.sigmoid(x)


def _rms(x, n, eps=1e-6):
    return x * lax.rsqrt(jnp.sum(x * x, axis=-1, keepdims=True) * (1.0 / n) + eps)


def _ada_kernel(cc_ref, w_ref, b_ref, o_ref):
    s = _silu(cc_ref[...])
    o_ref[...] = jnp.dot(s, w_ref[...], preferred_element_type=F32, precision=HIGHEST) + b_ref[...]


def ada_table(cc8, w_ada, b_ada):
    depth, d, n = w_ada.shape
    tn = n // 4
    return pl.pallas_call(
        _ada_kernel,
        out_shape=jax.ShapeDtypeStruct((depth, 8, n), F32),
        grid=(depth, n // tn),
        in_specs=[pl.BlockSpec((8, d), lambda l, j: (0, 0)),
                  pl.BlockSpec((None, d, tn), lambda l, j: (l, 0, j)),
                  pl.BlockSpec((None, 1, tn), lambda l, j: (l, 0, j))],
        out_specs=pl.BlockSpec((None, 8, tn), lambda l, j: (l, 0, j)),
        compiler_params=_cp("arbitrary", "arbitrary"),
        name="ada_table",
    )(cc8, w_ada, b_ada.reshape(depth, 1, n))


def _mod_spec(l, midx, col, d):
    return pl.BlockSpec((None, None, 1, d), lambda i: (l, midx(i), 0, col))


def _in_kernel(x_ref, g_ref, sh_ref, sc_ref, w1, w2, w3, w4, w5, h_ref, o1, o2, o3, o4, o5):
    x = x_ref[...]
    y = _rms(x, x.shape[-1]) * g_ref[...]
    hb = (y * (1.0 + sc_ref[...]) + sh_ref[...]).astype(BF16)
    h_ref[...] = hb
    for w, o in ((w1, o1), (w2, o2), (w3, o3), (w4, o4), (w5, o5)):
        o[...] = jnp.dot(hb, w[...], preferred_element_type=F32).astype(o.dtype)


def in_proj(x, g, mod4, l, midx, ws, out_dtypes):
    r, d = x.shape
    in_specs = [pl.BlockSpec((TM, d), lambda i: (i, 0)), _full((1, d)),
                _mod_spec(l, midx, 0, d), _mod_spec(l, midx, 1, d)]
    in_specs += [_wspec(w) for w in ws]
    out_shape = [jax.ShapeDtypeStruct((r, d), BF16)]
    out_specs = [pl.BlockSpec((TM, d), lambda i: (i, 0))]
    for w, dt in zip(ws, out_dtypes):
        out_shape.append(jax.ShapeDtypeStruct((r, w.shape[1]), dt))
        out_specs.append(pl.BlockSpec((TM, w.shape[1]), lambda i: (i, 0)))
    return pl.pallas_call(
        _in_kernel, out_shape=out_shape, grid=(r // TM,), in_specs=in_specs, out_specs=out_specs,
        compiler_params=_cp("parallel"), name="in_proj",
    )(x, g, mod4, mod4, *[_warg(w) for w in ws])


def _s5_disc_kernel(are, aim, ldt, bre, bim, lam_re, lam_im, bbre, bbim):
    a_re, a_im = are[...], aim[...]
    dt = jnp.exp(ldt[...])
    mag = jnp.exp(a_re * dt)
    ang = a_im * dt
    ab_re = mag * jnp.cos(ang)
    ab_im = mag * jnp.sin(ang)
    den = a_re * a_re + a_im * a_im
    f_re = ((ab_re - 1.0) * a_re + ab_im * a_im) / den
    f_im = (ab_im * a_re - (ab_re - 1.0) * a_im) / den
    lam_re[...] = ab_re
    lam_im[...] = ab_im
    for c in range(S5_GROUP):
        bbre[c] = f_re * bre[c] - f_im * bim[c]
        bbim[c] = f_re * bim[c] + f_im * bre[c]


def s5_discretise(a_re, a_im, b_re, b_im, log_dt):
    shp = a_re.shape
    rows = int(np.prod(shp)) // LANE
    are = a_re.astype(F32).reshape(rows, LANE)
    aim = a_im.astype(F32).reshape(rows, LANE)
    ldt = jnp.broadcast_to(log_dt.astype(F32)[..., None], shp).reshape(rows, LANE)
    bre = jnp.moveaxis(b_re.astype(F32), -1, 0).reshape(S5_GROUP, rows, LANE)
    bim = jnp.moveaxis(b_im.astype(F32), -1, 0).reshape(S5_GROUP, rows, LANE)
    outs = pl.pallas_call(
        _s5_disc_kernel,
        out_shape=[jax.ShapeDtypeStruct((rows, LANE), F32)] * 2
        + [jax.ShapeDtypeStruct((S5_GROUP, rows, LANE), F32)] * 2,
        name="s5_discretise",
    )(are, aim, ldt, bre, bim)
    lam_re, lam_im = outs[0].reshape(shp), outs[1].reshape(shp)
    bb_re = jnp.moveaxis(outs[2].reshape((S5_GROUP,) + shp), 0, -1)
    bb_im = jnp.moveaxis(outs[3].reshape((S5_GROUP,) + shp), 0, -1)
    return lam_re, lam_im, bb_re, bb_im


def s5_pack(lam_re, lam_im, bb_re, bb_im, c_re, c_im):
    depth = lam_re.shape[0]
    gpb = LANE // S5_GROUP
    eye = jnp.eye(gpb, dtype=F32)

    def lam_blocks(v):
        return v.reshape(depth, 2, S5_JB, 1, S5_BW)

    lam = jnp.concatenate([lam_blocks(lam_re), lam_blocks(lam_im)], axis=-1)
    lam = jnp.broadcast_to(lam, (depth, 2, S5_JB, SUBLANE, 2 * S5_BW))

    def b_blocks(bb):
        v = bb.reshape(depth, 2, S5_JB, gpb, S5_STATE, S5_GROUP)
        return jnp.einsum('ldjgpc,gh->ldjgchp', v, eye).reshape(depth, 2, S5_JB, LANE, S5_BW)

    bblk = jnp.concatenate([b_blocks(bb_re), b_blocks(bb_im)], axis=-1).astype(BF16)

    def c_blocks(cc):
        v = cc.astype(F32).reshape(depth, 2, S5_JB, gpb, S5_GROUP, S5_STATE)
        return jnp.einsum('ldjgcp,gh->ldjhpgc', v, eye).reshape(depth, 2, S5_JB, S5_BW, LANE)

    cblk = jnp.concatenate([c_blocks(c_re), -c_blocks(c_im)], axis=-2).astype(BF16)
    return lam, bblk, cblk


def _cpow(re, im, n):
    out = None
    while n:
        if n & 1:
            out = (re, im) if out is None else (out[0] * re - out[1] * im, out[0] * im + out[1] * re)
        n >>= 1
        if n:
            re, im = re * re - im * im, 2.0 * re * im
    return out


def _s5_pass_kernel(*refs, tt, lseg, with_y):
    if with_y:
        u_ref, bblk, lam, cblk, sloc, h0, y_ref, hfin, uperm, hst, yperm = refs[:11]
    else:
        u_ref, bblk, lam, s_out, uperm, hst = refs[:6]
    bus = refs[-S5_JB:]
    d = pl.program_id(1)
    j = pl.program_id(2)

    @pl.when(j == 0)
    def _():
        if not with_y:
            hst[...] = jnp.zeros(hst.shape, F32)
        else:
            for jb in range(S5_JB):
                pr, pi = _cpow(lam[jb, 0:1, :S5_BW], lam[jb, 0:1, S5_BW:], lseg)

                def chain(order, jb=jb, pr=pr, pi=pi):
                    cr, ci = h0[jb, :, :S5_BW], h0[jb, :, S5_BW:]
                    for s in order:
                        hst[jb, s:s + 1, :S5_BW] = cr
                        hst[jb, s:s + 1, S5_BW:] = ci
                        sr, si = sloc[jb, s:s + 1, :S5_BW], sloc[jb, s:s + 1, S5_BW:]
                        cr, ci = pr * cr - pi * ci + sr, pr * ci + pi * cr + si
                    hfin[jb, :, :S5_BW] = cr
                    hfin[jb, :, S5_BW:] = ci

                @pl.when(d == 0)
                def _():
                    chain(range(S5_NSEG))

                @pl.when(d == 1)
                def _():
                    chain(range(S5_NSEG - 1, -1, -1))

    for s in range(S5_NSEG):
        us = u_ref[s].astype(F32)
        for k in range(S5_JB):
            uperm[k, pl.ds(s, tt, stride=S5_NSEG), :] = us[:, k * LANE:(k + 1) * LANE]

    for jb in range(S5_JB):
        bus[jb][...] = jnp.dot(uperm[jb].astype(BF16), bblk[jb], preferred_element_type=F32)

    for jb in range(S5_JB):
        bu = bus[jb]
        lr, li = lam[jb, :, :S5_BW], lam[jb, :, S5_BW:]

        def step(i, carry, lr=lr, li=li, bu=bu):
            hr, hi = carry
            t = i + d * (tt - 1 - 2 * i)
            r0 = pl.multiple_of(t * S5_NSEG, S5_NSEG)
            nr = lr * hr - li * hi + bu[pl.ds(r0, S5_NSEG), :S5_BW]
            ni = lr * hi + li * hr + bu[pl.ds(r0, S5_NSEG), S5_BW:]
            if with_y:
                bu[pl.ds(r0, S5_NSEG), :S5_BW] = nr
                bu[pl.ds(r0, S5_NSEG), S5_BW:] = ni
            return nr, ni

        hr, hi = lax.fori_loop(0, tt, step, (hst[jb, :, :S5_BW], hst[jb, :, S5_BW:]), unroll=True)
        hst[jb, :, :S5_BW] = hr
        hst[jb, :, S5_BW:] = hi

    if with_y:
        for jb in range(S5_JB):
            yperm[jb] = jnp.dot(bus[jb][...].astype(BF16), cblk[jb], preferred_element_type=F32)
        for s in range(S5_NSEG):
            for k in range(S5_JB):
                y_ref[s, :, k * LANE:(k + 1) * LANE] = (
                    yperm[k, pl.ds(s, tt, stride=S5_NSEG), :].astype(y_ref.dtype))
    else:
        s_out[...] = hst[...]


def s5_scan(u, bblk, lam, cblk, l, bsz, h0):
    seq_len = u.shape[0] // bsz
    lseg = seq_len // S5_NSEG
    tt = min(lseg, 128)
    nt = lseg // tt
    u5 = u.reshape(bsz, S5_NSEG, lseg, BR_W)

    def tile(d, j):
        return j + d * (nt - 1 - 2 * j)

    u_spec = pl.BlockSpec((None, S5_NSEG, tt, BR_W), lambda b, d, j: (b, 0, tile(d, j), 0))
    y_spec = pl.BlockSpec((None, None, S5_NSEG, tt, BR_W), lambda b, d, j: (d, b, 0, tile(d, j), 0))
    w_b = pl.BlockSpec((None, None, S5_JB, LANE, 2 * S5_BW), lambda b, d, j: (l, d, 0, 0, 0))
    w_lam = pl.BlockSpec((None, None, S5_JB, SUBLANE, 2 * S5_BW), lambda b, d, j: (l, d, 0, 0, 0))
    w_c = pl.BlockSpec((None, None, S5_JB, 2 * S5_BW, LANE), lambda b, d, j: (l, d, 0, 0, 0))
    st8 = pl.BlockSpec((None, None, S5_JB, SUBLANE, 2 * S5_BW), lambda b, d, j: (b, d, 0, 0, 0))
    st1 = pl.BlockSpec((None, None, S5_JB, 1, 2 * S5_BW), lambda b, d, j: (b, d, 0, 0, 0))
    n_rows = S5_NSEG * tt
    scratch = [pltpu.VMEM((S5_JB, n_rows, LANE), F32), pltpu.VMEM((S5_JB, SUBLANE, 2 * S5_BW), F32)]
    bus = [pltpu.VMEM((n_rows, 2 * S5_BW), F32)] * S5_JB
    grid = (bsz, 2, nt)
    cp = _cp("arbitrary", "arbitrary", "arbitrary")

    sloc = pl.pallas_call(
        functools.partial(_s5_pass_kernel, tt=tt, lseg=lseg, with_y=False),
        out_shape=jax.ShapeDtypeStruct((bsz, 2, S5_JB, SUBLANE, 2 * S5_BW), F32),
        grid=grid, in_specs=[u_spec, w_b, w_lam], out_specs=st8,
        scratch_shapes=scratch + bus, compiler_params=cp, name="s5_local",
    )(u5, bblk, lam)

    y, hfin = pl.pallas_call(
        functools.partial(_s5_pass_kernel, tt=tt, lseg=lseg, with_y=True),
        out_shape=[jax.ShapeDtypeStruct((2, bsz, S5_NSEG, lseg, BR_W), BF16),
                   jax.ShapeDtypeStruct((bsz, 2, S5_JB, 1, 2 * S5_BW), F32)],
        grid=grid, in_specs=[u_spec, w_b, w_lam, w_c, st8, st1], out_specs=[y_spec, st1],
        scratch_shapes=scratch + [pltpu.VMEM((S5_JB, n_rows, LANE), F32)] + bus,
        compiler_params=cp, name="s5_emit",
    )(u5, bblk, lam, cblk, sloc, h0)
    return y.reshape(2, bsz * seq_len, BR_W), hfin


def _sgu_kernel(z_ref, g_ref, b_ref, w_ref, bias_ref, o_ref):
    z = jax.nn.gelu(z_ref[...].astype(F32))
    u, v = z[:, :BR_W], z[:, BR_W:]
    mu = jnp.mean(v, axis=-1, keepdims=True)
    vc = v - mu
    vn = vc * lax.rsqrt(jnp.mean(vc * vc, axis=-1, keepdims=True) + 1e-5) * g_ref[...] + b_ref[...]
    n_chunk = z.shape[0] // SGU_CHUNK
    half = LANE // 2
    lane = lax.broadcasted_iota(jnp.int32, (SGU_CHUNK, LANE), 1)
    for k in range(BR_W // LANE):
        cols = []
        for c in range(n_chunk):
            blk = vn[c * SGU_CHUNK:(c + 1) * SGU_CHUNK, k * LANE:(k + 1) * LANE]
            lo = jnp.where(lane < half, blk, 0.0)
            cols.append(jnp.concatenate([lo, blk - lo], axis=0))
        rhs = jnp.concatenate(cols, axis=1).astype(BF16)
        mixed = jnp.dot(w_ref[k], rhs, preferred_element_type=F32)
        for c in range(n_chunk):
            rows = slice(c * SGU_CHUNK, (c + 1) * SGU_CHUNK)
            m = mixed[:, c * LANE:(c + 1) * LANE] + bias_ref[:, k * LANE:(k + 1) * LANE]
            o_ref[rows, k * LANE:(k + 1) * LANE] = (u[rows, k * LANE:(k + 1) * LANE] * m).astype(o_ref.dtype)


def sgu(z, ln_g, ln_b, w_pair, bias):
    r = z.shape[0]
    return pl.pallas_call(
        _sgu_kernel, out_shape=jax.ShapeDtypeStruct((r, BR_W), BF16), grid=(r // TM,),
        in_specs=[pl.BlockSpec((TM, 2 * BR_W), lambda i: (i, 0)), _full((1, BR_W)), _full((1, BR_W)),
                  _wspec(w_pair), _wspec(bias)],
        out_specs=pl.BlockSpec((TM, BR_W), lambda i: (i, 0)),
        compiler_params=_cp("parallel"), name="sgu",
    )(z, ln_g, ln_b, _warg(w_pair), _warg(bias))


def _ssd_prep_kernel(cur_ref, prev_ref, next_ref, dt_ref, w_ref, b_ref, dtb_ref, alane_ref, esel_ref, tri_ref,
                     xs_ref, bc_ref, ac_ref, act_ref, xt_ref, *, seq_len):
    i = pl.program_id(0)
    x = cur_ref[:, BR_W:].astype(F32)
    tm = x.shape[0]
    row = lax.broadcasted_iota(jnp.int32, (tm, 1), 0)
    pos = lax.rem(row + i * tm, seq_len)
    prev_row = prev_ref[SUBLANE - 1:SUBLANE, BR_W:].astype(F32)
    next_row = next_ref[0:1, BR_W:].astype(F32)
    x_prev = jnp.where(row == 0, prev_row, pltpu.roll(x, 1, axis=0))
    x_prev = jnp.where(pos == 0, 0.0, x_prev)
    x_next = jnp.where(row == tm - 1, next_row, pltpu.roll(x, tm - 1, axis=0))
    x_next = jnp.where(pos == seq_len - 1, 0.0, x_next)
    y = _silu(w_ref[0:1, :] * x_prev + w_ref[1:2, :] * x + w_ref[2:3, :] * x_next + b_ref[...])
    xs = y[:, :BR_W]
    xs_ref[...] = xs
    bc_ref[...] = y[:, BR_W:].astype(bc_ref.dtype)
    t = dt_ref[...] + dtb_ref[...]
    sp = jnp.maximum(t, 0.0) + jnp.log1p(jnp.exp(-jnp.abs(t)))
    lane = lax.broadcasted_iota(jnp.int32, t.shape, 1)
    d_a = sp * alane_ref[...]
    d_a = (jnp.where(lane < SSD_HEADS, d_a, 0.0),
           jnp.where(lane < SSD_HEADS, pltpu.roll(d_a, LANE - SSD_HEADS, axis=1), 0.0))
    for d in range(2):
        for c in range(tm // SSD_T):
            rows = slice(c * SSD_T, (c + 1) * SSD_T)
            a_cum = jnp.dot(tri_ref[d], d_a[d][rows], preferred_element_type=F32, precision=HIGHEST)
            ac_ref[d, rows, :] = a_cum
            act_ref[d, c] = a_cum.T[:SUBLANE, :]
    sp_hi = sp.astype(BF16)
    sp_lo = (sp - sp_hi.astype(F32)).astype(BF16)
    for d in range(2):
        dt_x = (jnp.dot(sp_hi, esel_ref[d], preferred_element_type=F32)
                + jnp.dot(sp_lo, esel_ref[d], preferred_element_type=F32))
        xt_ref[d] = (xs * dt_x).astype(xt_ref.dtype)


def ssd_prep(p_ssd, dt_raw, seq_len, conv_w, conv_b, dt_bias, a_lane, esel, tri):
    r, w = p_ssd.shape
    nb = r // SUBLANE
    per = TM // SUBLANE
    cpt = TM // SSD_T
    return pl.pallas_call(
        functools.partial(_ssd_prep_kernel, seq_len=seq_len),
        out_shape=[jax.ShapeDtypeStruct((r, BR_W), F32), jax.ShapeDtypeStruct((r, 2 * SSD_GN), BF16),
                   jax.ShapeDtypeStruct((2, r, LANE), F32),
                   jax.ShapeDtypeStruct((2, r // SSD_T, SUBLANE, SSD_T), F32),
                   jax.ShapeDtypeStruct((2, r, BR_W), BF16)],
        grid=(r // TM,),
        in_specs=[pl.BlockSpec((TM, w), lambda i: (i, 0)),
                  pl.BlockSpec((SUBLANE, w), lambda i: (jnp.maximum(i * per - 1, 0), 0)),
                  pl.BlockSpec((SUBLANE, w), lambda i: (jnp.minimum((i + 1) * per, nb - 1), 0)),
                  pl.BlockSpec((TM, LANE), lambda i: (i, 0)),
                  _full(conv_w.shape), _full(conv_b.shape), _full(dt_bias.shape), _full(a_lane.shape),
                  _full(esel.shape), _full(tri.shape)],
        out_specs=[pl.BlockSpec((TM, BR_W), lambda i: (i, 0)), pl.BlockSpec((TM, 2 * SSD_GN), lambda i: (i, 0)),
                   pl.BlockSpec((2, TM, LANE), lambda i: (0, i, 0)),
                   pl.BlockSpec((2, cpt, SUBLANE, SSD_T), lambda i: (0, i, 0, 0)),
                   pl.BlockSpec((2, TM, BR_W), lambda i: (0, i, 0))],
        compiler_params=_cp("parallel"), name="ssd_prep",
    )(p_ssd, p_ssd, p_ssd, dt_raw, conv_w, conv_b, dt_bias, a_lane, esel, tri)


def _ssd_scan_kernel(xt_ref, bc_ref, ac_ref, act_ref, tri_ref, h0_ref, y_ref, hfin_ref, st_ref, *, n_chunk, bsz):
    d = pl.program_id(0)
    j = pl.program_id(1)
    T = SSD_T

    @pl.when(j == 0)
    def _():
        st_ref[...] = h0_ref[...]

    tri = tri_ref[...]
    mask = tri > 0.5
    rep = SSD_HEADS // SSD_GROUPS
    low_lane = lax.broadcasted_iota(jnp.int32, (1, LANE), 1) < SSD_HEAD_DIM
    group_row = [(lax.broadcasted_iota(jnp.int32, (SSD_GN, 1), 0) // SSD_STATE) == g for g in range(SSD_GROUPS)]

    def chunk(ci, carry):
        c = ci + d * (n_chunk - 1 - 2 * ci)
        r0 = pl.multiple_of(c * T, T)
        for b in range(bsz):
            a_cum = ac_ref[b, pl.ds(r0, T), :]
            a_cum_t = act_ref[b, c]
            total = jnp.where(d == 0, a_cum[T - 1:T, :], a_cum[0:1, :])
            bc = bc_ref[b, pl.ds(r0, T), :]
            bm_t = bc[:, :SSD_GN].astype(F32).T
            cm = bc[:, SSD_GN:]
            b_grp = [jnp.where(group_row[g], bm_t, 0.0) for g in range(SSD_GROUPS)]
            scores = [jnp.dot(cm, b_grp[g].astype(BF16), preferred_element_type=F32)
                      for g in range(SSD_GROUPS)]
            for k in range(SSD_HEADS // 2):
                x_pair = xt_ref[b, pl.ds(r0, T), k * LANE:(k + 1) * LANE]
                y_d, s_n, e_col, e_tot = [], [], [], []
                for h in (2 * k, 2 * k + 1):
                    g = h // rep
                    col = a_cum[:, h:h + 1]
                    rowv = a_cum_t[h:h + 1, :]
                    tot = total[:, h:h + 1]
                    decay = jnp.exp(jnp.where(mask, col - rowv, -1e30))
                    y_d.append(jnp.dot((scores[g] * decay).astype(BF16), x_pair, preferred_element_type=F32))
                    bw = (b_grp[g] * jnp.exp(tot - rowv)).astype(BF16)
                    s_n.append(jnp.dot(bw, x_pair, preferred_element_type=F32))
                    e_col.append(jnp.exp(col))
                    e_tot.append(jnp.exp(tot))
                s_old = st_ref[b, k]
                y_off = jnp.dot(cm, s_old.astype(BF16), preferred_element_type=F32)
                y = (jnp.where(low_lane, y_d[0], y_d[1])
                     + y_off * jnp.where(low_lane, e_col[0], e_col[1]))
                y_ref[b, pl.ds(r0, T), k * LANE:(k + 1) * LANE] = y.astype(y_ref.dtype)
                st_ref[b, k] = (s_old * jnp.where(low_lane, e_tot[0], e_tot[1])
                                + jnp.where(low_lane, s_n[0], s_n[1]))
        return carry

    lax.fori_loop(0, n_chunk, chunk, 0, unroll=True)
    hfin_ref[...] = st_ref[...]


def ssd_scan(xt, bc, a_cum, a_cum_t, tri, bsz, h0):
    r = bc.shape[0]
    seq_len = r // bsz
    ts = min(seq_len, TM)
    nt = seq_len // ts

    def tile(d, j):
        return j + d * (nt - 1 - 2 * j)

    st_shape = (bsz, SSD_HEADS // 2, SSD_GN, LANE)
    st = pl.BlockSpec((None,) + st_shape, lambda d, j: (d, 0, 0, 0, 0))
    y, hfin = pl.pallas_call(
        functools.partial(_ssd_scan_kernel, n_chunk=ts // SSD_T, bsz=bsz),
        out_shape=[jax.ShapeDtypeStruct((2, bsz, seq_len, BR_W), BF16),
                   jax.ShapeDtypeStruct((2,) + st_shape, F32)],
        grid=(2, nt),
        in_specs=[pl.BlockSpec((None, bsz, ts, BR_W), lambda d, j: (d, 0, tile(d, j), 0)),
                  pl.BlockSpec((bsz, ts, 2 * SSD_GN), lambda d, j: (0, tile(d, j), 0)),
                  pl.BlockSpec((None, bsz, ts, LANE), lambda d, j: (d, 0, tile(d, j), 0)),
                  pl.BlockSpec((None, bsz, ts // SSD_T, SUBLANE, SSD_T), lambda d, j: (d, 0, tile(d, j), 0, 0)),
                  pl.BlockSpec((None, SSD_T, SSD_T), lambda d, j: (d, 0, 0)), st],
        out_specs=[pl.BlockSpec((None, bsz, ts, BR_W), lambda d, j: (d, 0, tile(d, j), 0)), st],
        scratch_shapes=[pltpu.VMEM(st_shape, F32)],
        compiler_params=_cp("arbitrary", "arbitrary"), name="ssd_scan",
    )(xt.reshape(2, bsz, seq_len, BR_W), bc.reshape(bsz, seq_len, 2 * SSD_GN),
      a_cum.reshape(2, bsz, seq_len, LANE), a_cum_t.reshape(2, bsz, seq_len // SSD_T, SUBLANE, SSD_T), tri, h0)
    return y.reshape(2, r, BR_W), hfin


def _rope_table_kernel(frow_ref, fcol_ref, cos_ref, sin_ref):
    tr = cos_ref.shape[0]
    pos = (lax.broadcasted_iota(jnp.int32, (tr, 1), 0) + pl.program_id(0) * tr).astype(F32)
    pos_row = jnp.floor((pos + 0.5) * (1.0 / GRID_W))
    pos_col = pos - pos_row * GRID_W
    ang = pos_row * frow_ref[...] + pos_col * fcol_ref[...]
    lane = lax.broadcasted_iota(jnp.int32, ang.shape, 1)
    quarter = MLA_ROPE // 2
    first = (lane >= MLA_NOPE) & (lane < MLA_NOPE + quarter)
    second = (lane >= MLA_NOPE + quarter) & (lane < MLA_QK)
    c, s = jnp.cos(ang), jnp.sin(ang)
    cos_ref[...] = jnp.where(first | second, c, jnp.where(lane < MLA_NOPE, 1.0, 0.0))
    sin_ref[...] = jnp.where(first, -s, jnp.where(second, s, 0.0))


def _rope_partner(v):
    quarter = MLA_ROPE // 2
    src = np.arange(LANE)
    src[MLA_NOPE:MLA_NOPE + quarter] += quarter
    src[MLA_NOPE + quarter:MLA_QK] -= quarter
    valid = (np.arange(LANE) >= MLA_NOPE) & (np.arange(LANE) < MLA_QK)
    return jnp.where(valid, v[..., src], 0)


def rope_tables(seq_len):
    pairs = MLA_ROPE // 4
    inv_freq = ROPE_BASE ** (-jnp.arange(pairs, dtype=F32) / pairs)
    zeros = lambda n: jnp.zeros((n,), F32)
    half = jnp.concatenate([inv_freq, zeros(pairs)])
    f_row = jnp.concatenate([zeros(MLA_NOPE), half, half, zeros(LANE - MLA_QK)]).reshape(1, LANE)
    half = jnp.concatenate([zeros(pairs), inv_freq])
    f_col = jnp.concatenate([zeros(MLA_NOPE), half, half, zeros(LANE - MLA_QK)]).reshape(1, LANE)
    tr = min(seq_len, 1024)
    spec = pl.BlockSpec((tr, LANE), lambda i: (i, 0))
    return pl.pallas_call(
        _rope_table_kernel, out_shape=[jax.ShapeDtypeStruct((seq_len, LANE), F32)] * 2,
        grid=(seq_len // tr,), in_specs=[_full((1, LANE))] * 2, out_specs=[spec] * 2,
        compiler_params=_cp("parallel"), name="rope_tables",
    )(f_row, f_col)


def _mla_prep_kernel(*refs, rope):
    if rope:
        (p_ref, gq_ref, gkv_ref, wq_ref, wkv_ref, nq_ref, nk_ref, wqs_ref, nqs_ref, nks_ref, cos_ref, sin_ref,
         q_ref, k_ref, v_ref) = refs
    else:
        p_ref, gq_ref, gkv_ref, wq_ref, wkv_ref, nq_ref, nk_ref, q_ref, k_ref, v_ref = refs
    p = p_ref[...].astype(F32)
    cq = (_rms(p[:, :MLA_Q_LORA], MLA_Q_LORA) * gq_ref[...]).astype(BF16)
    ckv = (_rms(p[:, MLA_Q_LORA:MLA_Q_LORA + MLA_KV_LORA], MLA_KV_LORA) * gkv_ref[...]).astype(BF16)
    kr = p[:, MLA_Q_LORA + MLA_KV_LORA:MLA_Q_LORA + MLA_KV_LORA + LANE]
    q_all = jnp.dot(cq, wq_ref[...], preferred_element_type=F32)
    kv_all = jnp.dot(ckv, wkv_ref[...], preferred_element_type=F32)
    lane = lax.broadcasted_iota(jnp.int32, (1, LANE), 1)
    one_col = jnp.where(lane == MLA_V, 1.0, 0.0)
    scale = MLA_QK ** -0.5 * LOG2E
    if rope:
        qs_all = jnp.dot(cq, wqs_ref[...], preferred_element_type=F32)
        q_cos, q_sin = nq_ref[...] * cos_ref[...], nqs_ref[...] * sin_ref[...]
        k_cos = nk_ref[...] * cos_ref[...]
        k_part = p[:, MLA_Q_LORA + MLA_KV_LORA + LANE:] * (nks_ref[...] * sin_ref[...])

    def inv_rms(t):
        return lax.rsqrt(jnp.sum(t * t, axis=-1, keepdims=True) * (1.0 / MLA_QK) + 1e-6)

    for h in range(MLA_HEADS):
        q = q_all[:, h * LANE:(h + 1) * LANE]
        k = kv_all[:, 2 * h * LANE:(2 * h + 1) * LANE] + kr
        if rope:
            q_out = (q * q_cos + qs_all[:, h * LANE:(h + 1) * LANE] * q_sin) * (inv_rms(q) * scale)
            k_out = (k * k_cos + k_part) * inv_rms(k)
        else:
            q_out = q * nq_ref[...] * (inv_rms(q) * scale)
            k_out = k * nk_ref[...] * inv_rms(k)
        q_ref[h] = q_out.astype(q_ref.dtype)
        k_ref[h] = k_out.astype(k_ref.dtype)
        v_ref[h] = (kv_all[:, (2 * h + 1) * LANE:(2 * h + 2) * LANE] + one_col).astype(v_ref.dtype)


def mla_prep(p_mla, bsz, gq, gkv, wq, wkv, nq, nk, rope_args):
    r, w = p_mla.shape
    seq_len = r // bsz
    tr = min(seq_len, TM)
    nt = seq_len // tr
    rope = rope_args is not None
    in_specs = [pl.BlockSpec((tr, w), lambda b, i: (b * nt + i, 0)), _full(gq.shape), _full(gkv.shape),
                _wspec(wq), _wspec(wkv), _full(nq.shape), _full(nk.shape)]
    args = [p_mla, gq, gkv, _warg(wq), _warg(wkv), nq, nk]
    if rope:
        in_specs += [_wspec(a) for a in rope_args[:3]]
        in_specs += [pl.BlockSpec((tr, LANE), lambda b, i: (i, 0))] * 2
        args += [_warg(a) for a in rope_args]
    head = pl.BlockSpec((None, MLA_HEADS, tr, LANE), lambda b, i: (b, 0, i, 0))
    return pl.pallas_call(
        functools.partial(_mla_prep_kernel, rope=rope),
        out_shape=[jax.ShapeDtypeStruct((bsz, MLA_HEADS, seq_len, LANE), BF16)] * 3,
        grid=(bsz, nt), in_specs=in_specs, out_specs=[head] * 3,
        compiler_params=_cp("parallel", "parallel"), name="mla_prep",
    )(*args)


def _attn_kernel(*refs, n_kv):
    bound_ref, q_ref = refs[0], refs[1]
    kv_refs = refs[2:2 + 2 * n_kv]
    o_ref = refs[2 + 2 * n_kv]
    m_ref = refs[3 + 2 * n_kv]
    scores = [[lax.dot_general(q_ref[hh], kv_refs[2 * i][hh], (((1,), (1,)), ((), ())),
                               preferred_element_type=F32) for i in range(n_kv)] for hh in range(2)]

    def row_max(first_set):
        for hh in range(2):
            m = scores[hh][n_kv - 1].max(axis=-1, keepdims=True)
            for si in scores[hh][first_set:n_kv - 1]:
                m = jnp.maximum(m, si.max(axis=-1, keepdims=True))
            m_ref[hh] = m

    if n_kv > 1:
        fast = bound_ref[0] <= ATTN_FAST_BOUND

        @pl.when(fast)
        def _():
            row_max(n_kv - 1)

        @pl.when(jnp.logical_not(fast))
        def _():
            row_max(0)
    else:
        row_max(0)

    for hh in range(2):
        m = m_ref[hh]
        acc = None
        for i, si in enumerate(scores[hh]):
            pv = jnp.dot(jnp.exp2((si - m).astype(BF16)), kv_refs[2 * i + 1][hh], preferred_element_type=F32)
            acc = pv if acc is None else acc + pv
        o = acc[:, :MLA_V] / acc[:, MLA_V:MLA_V + 1]
        o_ref[:, hh * MLA_V:(hh + 1) * MLA_V] = o.astype(o_ref.dtype)


def attention(q, kvs, score_bound):
    bsz, nh, lq, _ = q.shape
    tq = min(lq, TQ)
    nq = lq // tq
    in_specs = [pl.BlockSpec(memory_space=pltpu.SMEM),
                pl.BlockSpec((None, 2, tq, LANE), lambda b, hp, i: (b, hp, i, 0))]
    args = [score_bound, q]
    for k, v in kvs:
        spec = pl.BlockSpec((None, 2, k.shape[2], LANE), lambda b, hp, i: (b, hp, 0, 0))
        in_specs += [spec, spec]
        args += [k, v]
    return pl.pallas_call(
        functools.partial(_attn_kernel, n_kv=len(kvs)),
        out_shape=jax.ShapeDtypeStruct((bsz * lq, nh * MLA_V), BF16),
        grid=(bsz, nh // 2, nq), in_specs=in_specs,
        out_specs=pl.BlockSpec((tq, 2 * MLA_V), lambda b, hp, i: (b * nq + i, hp)),
        scratch_shapes=[pltpu.VMEM((2, tq, 1), F32)],
        compiler_params=_cp("parallel", "parallel", "arbitrary"), name="attention",
    )(*args)


def _merge_kernel(x_ref, h_ref, u_ref, s5f_ref, s5b_ref, s5d_ref, wglu_ref, b_ref,
                  xs_ref, ssdf_ref, ssdb_ref, z_ref, ssdd_ref, ssdg_ref, d_ref,
                  g1_ref, wg_ref, wb_ref, wo_ref,
                  n2_ref, sh2_ref, sc2_ref, g2_ref, wi_ref, wfo_ref, o_ref, *, n_split):
    y = s5d_ref[...] * u_ref[...].astype(F32) + s5f_ref[...].astype(F32) + s5b_ref[...].astype(F32)
    g = jax.nn.gelu(y)
    a = (g * jax.nn.sigmoid(jnp.dot(g.astype(BF16), wglu_ref[...], preferred_element_type=F32))).astype(BF16)
    y = ssdd_ref[...] * xs_ref[...] + ssdf_ref[...].astype(F32) + ssdb_ref[...].astype(F32)
    c = (_rms(y * _silu(z_ref[...].astype(F32)), BR_W) * ssdg_ref[...]).astype(BF16)
    h = h_ref[...]
    dm = x_ref.shape[-1]
    merged = None
    for i, br in enumerate((a, b_ref[...], c, d_ref[...])):
        gate = jax.nn.sigmoid(jnp.dot(h, wg_ref[:, i * dm:(i + 1) * dm], preferred_element_type=F32))
        term = gate * jnp.dot(br, wb_ref[i], preferred_element_type=F32)
        merged = term if merged is None else merged + term
    mix = jnp.dot(merged.astype(BF16), wo_ref[...], preferred_element_type=F32)
    x = x_ref[...] + g1_ref[...] * mix

    y = _rms(x, dm) * n2_ref[...]
    hb = (y * (1.0 + sc2_ref[...]) + sh2_ref[...]).astype(BF16)
    hid = wfo_ref.shape[0]
    step = hid // n_split
    acc = None
    for c in range(n_split):
        gate = jnp.dot(hb, wi_ref[:, c * step:(c + 1) * step], preferred_element_type=F32)
        up = jnp.dot(hb, wi_ref[:, hid + c * step:hid + (c + 1) * step], preferred_element_type=F32)
        part = jnp.dot((_silu(gate) * up).astype(BF16), wfo_ref[c * step:(c + 1) * step, :],
                       preferred_element_type=F32)
        acc = part if acc is None else acc + part
    o_ref[...] = x + g2_ref[...] * acc


def merge_ffn(x, h, s5_in, b, ssd_in, d, mod4, l, midx, wg, wb, wo, norm2, wi, wfo):
    r, dm = x.shape
    n_split = 11 if wfo.shape[0] % (11 * LANE) == 0 else 1
    row = pl.BlockSpec((TM, dm), lambda i: (i, 0))
    br = pl.BlockSpec((TM, BR_W), lambda i: (i, 0))
    fwd = pl.BlockSpec((None, TM, BR_W), lambda i: (0, i, 0))
    bwd = pl.BlockSpec((None, TM, BR_W), lambda i: (1, i, 0))
    vec = _full((1, BR_W))
    u, y_s5, s5_d, w_glu = s5_in
    xs, y_ssd, p_ssd, ssd_d, ssd_g = ssd_in
    return pl.pallas_call(
        functools.partial(_merge_kernel, n_split=n_split),
        out_shape=jax.ShapeDtypeStruct((r, dm), F32), grid=(r // TM,),
        in_specs=[row, row, br, fwd, bwd, vec, _wspec(w_glu), br,
                  br, fwd, bwd, br, vec, vec, br,
                  _mod_spec(l, midx, 2, dm), _wspec(wg), _wspec(wb), _wspec(wo),
                  _full((1, dm)), _mod_spec(l, midx, 3, dm), _mod_spec(l, midx, 4, dm),
                  _mod_spec(l, midx, 5, dm), _wspec(wi), _wspec(wfo)],
        out_specs=row, compiler_params=_cp("parallel"), name="merge_ffn",
    )(x, h, u, y_s5, y_s5, s5_d, _warg(w_glu), b, xs, y_ssd, y_ssd, p_ssd, ssd_d, ssd_g, d, mod4,
      _warg(wg), _warg(wb), _warg(wo), norm2, mod4, mod4, mod4, _warg(wi), _warg(wfo))


def _pad_cols(w, n):
    return jnp.pad(w, ((0, 0),) * (w.ndim - 1) + ((0, n - w.shape[-1]),))


def kernel(x, c, ctx, c_ctx, w_ada, b_ada, norm1_g, norm2_g, w_in, s5_a_re, s5_a_im, s5_b_re, s5_b_im, s5_c_re, s5_c_im, s5_log_dt, s5_d, s5_w_glu, sgu_ln_g, sgu_ln_b, sgu_w_s, sgu_b_s, ssd_conv_w, ssd_conv_b, ssd_a_log, ssd_dt_bias, ssd_d, ssd_norm_g, mla_q_a_norm, mla_w_uq, mla_kv_a_norm, mla_w_ukv, mla_q_norm, mla_k_norm, w_branch, w_out, w_ffn_in, w_ffn_out):
    bsz, seq, dm = x.shape
    lc = ctx.shape[1]
    depth = w_ada.shape[0]
    assert seq % TM == 0 and (bsz * lc) % TM == 0 and seq % (S5_NSEG * SUBLANE) == 0
    assert lc % SSD_T == 0 and lc % (S5_NSEG * SUBLANE) == 0 and bsz + 1 <= 8

    cc8 = jnp.zeros((8, dm), F32).at[:bsz].set(c.astype(F32)).at[bsz].set(c_ctx.astype(F32))
    mod4 = ada_table(cc8, w_ada.astype(F32), b_ada.astype(F32)).reshape(depth, 8, 1, 6 * dm)
    lat_tiles = seq // TM
    midx_lat = lambda i: i // lat_tiles
    midx_ctx = lambda i: bsz

    off = np.cumsum([0, BR_W, 2 * BR_W, BR_W + SSD_CONV_CH + 2 * SSD_HEADS,
                     MLA_Q_LORA + MLA_KV_LORA + MLA_ROPE, N_BRANCH * dm])
    w_s5 = w_in[:, :, off[0]:off[1]].astype(BF16)
    w_sgu = w_in[:, :, off[1]:off[2]].astype(BF16)
    w_ssd = w_in[:, :, off[2]:off[2] + BR_W + SSD_CONV_CH].astype(BF16)
    w_dt = _pad_cols(w_in[:, :, off[2] + BR_W + SSD_CONV_CH:off[3]], LANE).astype(BF16)
    w_mla_main = w_in[:, :, off[3]:off[3] + MLA_Q_LORA + MLA_KV_LORA]
    w_kr = w_in[:, :, off[3] + MLA_Q_LORA + MLA_KV_LORA:off[4]]
    zeros = lambda n: jnp.zeros((depth, dm, n), w_in.dtype)
    kr_block = jnp.concatenate([zeros(MLA_NOPE), w_kr, zeros(LANE - MLA_QK)], axis=-1)
    w_mla = jnp.concatenate([w_mla_main, kr_block, _rope_partner(kr_block)], axis=-1).astype(BF16)
    w_gate = w_in[:, :, off[4]:off[5]].astype(BF16)

    lam_re, lam_im, bb_re, bb_im = s5_discretise(s5_a_re, s5_a_im, s5_b_re, s5_b_im, s5_log_dt)
    lam, bblk, cblk = s5_pack(lam_re, lam_im, bb_re, bb_im, s5_c_re, s5_c_im)
    s5_wg = s5_w_glu.astype(BF16)

    sgu_w = sgu_w_s.reshape(depth, SGU_HEADS // 2, 2, SGU_CHUNK, SGU_CHUNK)
    sgu_w = jnp.concatenate([sgu_w[:, :, 0], sgu_w[:, :, 1]], axis=-1).astype(BF16)
    sgu_b = jnp.repeat(jnp.swapaxes(sgu_b_s, 1, 2), BR_W // SGU_HEADS, axis=-1).astype(F32)

    a_neg = -jnp.exp(ssd_a_log.astype(F32))
    a_lane = _pad_cols(a_neg.reshape(depth, 1, 2 * SSD_HEADS), LANE)
    head_of_lane = jnp.arange(LANE)[None, :, None] - SSD_HEADS * jnp.arange(2)[:, None, None]
    col_head = jnp.arange(BR_W) // SSD_HEAD_DIM
    esel = ((head_of_lane == col_head[None, None, :]) & (head_of_lane >= 0)).astype(BF16)
    ti = jnp.arange(SSD_T)
    tri = jnp.stack([ti[None, :] <= ti[:, None], ti[None, :] >= ti[:, None]]).astype(F32)
    dt_bias = _pad_cols(ssd_dt_bias.astype(F32).reshape(depth, 1, 2 * SSD_HEADS), LANE)
    ssd_dskip = jnp.repeat(ssd_d.astype(F32), SSD_HEAD_DIM, axis=-1)[:, None, :]

    wq = mla_w_uq.reshape(depth, MLA_Q_LORA, MLA_HEADS, MLA_QK)
    wq = _pad_cols(wq, LANE)
    wq_sw = _rope_partner(wq).reshape(depth, MLA_Q_LORA, MLA_HEADS * LANE).astype(BF16)
    wq = wq.reshape(depth, MLA_Q_LORA, MLA_HEADS * LANE).astype(BF16)
    wkv = mla_w_ukv.reshape(depth, MLA_KV_LORA, MLA_HEADS, 2, MLA_NOPE)
    wkv = _pad_cols(wkv, LANE).reshape(depth, MLA_KV_LORA, MLA_HEADS * 2 * LANE).astype(BF16)
    nq = _pad_cols(mla_q_norm.astype(F32), LANE)[:, None, :]
    nk = _pad_cols(mla_k_norm.astype(F32), LANE)[:, None, :]
    nq_sw, nk_sw = _rope_partner(nq), _rope_partner(nk)
    rope_cos, rope_sin = rope_tables(seq)
    score_bound = (MLA_QK ** 0.5 * LOG2E * jnp.max(jnp.abs(mla_q_norm.astype(F32)), axis=-1)
                   * jnp.max(jnp.abs(mla_k_norm.astype(F32)), axis=-1))

    wb = w_branch.astype(BF16)
    wo = w_out.astype(BF16)
    wfi = w_ffn_in.astype(BF16)
    wfo = w_ffn_out.astype(BF16)

    row = lambda v, l: v[l].astype(F32).reshape(1, -1)
    x_lat = x.astype(F32).reshape(bsz * seq, dm)
    x_ctx = ctx.astype(F32).reshape(bsz * lc, dm)
    s5_zero = jnp.zeros((bsz, 2, S5_JB, 1, 2 * S5_BW), F32)
    ssd_zero = jnp.zeros((2, bsz, SSD_HEADS // 2, SSD_GN, LANE), F32)

    for l in range(depth):
        need_ctx = l < depth - 1
        lay = lambda w: _Layer(w, l)
        ws = tuple(lay(w) for w in (w_s5, w_sgu, w_ssd, w_dt, w_mla))
        w_merge = (lay(w_gate), lay(wb), lay(wo))
        w_ffn = (lay(wfi), lay(wfo))
        dts = (BF16, BF16, BF16, F32, BF16)
        g1 = row(norm1_g, l)
        h_l, u_l, z_l, p_l, dtr_l, m_l = in_proj(x_lat, g1, mod4, l, midx_lat, ws, dts)
        h_c, u_c, z_c, p_c, dtr_c, m_c = in_proj(x_ctx, g1, mod4, l, midx_ctx, ws, dts)

        y_c, s5_h = s5_scan(u_c, bblk, lam, cblk, l, bsz, s5_zero)
        y_l, _ = s5_scan(u_l, bblk, lam, cblk, l, bsz, s5_h)
        s5_tail = (row(s5_d, l), lay(s5_wg))
        b_l = sgu(z_l, row(sgu_ln_g, l), row(sgu_ln_b, l), lay(sgu_w), lay(sgu_b))
        conv_w, conv_b = ssd_conv_w[l].astype(F32), row(ssd_conv_b, l)
        ssd_w = (conv_w, conv_b, dt_bias[l], a_lane[l], esel, tri)
        xs_c, bc_c, ac_c, act_c, xt_c = ssd_prep(p_c, dtr_c, lc, *ssd_w)
        xs_l, bc_l, ac_l, act_l, xt_l = ssd_prep(p_l, dtr_l, seq, *ssd_w)
        yc_c, ssd_h = ssd_scan(xt_c, bc_c, ac_c, act_c, tri, bsz, ssd_zero)
        yc_l, _ = ssd_scan(xt_l, bc_l, ac_l, act_l, tri, bsz, ssd_h)
        ssd_tail = (ssd_dskip[l], row(ssd_norm_g, l))
        mla_w = (row(mla_q_a_norm, l), row(mla_kv_a_norm, l), lay(wq), lay(wkv), nq[l], nk[l])
        q_c, k_c, v_c = mla_prep(m_c, bsz, *mla_w, None)
        q_l, k_l, v_l = mla_prep(m_l, bsz, *mla_w, (lay(wq_sw), nq_sw[l], nk_sw[l], rope_cos, rope_sin))
        d_l = attention(q_l, [(k_l, v_l), (k_c, v_c)], score_bound[l:l + 1])

        x_lat = merge_ffn(x_lat, h_l, (u_l, y_l) + s5_tail, b_l, (xs_l, yc_l, p_l) + ssd_tail, d_l,
                          mod4, l, midx_lat, *w_merge, row(norm2_g, l), *w_ffn)
        if need_ctx:
            b_c = sgu(z_c, row(sgu_ln_g, l), row(sgu_ln_b, l), lay(sgu_w), lay(sgu_b))
            d_c = attention(q_c, [(k_c, v_c)], score_bound[l:l + 1])
            x_ctx = merge_ffn(x_ctx, h_c, (u_c, y_c) + s5_tail, b_c, (xs_c, yc_c, p_c) + ssd_tail, d_c,
                              mod4, l, midx_ctx, *w_merge, row(norm2_g, l), *w_ffn)
    return x_lat.reshape(bsz, seq, dm).astype(x.dtype)
```

```python
import functools

import jax
import jax.numpy as jnp
import numpy as np
from jax import lax
from jax.experimental import pallas as pl
from jax.experimental.pallas import tpu as pltpu

F32 = jnp.float32
BF16 = jnp.bfloat16
HIGHEST = lax.Precision.HIGHEST

LANE = 128
SUBLANE = 8
VMEM_LIMIT = 56 * 1024 * 1024

GRID_W = 64
BR_W = 384
S5_GROUP = 16
S5_GROUPS = BR_W // S5_GROUP
S5_STATE = 64
S5_NSEG = SUBLANE
S5_JB = BR_W // LANE
S5_BW = (LANE // S5_GROUP) * S5_STATE
SGU_CHUNK = 128
SGU_HEADS = 6
SSD_HEADS = 6
SSD_HEAD_DIM = 64
SSD_GROUPS = 2
SSD_STATE = 64
SSD_GN = SSD_GROUPS * SSD_STATE
SSD_CONV_CH = BR_W + 2 * SSD_GN
SSD_T = 128
MLA_HEADS = 6
MLA_NOPE = 64
MLA_ROPE = 32
MLA_V = 64
MLA_QK = MLA_NOPE + MLA_ROPE
MLA_Q_LORA = 384
MLA_KV_LORA = 256
ROPE_BASE = 10000.0
LOG2E = 1.4426950408889634
N_BRANCH = 4

TM = 512
TQ = 256
ATTN_FAST_BOUND = 30.0


def _cp(*sem):
    return pltpu.CompilerParams(dimension_semantics=sem, vmem_limit_bytes=VMEM_LIMIT)


def _full(shape):
    n = len(shape)
    return pl.BlockSpec(shape, lambda *_: (0,) * n)


class _Layer:
    def __init__(self, arr, l):
        self.arr, self.l = arr, l

    @property
    def shape(self):
        return self.arr.shape[1:]


def _wspec(w):
    if not isinstance(w, _Layer):
        return _full(w.shape)
    n, l = len(w.shape), w.l
    return pl.BlockSpec((None,) + tuple(w.shape), lambda *_: (l,) + (0,) * n, pipeline_mode=pl.Buffered(1))


def _warg(w):
    return w.arr if isinstance(w, _Layer) else w


def _silu(x):
    return x * jax.nn.sigmoid(x)


def _rms(x, n, eps=1e-6):
    return x * lax.rsqrt(jnp.sum(x * x, axis=-1, keepdims=True) * (1.0 / n) + eps)


def _ada_kernel(cc_ref, w_ref, b_ref, o_ref):
    s = _silu(cc_ref[...])
    o_ref[...] = jnp.dot(s, w_ref[...], preferred_element_type=F32, precision=HIGHEST) + b_ref[...]


def ada_table(cc8, w_ada, b_ada):
    depth, d, n = w_ada.shape
    tn = n // 4
    return pl.pallas_call(
        _ada_kernel,
        out_shape=jax.ShapeDtypeStruct((depth, 8, n), F32),
        grid=(depth, n // tn),
        in_specs=[pl.BlockSpec((8, d), lambda l, j: (0, 0)),
                  pl.BlockSpec((None, d, tn), lambda l, j: (l, 0, j)),
                  pl.BlockSpec((None, 1, tn), lambda l, j: (l, 0, j))],
        out_specs=pl.BlockSpec((None, 8, tn), lambda l, j: (l, 0, j)),
        compiler_params=_cp("arbitrary", "arbitrary"),
        name="ada_table",
    )(cc8, w_ada, b_ada.reshape(depth, 1, n))


def _mod_spec(l, midx, col, d):
    return pl.BlockSpec((None, None, 1, d), lambda i: (l, midx(i), 0, col))


def _in_kernel(x_ref, g_ref, sh_ref, sc_ref, w1, w2, w3, w4, w5, h_ref, o1, o2, o3, o4, o5):
    x = x_ref[...]
    y = _rms(x, x.shape[-1]) * g_ref[...]
    hb = (y * (1.0 + sc_ref[...]) + sh_ref[...]).astype(BF16)
    h_ref[...] = hb
    for w, o in ((w1, o1), (w2, o2), (w3, o3), (w4, o4), (w5, o5)):
        o[...] = jnp.dot(hb, w[...], preferred_element_type=F32).astype(o.dtype)


def in_proj(x, g, mod4, l, midx, ws, out_dtypes):
    r, d = x.shape
    in_specs = [pl.BlockSpec((TM, d), lambda i: (i, 0)), _full((1, d)),
                _mod_spec(l, midx, 0, d), _mod_spec(l, midx, 1, d)]
    in_specs += [_wspec(w) for w in ws]
    out_shape = [jax.ShapeDtypeStruct((r, d), BF16)]
    out_specs = [pl.BlockSpec((TM, d), lambda i: (i, 0))]
    for w, dt in zip(ws, out_dtypes):
        out_shape.append(jax.ShapeDtypeStruct((r, w.shape[1]), dt))
        out_specs.append(pl.BlockSpec((TM, w.shape[1]), lambda i: (i, 0)))
    return pl.pallas_call(
        _in_kernel, out_shape=out_shape, grid=(r // TM,), in_specs=in_specs, out_specs=out_specs,
        compiler_params=_cp("parallel"), name="in_proj",
    )(x, g, mod4, mod4, *[_warg(w) for w in ws])


def _s5_disc_kernel(are, aim, ldt, bre, bim, lam_re, lam_im, bbre, bbim):
    a_re, a_im = are[...], aim[...]
    dt = jnp.exp(ldt[...])
    mag = jnp.exp(a_re * dt)
    ang = a_im * dt
    ab_re = mag * jnp.cos(ang)
    ab_im = mag * jnp.sin(ang)
    den = a_re * a_re + a_im * a_im
    f_re = ((ab_re - 1.0) * a_re + ab_im * a_im) / den
    f_im = (ab_im * a_re - (ab_re - 1.0) * a_im) / den
    lam_re[...] = ab_re
    lam_im[...] = ab_im
    for c in range(S5_GROUP):
        bbre[c] = f_re * bre[c] - f_im * bim[c]
        bbim[c] = f_re * bim[c] + f_im * bre[c]


def s5_discretise(a_re, a_im, b_re, b_im, log_dt):
    shp = a_re.shape
    rows = int(np.prod(shp)) // LANE
    are = a_re.astype(F32).reshape(rows, LANE)
    aim = a_im.astype(F32).reshape(rows, LANE)
    ldt = jnp.broadcast_to(log_dt.astype(F32)[..., None], shp).reshape(rows, LANE)
    bre = jnp.moveaxis(b_re.astype(F32), -1, 0).reshape(S5_GROUP, rows, LANE)
    bim = jnp.moveaxis(b_im.astype(F32), -1, 0).reshape(S5_GROUP, rows, LANE)
    outs = pl.pallas_call(
        _s5_disc_kernel,
        out_shape=[jax.ShapeDtypeStruct((rows, LANE), F32)] * 2
        + [jax.ShapeDtypeStruct((S5_GROUP, rows, LANE), F32)] * 2,
        name="s5_discretise",
    )(are, aim, ldt, bre, bim)
    lam_re, lam_im = outs[0].reshape(shp), outs[1].reshape(shp)
    bb_re = jnp.moveaxis(outs[2].reshape((S5_GROUP,) + shp), 0, -1)
    bb_im = jnp.moveaxis(outs[3].reshape((S5_GROUP,) + shp), 0, -1)
    return lam_re, lam_im, bb_re, bb_im


def s5_pack(lam_re, lam_im, bb_re, bb_im, c_re, c_im):
    depth = lam_re.shape[0]
    gpb = LANE // S5_GROUP
    eye = jnp.eye(gpb, dtype=F32)

    def lam_blocks(v):
        return v.reshape(depth, 2, S5_JB, 1, S5_BW)

    lam = jnp.concatenate([lam_blocks(lam_re), lam_blocks(lam_im)], axis=-1)
    lam = jnp.broadcast_to(lam, (depth, 2, S5_JB, SUBLANE, 2 * S5_BW))

    def b_blocks(bb):
        v = bb.reshape(depth, 2, S5_JB, gpb, S5_STATE, S5_GROUP)
        return jnp.einsum('ldjgpc,gh->ldjgchp', v, eye).reshape(depth, 2, S5_JB, LANE, S5_BW)

    bblk = jnp.concatenate([b_blocks(bb_re), b_blocks(bb_im)], axis=-1).astype(BF16)

    def c_blocks(cc):
        v = cc.astype(F32).reshape(depth, 2, S5_JB, gpb, S5_GROUP, S5_STATE)
        return jnp.einsum('ldjgcp,gh->ldjhpgc', v, eye).reshape(depth, 2, S5_JB, S5_BW, LANE)

    cblk = jnp.concatenate([c_blocks(c_re), -c_blocks(c_im)], axis=-2).astype(BF16)
    return lam, bblk, cblk


def _cpow(re, im, n):
    out = None
    while n:
        if n & 1:
            out = (re, im) if out is None else (out[0] * re - out[1] * im, out[0] * im + out[1] * re)
        n >>= 1
        if n:
            re, im = re * re - im * im, 2.0 * re * im
    return out


def _s5_pass_kernel(*refs, tt, lseg, with_y, n_sub):
    if with_y:
        u_ref, bblk, lam, cblk, sloc, h0, y_ref, hfin, uperm, hst, yperm = refs[:11]
    else:
        u_ref, bblk, lam, s_out, uperm, hst = refs[:6]
    bus = refs[-n_sub * S5_JB:]
    d = pl.program_id(1)
    j = pl.program_id(2)
    offs = [pl.multiple_of((ph + d * (n_sub - 1 - 2 * ph)) * tt, tt) for ph in range(n_sub)]

    @pl.when(j == 0)
    def _():
        if not with_y:
            hst[...] = jnp.zeros(hst.shape, F32)
        else:
            for jb in range(S5_JB):
                pr, pi = _cpow(lam[jb, 0:1, :S5_BW], lam[jb, 0:1, S5_BW:], lseg)

                def chain(order, jb=jb, pr=pr, pi=pi):
                    cr, ci = h0[jb, :, :S5_BW], h0[jb, :, S5_BW:]
                    for s in order:
                        hst[jb, s:s + 1, :S5_BW] = cr
                        hst[jb, s:s + 1, S5_BW:] = ci
                        sr, si = sloc[jb, s:s + 1, :S5_BW], sloc[jb, s:s + 1, S5_BW:]
                        cr, ci = pr * cr - pi * ci + sr, pr * ci + pi * cr + si
                    hfin[jb, :, :S5_BW] = cr
                    hfin[jb, :, S5_BW:] = ci

                @pl.when(d == 0)
                def _():
                    chain(range(S5_NSEG))

                @pl.when(d == 1)
                def _():
                    chain(range(S5_NSEG - 1, -1, -1))

    def drive(ph):
        for s in range(S5_NSEG):
            us = u_ref[s, pl.ds(offs[ph], tt), :].astype(F32)
            for k in range(S5_JB):
                uperm[ph, k, pl.ds(s, tt, stride=S5_NSEG), :] = us[:, k * LANE:(k + 1) * LANE]
        for jb in range(S5_JB):
            bus[ph * S5_JB + jb][...] = jnp.dot(uperm[ph, jb].astype(BF16), bblk[jb], preferred_element_type=F32)

    def scan(ph):
        for jb in range(S5_JB):
            bu = bus[ph * S5_JB + jb]
            lr, li = lam[jb, :, :S5_BW], lam[jb, :, S5_BW:]

            def step(i, carry, lr=lr, li=li, bu=bu):
                hr, hi = carry
                t = i + d * (tt - 1 - 2 * i)
                r0 = pl.multiple_of(t * S5_NSEG, S5_NSEG)
                nr = lr * hr - li * hi + bu[pl.ds(r0, S5_NSEG), :S5_BW]
                ni = lr * hi + li * hr + bu[pl.ds(r0, S5_NSEG), S5_BW:]
                if with_y:
                    bu[pl.ds(r0, S5_NSEG), :S5_BW] = nr
                    bu[pl.ds(r0, S5_NSEG), S5_BW:] = ni
                return nr, ni

            hr, hi = lax.fori_loop(0, tt, step, (hst[jb, :, :S5_BW], hst[jb, :, S5_BW:]), unroll=True)
            hst[jb, :, :S5_BW] = hr
            hst[jb, :, S5_BW:] = hi

    def read(ph):
        for jb in range(S5_JB):
            yperm[ph, jb] = jnp.dot(bus[ph * S5_JB + jb][...].astype(BF16), cblk[jb],
                                    preferred_element_type=F32)
        for s in range(S5_NSEG):
            for k in range(S5_JB):
                y_ref[s, pl.ds(offs[ph], tt), k * LANE:(k + 1) * LANE] = (
                    yperm[ph, k, pl.ds(s, tt, stride=S5_NSEG), :].astype(y_ref.dtype))

    for ph in range(n_sub):
        drive(ph)
    for ph in range(n_sub):
        scan(ph)
        if with_y:
            read(ph)
    if not with_y:
        s_out[...] = hst[...]


def s5_scan(u, bblk, lam, cblk, l, bsz, h0):
    seq_len = u.shape[0] // bsz
    lseg = seq_len // S5_NSEG
    tt = min(lseg, 64)
    n_sub = next(n for n in (4, 2, 1) if (lseg // tt) % n == 0)
    nt = lseg // (tt * n_sub)
    u5 = u.reshape(bsz, S5_NSEG, lseg, BR_W)

    def tile(d, j):
        return j + d * (nt - 1 - 2 * j)

    u_spec = pl.BlockSpec((None, S5_NSEG, n_sub * tt, BR_W), lambda b, d, j: (b, 0, tile(d, j), 0))
    y_spec = pl.BlockSpec((None, None, S5_NSEG, n_sub * tt, BR_W), lambda b, d, j: (d, b, 0, tile(d, j), 0))
    w_b = pl.BlockSpec((None, None, S5_JB, LANE, 2 * S5_BW), lambda b, d, j: (l, d, 0, 0, 0))
    w_lam = pl.BlockSpec((None, None, S5_JB, SUBLANE, 2 * S5_BW), lambda b, d, j: (l, d, 0, 0, 0))
    w_c = pl.BlockSpec((None, None, S5_JB, 2 * S5_BW, LANE), lambda b, d, j: (l, d, 0, 0, 0))
    st8 = pl.BlockSpec((None, None, S5_JB, SUBLANE, 2 * S5_BW), lambda b, d, j: (b, d, 0, 0, 0))
    st1 = pl.BlockSpec((None, None, S5_JB, 1, 2 * S5_BW), lambda b, d, j: (b, d, 0, 0, 0))
    n_rows = S5_NSEG * tt
    scratch = [pltpu.VMEM((n_sub, S5_JB, n_rows, LANE), F32), pltpu.VMEM((S5_JB, SUBLANE, 2 * S5_BW), F32)]
    bus = [pltpu.VMEM((n_rows, 2 * S5_BW), F32)] * (n_sub * S5_JB)
    grid = (bsz, 2, nt)
    cp = _cp("arbitrary", "arbitrary", "arbitrary")

    sloc = pl.pallas_call(
        functools.partial(_s5_pass_kernel, tt=tt, lseg=lseg, with_y=False, n_sub=n_sub),
        out_shape=jax.ShapeDtypeStruct((bsz, 2, S5_JB, SUBLANE, 2 * S5_BW), F32),
        grid=grid, in_specs=[u_spec, w_b, w_lam], out_specs=st8,
        scratch_shapes=scratch + bus, compiler_params=cp, name="s5_local",
    )(u5, bblk, lam)

    y, hfin = pl.pallas_call(
        functools.partial(_s5_pass_kernel, tt=tt, lseg=lseg, with_y=True, n_sub=n_sub),
        out_shape=[jax.ShapeDtypeStruct((2, bsz, S5_NSEG, lseg, BR_W), BF16),
                   jax.ShapeDtypeStruct((bsz, 2, S5_JB, 1, 2 * S5_BW), F32)],
        grid=grid, in_specs=[u_spec, w_b, w_lam, w_c, st8, st1], out_specs=[y_spec, st1],
        scratch_shapes=scratch + [pltpu.VMEM((n_sub, S5_JB, n_rows, LANE), F32)] + bus,
        compiler_params=cp, name="s5_emit",
    )(u5, bblk, lam, cblk, sloc, h0)
    return y.reshape(2, bsz * seq_len, BR_W), hfin


def _sgu_kernel(z_ref, g_ref, b_ref, w_ref, bias_ref, o_ref):
    z = jax.nn.gelu(z_ref[...].astype(F32))
    u, v = z[:, :BR_W], z[:, BR_W:]
    mu = jnp.mean(v, axis=-1, keepdims=True)
    vc = v - mu
    vn = vc * lax.rsqrt(jnp.mean(vc * vc, axis=-1, keepdims=True) + 1e-5) * g_ref[...] + b_ref[...]
    n_chunk = z.shape[0] // SGU_CHUNK
    half = LANE // 2
    lane = lax.broadcasted_iota(jnp.int32, (SGU_CHUNK, LANE), 1)
    for k in range(BR_W // LANE):
        cols = []
        for c in range(n_chunk):
            blk = vn[c * SGU_CHUNK:(c + 1) * SGU_CHUNK, k * LANE:(k + 1) * LANE]
            lo = jnp.where(lane < half, blk, 0.0)
            cols.append(jnp.concatenate([lo, blk - lo], axis=0))
        rhs = jnp.concatenate(cols, axis=1).astype(BF16)
        mixed = jnp.dot(w_ref[k], rhs, preferred_element_type=F32)
        for c in range(n_chunk):
            rows = slice(c * SGU_CHUNK, (c + 1) * SGU_CHUNK)
            m = mixed[:, c * LANE:(c + 1) * LANE] + bias_ref[:, k * LANE:(k + 1) * LANE]
            o_ref[rows, k * LANE:(k + 1) * LANE] = (u[rows, k * LANE:(k + 1) * LANE] * m).astype(o_ref.dtype)


def sgu(z, ln_g, ln_b, w_pair, bias):
    r = z.shape[0]
    return pl.pallas_call(
        _sgu_kernel, out_shape=jax.ShapeDtypeStruct((r, BR_W), BF16), grid=(r // TM,),
        in_specs=[pl.BlockSpec((TM, 2 * BR_W), lambda i: (i, 0)), _full((1, BR_W)), _full((1, BR_W)),
                  _wspec(w_pair), _wspec(bias)],
        out_specs=pl.BlockSpec((TM, BR_W), lambda i: (i, 0)),
        compiler_params=_cp("parallel"), name="sgu",
    )(z, ln_g, ln_b, _warg(w_pair), _warg(bias))


def _ssd_prep_kernel(cur_ref, prev_ref, next_ref, dt_ref, w_ref, b_ref, dtb_ref, alane_ref, esel_ref, tri_ref,
                     xs_ref, bc_ref, ac_ref, act_ref, xt_ref, *, seq_len):
    i = pl.program_id(0)
    x = cur_ref[:, BR_W:].astype(F32)
    tm = x.shape[0]
    row = lax.broadcasted_iota(jnp.int32, (tm, 1), 0)
    pos = lax.rem(row + i * tm, seq_len)
    prev_row = prev_ref[SUBLANE - 1:SUBLANE, BR_W:].astype(F32)
    next_row = next_ref[0:1, BR_W:].astype(F32)
    x_prev = jnp.where(row == 0, prev_row, pltpu.roll(x, 1, axis=0))
    x_prev = jnp.where(pos == 0, 0.0, x_prev)
    x_next = jnp.where(row == tm - 1, next_row, pltpu.roll(x, tm - 1, axis=0))
    x_next = jnp.where(pos == seq_len - 1, 0.0, x_next)
    y = _silu(w_ref[0:1, :] * x_prev + w_ref[1:2, :] * x + w_ref[2:3, :] * x_next + b_ref[...])
    xs = y[:, :BR_W]
    xs_ref[...] = xs
    bc_ref[...] = y[:, BR_W:].astype(bc_ref.dtype)
    t = dt_ref[...] + dtb_ref[...]
    sp = jnp.maximum(t, 0.0) + jnp.log1p(jnp.exp(-jnp.abs(t)))
    lane = lax.broadcasted_iota(jnp.int32, t.shape, 1)
    d_a = sp * alane_ref[...]
    d_a = (jnp.where(lane < SSD_HEADS, d_a, 0.0),
           jnp.where(lane < SSD_HEADS, pltpu.roll(d_a, LANE - SSD_HEADS, axis=1), 0.0))
    for d in range(2):
        for c in range(tm // SSD_T):
            rows = slice(c * SSD_T, (c + 1) * SSD_T)
            a_cum = jnp.dot(tri_ref[d], d_a[d][rows], preferred_element_type=F32, precision=HIGHEST)
            ac_ref[d, rows, :] = a_cum
            act_ref[d, c] = a_cum.T[:SUBLANE, :]
    sp_hi = sp.astype(BF16)
    sp_lo = (sp - sp_hi.astype(F32)).astype(BF16)
    for d in range(2):
        dt_x = (jnp.dot(sp_hi, esel_ref[d], preferred_element_type=F32)
                + jnp.dot(sp_lo, esel_ref[d], preferred_element_type=F32))
        xt_ref[d] = (xs * dt_x).astype(xt_ref.dtype)


def ssd_prep(p_ssd, dt_raw, seq_len, conv_w, conv_b, dt_bias, a_lane, esel, tri):
    r, w = p_ssd.shape
    nb = r // SUBLANE
    per = TM // SUBLANE
    cpt = TM // SSD_T
    return pl.pallas_call(
        functools.partial(_ssd_prep_kernel, seq_len=seq_len),
        out_shape=[jax.ShapeDtypeStruct((r, BR_W), F32), jax.ShapeDtypeStruct((r, 2 * SSD_GN), BF16),
                   jax.ShapeDtypeStruct((2, r, LANE), F32),
                   jax.ShapeDtypeStruct((2, r // SSD_T, SUBLANE, SSD_T), F32),
                   jax.ShapeDtypeStruct((2, r, BR_W), BF16)],
        grid=(r // TM,),
        in_specs=[pl.BlockSpec((TM, w), lambda i: (i, 0)),
                  pl.BlockSpec((SUBLANE, w), lambda i: (jnp.maximum(i * per - 1, 0), 0)),
                  pl.BlockSpec((SUBLANE, w), lambda i: (jnp.minimum((i + 1) * per, nb - 1), 0)),
                  pl.BlockSpec((TM, LANE), lambda i: (i, 0)),
                  _full(conv_w.shape), _full(conv_b.shape), _full(dt_bias.shape), _full(a_lane.shape),
                  _full(esel.shape), _full(tri.shape)],
        out_specs=[pl.BlockSpec((TM, BR_W), lambda i: (i, 0)), pl.BlockSpec((TM, 2 * SSD_GN), lambda i: (i, 0)),
                   pl.BlockSpec((2, TM, LANE), lambda i: (0, i, 0)),
                   pl.BlockSpec((2, cpt, SUBLANE, SSD_T), lambda i: (0, i, 0, 0)),
                   pl.BlockSpec((2, TM, BR_W), lambda i: (0, i, 0))],
        compiler_params=_cp("parallel"), name="ssd_prep",
    )(p_ssd, p_ssd, p_ssd, dt_raw, conv_w, conv_b, dt_bias, a_lane, esel, tri)


def _ssd_scan_kernel(xt_ref, bc_ref, ac_ref, act_ref, tri_ref, h0_ref, y_ref, hfin_ref, st_ref, *, n_chunk, bsz):
    d = pl.program_id(0)
    j = pl.program_id(1)
    T = SSD_T

    @pl.when(j == 0)
    def _():
        st_ref[...] = h0_ref[...]

    tri = tri_ref[...]
    mask = tri > 0.5
    rep = SSD_HEADS // SSD_GROUPS
    low_lane = lax.broadcasted_iota(jnp.int32, (1, LANE), 1) < SSD_HEAD_DIM
    group_row = [(lax.broadcasted_iota(jnp.int32, (SSD_GN, 1), 0) // SSD_STATE) == g for g in range(SSD_GROUPS)]

    def chunk(ci, carry):
        c = ci + d * (n_chunk - 1 - 2 * ci)
        r0 = pl.multiple_of(c * T, T)
        for b in range(bsz):
            a_cum = ac_ref[b, pl.ds(r0, T), :]
            a_cum_t = act_ref[b, c]
            total = jnp.where(d == 0, a_cum[T - 1:T, :], a_cum[0:1, :])
            bc = bc_ref[b, pl.ds(r0, T), :]
            bm_t = bc[:, :SSD_GN].astype(F32).T
            cm = bc[:, SSD_GN:]
            b_grp = [jnp.where(group_row[g], bm_t, 0.0) for g in range(SSD_GROUPS)]
            scores = [jnp.dot(cm, b_grp[g].astype(BF16), preferred_element_type=F32)
                      for g in range(SSD_GROUPS)]
            for k in range(SSD_HEADS // 2):
                x_pair = xt_ref[b, pl.ds(r0, T), k * LANE:(k + 1) * LANE]
                y_d, s_n, e_col, e_tot = [], [], [], []
                for h in (2 * k, 2 * k + 1):
                    g = h // rep
                    col = a_cum[:, h:h + 1]
                    rowv = a_cum_t[h:h + 1, :]
                    tot = total[:, h:h + 1]
                    decay = jnp.exp(jnp.where(mask, col - rowv, -1e30))
                    y_d.append(jnp.dot((scores[g] * decay).astype(BF16), x_pair, preferred_element_type=F32))
                    bw = (b_grp[g] * jnp.exp(tot - rowv)).astype(BF16)
                    s_n.append(jnp.dot(bw, x_pair, preferred_element_type=F32))
                    e_col.append(jnp.exp(col))
                    e_tot.append(jnp.exp(tot))
                s_old = st_ref[b, k]
                y_off = jnp.dot(cm, s_old.astype(BF16), preferred_element_type=F32)
                y = (jnp.where(low_lane, y_d[0], y_d[1])
                     + y_off * jnp.where(low_lane, e_col[0], e_col[1]))
                y_ref[b, pl.ds(r0, T), k * LANE:(k + 1) * LANE] = y.astype(y_ref.dtype)
                st_ref[b, k] = (s_old * jnp.where(low_lane, e_tot[0], e_tot[1])
                                + jnp.where(low_lane, s_n[0], s_n[1]))
        return carry

    lax.fori_loop(0, n_chunk, chunk, 0, unroll=True)
    hfin_ref[...] = st_ref[...]


def ssd_scan(xt, bc, a_cum, a_cum_t, tri, bsz, h0):
    r = bc.shape[0]
    seq_len = r // bsz
    ts = min(seq_len, TM)
    nt = seq_len // ts

    def tile(d, j):
        return j + d * (nt - 1 - 2 * j)

    st_shape = (bsz, SSD_HEADS // 2, SSD_GN, LANE)
    st = pl.BlockSpec((None,) + st_shape, lambda d, j: (d, 0, 0, 0, 0))
    y, hfin = pl.pallas_call(
        functools.partial(_ssd_scan_kernel, n_chunk=ts // SSD_T, bsz=bsz),
        out_shape=[jax.ShapeDtypeStruct((2, bsz, seq_len, BR_W), BF16),
                   jax.ShapeDtypeStruct((2,) + st_shape, F32)],
        grid=(2, nt),
        in_specs=[pl.BlockSpec((None, bsz, ts, BR_W), lambda d, j: (d, 0, tile(d, j), 0)),
                  pl.BlockSpec((bsz, ts, 2 * SSD_GN), lambda d, j: (0, tile(d, j), 0)),
                  pl.BlockSpec((None, bsz, ts, LANE), lambda d, j: (d, 0, tile(d, j), 0)),
                  pl.BlockSpec((None, bsz, ts // SSD_T, SUBLANE, SSD_T), lambda d, j: (d, 0, tile(d, j), 0, 0)),
                  pl.BlockSpec((None, SSD_T, SSD_T), lambda d, j: (d, 0, 0)), st],
        out_specs=[pl.BlockSpec((None, bsz, ts, BR_W), lambda d, j: (d, 0, tile(d, j), 0)), st],
        scratch_shapes=[pltpu.VMEM(st_shape, F32)],
        compiler_params=_cp("arbitrary", "arbitrary"), name="ssd_scan",
    )(xt.reshape(2, bsz, seq_len, BR_W), bc.reshape(bsz, seq_len, 2 * SSD_GN),
      a_cum.reshape(2, bsz, seq_len, LANE), a_cum_t.reshape(2, bsz, seq_len // SSD_T, SUBLANE, SSD_T), tri, h0)
    return y.reshape(2, r, BR_W), hfin


def _rope_table_kernel(frow_ref, fcol_ref, cos_ref, sin_ref):
    tr = cos_ref.shape[0]
    pos = (lax.broadcasted_iota(jnp.int32, (tr, 1), 0) + pl.program_id(0) * tr).astype(F32)
    pos_row = jnp.floor((pos + 0.5) * (1.0 / GRID_W))
    pos_col = pos - pos_row * GRID_W
    ang = pos_row * frow_ref[...] + pos_col * fcol_ref[...]
    lane = lax.broadcasted_iota(jnp.int32, ang.shape, 1)
    quarter = MLA_ROPE // 2
    first = (lane >= MLA_NOPE) & (lane < MLA_NOPE + quarter)
    second = (lane >= MLA_NOPE + quarter) & (lane < MLA_QK)
    c, s = jnp.cos(ang), jnp.sin(ang)
    cos_ref[...] = jnp.where(first | second, c, jnp.where(lane < MLA_NOPE, 1.0, 0.0))
    sin_ref[...] = jnp.where(first, -s, jnp.where(second, s, 0.0))


def _rope_partner(v):
    quarter = MLA_ROPE // 2
    src = np.arange(LANE)
    src[MLA_NOPE:MLA_NOPE + quarter] += quarter
    src[MLA_NOPE + quarter:MLA_QK] -= quarter
    valid = (np.arange(LANE) >= MLA_NOPE) & (np.arange(LANE) < MLA_QK)
    return jnp.where(valid, v[..., src], 0)


def rope_tables(seq_len):
    pairs = MLA_ROPE // 4
    inv_freq = ROPE_BASE ** (-jnp.arange(pairs, dtype=F32) / pairs)
    zeros = lambda n: jnp.zeros((n,), F32)
    half = jnp.concatenate([inv_freq, zeros(pairs)])
    f_row = jnp.concatenate([zeros(MLA_NOPE), half, half, zeros(LANE - MLA_QK)]).reshape(1, LANE)
    half = jnp.concatenate([zeros(pairs), inv_freq])
    f_col = jnp.concatenate([zeros(MLA_NOPE), half, half, zeros(LANE - MLA_QK)]).reshape(1, LANE)
    tr = min(seq_len, 1024)
    spec = pl.BlockSpec((tr, LANE), lambda i: (i, 0))
    return pl.pallas_call(
        _rope_table_kernel, out_shape=[jax.ShapeDtypeStruct((seq_len, LANE), F32)] * 2,
        grid=(seq_len // tr,), in_specs=[_full((1, LANE))] * 2, out_specs=[spec] * 2,
        compiler_params=_cp("parallel"), name="rope_tables",
    )(f_row, f_col)


def _mla_prep_kernel(*refs, rope):
    if rope:
        (p_ref, gq_ref, gkv_ref, wq_ref, wkv_ref, nq_ref, nk_ref, wqs_ref, nqs_ref, nks_ref, cos_ref, sin_ref,
         q_ref, k_ref, v_ref) = refs
    else:
        p_ref, gq_ref, gkv_ref, wq_ref, wkv_ref, nq_ref, nk_ref, q_ref, k_ref, v_ref = refs
    p = p_ref[...].astype(F32)
    cq = (_rms(p[:, :MLA_Q_LORA], MLA_Q_LORA) * gq_ref[...]).astype(BF16)
    ckv = (_rms(p[:, MLA_Q_LORA:MLA_Q_LORA + MLA_KV_LORA], MLA_KV_LORA) * gkv_ref[...]).astype(BF16)
    kr = p[:, MLA_Q_LORA + MLA_KV_LORA:MLA_Q_LORA + MLA_KV_LORA + LANE]
    q_all = jnp.dot(cq, wq_ref[...], preferred_element_type=F32)
    kv_all = jnp.dot(ckv, wkv_ref[...], preferred_element_type=F32)
    lane = lax.broadcasted_iota(jnp.int32, (1, LANE), 1)
    one_col = jnp.where(lane == MLA_V, 1.0, 0.0)
    scale = MLA_QK ** -0.5 * LOG2E
    if rope:
        qs_all = jnp.dot(cq, wqs_ref[...], preferred_element_type=F32)
        q_cos, q_sin = nq_ref[...] * cos_ref[...], nqs_ref[...] * sin_ref[...]
        k_cos = nk_ref[...] * cos_ref[...]
        k_part = p[:, MLA_Q_LORA + MLA_KV_LORA + LANE:] * (nks_ref[...] * sin_ref[...])

    def inv_rms(t):
        return lax.rsqrt(jnp.sum(t * t, axis=-1, keepdims=True) * (1.0 / MLA_QK) + 1e-6)

    for h in range(MLA_HEADS):
        q = q_all[:, h * LANE:(h + 1) * LANE]
        k = kv_all[:, 2 * h * LANE:(2 * h + 1) * LANE] + kr
        if rope:
            q_out = (q * q_cos + qs_all[:, h * LANE:(h + 1) * LANE] * q_sin) * (inv_rms(q) * scale)
            k_out = (k * k_cos + k_part) * inv_rms(k)
        else:
            q_out = q * nq_ref[...] * (inv_rms(q) * scale)
            k_out = k * nk_ref[...] * inv_rms(k)
        q_ref[h] = q_out.astype(q_ref.dtype)
        k_ref[h] = k_out.astype(k_ref.dtype)
        v_ref[h] = (kv_all[:, (2 * h + 1) * LANE:(2 * h + 2) * LANE] + one_col).astype(v_ref.dtype)


def mla_prep(p_mla, bsz, gq, gkv, wq, wkv, nq, nk, rope_args):
    r, w = p_mla.shape
    seq_len = r // bsz
    tr = min(seq_len, TM)
    nt = seq_len // tr
    rope = rope_args is not None
    in_specs = [pl.BlockSpec((tr, w), lambda b, i: (b * nt + i, 0)), _full(gq.shape), _full(gkv.shape),
                _wspec(wq), _wspec(wkv), _full(nq.shape), _full(nk.shape)]
    args = [p_mla, gq, gkv, _warg(wq), _warg(wkv), nq, nk]
    if rope:
        in_specs += [_wspec(a) for a in rope_args[:3]]
        in_specs += [pl.BlockSpec((tr, LANE), lambda b, i: (i, 0))] * 2
        args += [_warg(a) for a in rope_args]
    head = pl.BlockSpec((None, MLA_HEADS, tr, LANE), lambda b, i: (b, 0, i, 0))
    return pl.pallas_call(
        functools.partial(_mla_prep_kernel, rope=rope),
        out_shape=[jax.ShapeDtypeStruct((bsz, MLA_HEADS, seq_len, LANE), BF16)] * 3,
        grid=(bsz, nt), in_specs=in_specs, out_specs=[head] * 3,
        compiler_params=_cp("parallel", "parallel"), name="mla_prep",
    )(*args)


def _attn_kernel(*refs, n_kv):
    bound_ref, q_ref = refs[0], refs[1]
    kv_refs = refs[2:2 + 2 * n_kv]
    o_ref = refs[2 + 2 * n_kv]
    m_ref = refs[3 + 2 * n_kv]
    scores = [[lax.dot_general(q_ref[hh], kv_refs[2 * i][hh], (((1,), (1,)), ((), ())),
                               preferred_element_type=F32) for i in range(n_kv)] for hh in range(2)]

    def row_max(first_set):
        for hh in range(2):
            m = scores[hh][n_kv - 1].max(axis=-1, keepdims=True)
            for si in scores[hh][first_set:n_kv - 1]:
                m = jnp.maximum(m, si.max(axis=-1, keepdims=True))
            m_ref[hh] = m

    if n_kv > 1:
        fast = bound_ref[0] <= ATTN_FAST_BOUND

        @pl.when(fast)
        def _():
            row_max(n_kv - 1)

        @pl.when(jnp.logical_not(fast))
        def _():
            row_max(0)
    else:
        row_max(0)

    for hh in range(2):
        m = m_ref[hh]
        acc = None
        for i, si in enumerate(scores[hh]):
            pv = jnp.dot(jnp.exp2((si - m).astype(BF16)), kv_refs[2 * i + 1][hh], preferred_element_type=F32)
            acc = pv if acc is None else acc + pv
        o = acc[:, :MLA_V] / acc[:, MLA_V:MLA_V + 1]
        o_ref[:, hh * MLA_V:(hh + 1) * MLA_V] = o.astype(o_ref.dtype)


def attention(q, kvs, score_bound):
    bsz, nh, lq, _ = q.shape
    tq = min(lq, TQ)
    nq = lq // tq
    in_specs = [pl.BlockSpec(memory_space=pltpu.SMEM),
                pl.BlockSpec((None, 2, tq, LANE), lambda b, hp, i: (b, hp, i, 0))]
    args = [score_bound, q]
    for k, v in kvs:
        spec = pl.BlockSpec((None, 2, k.shape[2], LANE), lambda b, hp, i: (b, hp, 0, 0))
        in_specs += [spec, spec]
        args += [k, v]
    return pl.pallas_call(
        functools.partial(_attn_kernel, n_kv=len(kvs)),
        out_shape=jax.ShapeDtypeStruct((bsz * lq, nh * MLA_V), BF16),
        grid=(bsz, nh // 2, nq), in_specs=in_specs,
        out_specs=pl.BlockSpec((tq, 2 * MLA_V), lambda b, hp, i: (b * nq + i, hp)),
        scratch_shapes=[pltpu.VMEM((2, tq, 1), F32)],
        compiler_params=_cp("parallel", "parallel", "arbitrary"), name="attention",
    )(*args)


def _merge_kernel(x_ref, h_ref, u_ref, s5f_ref, s5b_ref, s5d_ref, wglu_ref, b_ref,
                  xs_ref, ssdf_ref, ssdb_ref, z_ref, ssdd_ref, ssdg_ref, d_ref,
                  g1_ref, wg_ref, wb_ref, wo_ref,
                  n2_ref, sh2_ref, sc2_ref, g2_ref, wi_ref, wfo_ref, o_ref, *, n_split):
    y = s5d_ref[...] * u_ref[...].astype(F32) + s5f_ref[...].astype(F32) + s5b_ref[...].astype(F32)
    g = jax.nn.gelu(y)
    a = (g * jax.nn.sigmoid(jnp.dot(g.astype(BF16), wglu_ref[...], preferred_element_type=F32))).astype(BF16)
    y = ssdd_ref[...] * xs_ref[...] + ssdf_ref[...].astype(F32) + ssdb_ref[...].astype(F32)
    c = (_rms(y * _silu(z_ref[...].astype(F32)), BR_W) * ssdg_ref[...]).astype(BF16)
    h = h_ref[...]
    dm = x_ref.shape[-1]
    merged = None
    for i, br in enumerate((a, b_ref[...], c, d_ref[...])):
        gate = jax.nn.sigmoid(jnp.dot(h, wg_ref[:, i * dm:(i + 1) * dm], preferred_element_type=F32))
        term = gate * jnp.dot(br, wb_ref[i], preferred_element_type=F32)
        merged = term if merged is None else merged + term
    mix = jnp.dot(merged.astype(BF16), wo_ref[...], preferred_element_type=F32)
    x = x_ref[...] + g1_ref[...] * mix

    y = _rms(x, dm) * n2_ref[...]
    hb = (y * (1.0 + sc2_ref[...]) + sh2_ref[...]).astype(BF16)
    hid = wfo_ref.shape[0]
    step = hid // n_split
    acc = None
    for c in range(n_split):
        gate = jnp.dot(hb, wi_ref[:, c * step:(c + 1) * step], preferred_element_type=F32)
        up = jnp.dot(hb, wi_ref[:, hid + c * step:hid + (c + 1) * step], preferred_element_type=F32)
        part = jnp.dot((_silu(gate) * up).astype(BF16), wfo_ref[c * step:(c + 1) * step, :],
                       preferred_element_type=F32)
        acc = part if acc is None else acc + part
    o_ref[...] = x + g2_ref[...] * acc


def merge_ffn(x, h, s5_in, b, ssd_in, d, mod4, l, midx, wg, wb, wo, norm2, wi, wfo):
    r, dm = x.shape
    n_split = 11 if wfo.shape[0] % (11 * LANE) == 0 else 1
    row = pl.BlockSpec((TM, dm), lambda i: (i, 0))
    br = pl.BlockSpec((TM, BR_W), lambda i: (i, 0))
    fwd = pl.BlockSpec((None, TM, BR_W), lambda i: (0, i, 0))
    bwd = pl.BlockSpec((None, TM, BR_W), lambda i: (1, i, 0))
    vec = _full((1, BR_W))
    u, y_s5, s5_d, w_glu = s5_in
    xs, y_ssd, p_ssd, ssd_d, ssd_g = ssd_in
    return pl.pallas_call(
        functools.partial(_merge_kernel, n_split=n_split),
        out_shape=jax.ShapeDtypeStruct((r, dm), F32), grid=(r // TM,),
        in_specs=[row, row, br, fwd, bwd, vec, _wspec(w_glu), br,
                  br, fwd, bwd, br, vec, vec, br,
                  _mod_spec(l, midx, 2, dm), _wspec(wg), _wspec(wb), _wspec(wo),
                  _full((1, dm)), _mod_spec(l, midx, 3, dm), _mod_spec(l, midx, 4, dm),
                  _mod_spec(l, midx, 5, dm), _wspec(wi), _wspec(wfo)],
        out_specs=row, compiler_params=_cp("parallel"), name="merge_ffn",
    )(x, h, u, y_s5, y_s5, s5_d, _warg(w_glu), b, xs, y_ssd, y_ssd, p_ssd, ssd_d, ssd_g, d, mod4,
      _warg(wg), _warg(wb), _warg(wo), norm2, mod4, mod4, mod4, _warg(wi), _warg(wfo))


def _pad_cols(w, n):
    return jnp.pad(w, ((0, 0),) * (w.ndim - 1) + ((0, n - w.shape[-1]),))


def kernel(x, c, ctx, c_ctx, w_ada, b_ada, norm1_g, norm2_g, w_in, s5_a_re, s5_a_im, s5_b_re, s5_b_im, s5_c_re, s5_c_im, s5_log_dt, s5_d, s5_w_glu, sgu_ln_g, sgu_ln_b, sgu_w_s, sgu_b_s, ssd_conv_w, ssd_conv_b, ssd_a_log, ssd_dt_bias, ssd_d, ssd_norm_g, mla_q_a_norm, mla_w_uq, mla_kv_a_norm, mla_w_ukv, mla_q_norm, mla_k_norm, w_branch, w_out, w_ffn_in, w_ffn_out):
    bsz, seq, dm = x.shape
    lc = ctx.shape[1]
    depth = w_ada.shape[0]
    assert seq % TM == 0 and (bsz * lc) % TM == 0 and seq % (S5_NSEG * SUBLANE) == 0
    assert lc % SSD_T == 0 and lc % (S5_NSEG * SUBLANE) == 0 and bsz + 1 <= 8

    cc8 = jnp.zeros((8, dm), F32).at[:bsz].set(c.astype(F32)).at[bsz].set(c_ctx.astype(F32))
    mod4 = ada_table(cc8, w_ada.astype(F32), b_ada.astype(F32)).reshape(depth, 8, 1, 6 * dm)
    lat_tiles = seq // TM
    midx_lat = lambda i: i // lat_tiles
    midx_ctx = lambda i: bsz

    off = np.cumsum([0, BR_W, 2 * BR_W, BR_W + SSD_CONV_CH + 2 * SSD_HEADS,
                     MLA_Q_LORA + MLA_KV_LORA + MLA_ROPE, N_BRANCH * dm])
    w_s5 = w_in[:, :, off[0]:off[1]].astype(BF16)
    w_sgu = w_in[:, :, off[1]:off[2]].astype(BF16)
    w_ssd = w_in[:, :, off[2]:off[2] + BR_W + SSD_CONV_CH].astype(BF16)
    w_dt = _pad_cols(w_in[:, :, off[2] + BR_W + SSD_CONV_CH:off[3]], LANE).astype(BF16)
    w_mla_main = w_in[:, :, off[3]:off[3] + MLA_Q_LORA + MLA_KV_LORA]
    w_kr = w_in[:, :, off[3] + MLA_Q_LORA + MLA_KV_LORA:off[4]]
    zeros = lambda n: jnp.zeros((depth, dm, n), w_in.dtype)
    kr_block = jnp.concatenate([zeros(MLA_NOPE), w_kr, zeros(LANE - MLA_QK)], axis=-1)
    w_mla = jnp.concatenate([w_mla_main, kr_block, _rope_partner(kr_block)], axis=-1).astype(BF16)
    w_gate = w_in[:, :, off[4]:off[5]].astype(BF16)

    lam_re, lam_im, bb_re, bb_im = s5_discretise(s5_a_re, s5_a_im, s5_b_re, s5_b_im, s5_log_dt)
    lam, bblk, cblk = s5_pack(lam_re, lam_im, bb_re, bb_im, s5_c_re, s5_c_im)
    s5_wg = s5_w_glu.astype(BF16)

    sgu_w = sgu_w_s.reshape(depth, SGU_HEADS // 2, 2, SGU_CHUNK, SGU_CHUNK)
    sgu_w = jnp.concatenate([sgu_w[:, :, 0], sgu_w[:, :, 1]], axis=-1).astype(BF16)
    sgu_b = jnp.repeat(jnp.swapaxes(sgu_b_s, 1, 2), BR_W // SGU_HEADS, axis=-1).astype(F32)

    a_neg = -jnp.exp(ssd_a_log.astype(F32))
    a_lane = _pad_cols(a_neg.reshape(depth, 1, 2 * SSD_HEADS), LANE)
    head_of_lane = jnp.arange(LANE)[None, :, None] - SSD_HEADS * jnp.arange(2)[:, None, None]
    col_head = jnp.arange(BR_W) // SSD_HEAD_DIM
    esel = ((head_of_lane == col_head[None, None, :]) & (head_of_lane >= 0)).astype(BF16)
    ti = jnp.arange(SSD_T)
    tri = jnp.stack([ti[None, :] <= ti[:, None], ti[None, :] >= ti[:, None]]).astype(F32)
    dt_bias = _pad_cols(ssd_dt_bias.astype(F32).reshape(depth, 1, 2 * SSD_HEADS), LANE)
    ssd_dskip = jnp.repeat(ssd_d.astype(F32), SSD_HEAD_DIM, axis=-1)[:, None, :]

    wq = mla_w_uq.reshape(depth, MLA_Q_LORA, MLA_HEADS, MLA_QK)
    wq = _pad_cols(wq, LANE)
    wq_sw = _rope_partner(wq).reshape(depth, MLA_Q_LORA, MLA_HEADS * LANE).astype(BF16)
    wq = wq.reshape(depth, MLA_Q_LORA, MLA_HEADS * LANE).astype(BF16)
    wkv = mla_w_ukv.reshape(depth, MLA_KV_LORA, MLA_HEADS, 2, MLA_NOPE)
    wkv = _pad_cols(wkv, LANE).reshape(depth, MLA_KV_LORA, MLA_HEADS * 2 * LANE).astype(BF16)
    nq = _pad_cols(mla_q_norm.astype(F32), LANE)[:, None, :]
    nk = _pad_cols(mla_k_norm.astype(F32), LANE)[:, None, :]
    nq_sw, nk_sw = _rope_partner(nq), _rope_partner(nk)
    rope_cos, rope_sin = rope_tables(seq)
    score_bound = (MLA_QK ** 0.5 * LOG2E * jnp.max(jnp.abs(mla_q_norm.astype(F32)), axis=-1)
                   * jnp.max(jnp.abs(mla_k_norm.astype(F32)), axis=-1))

    wb = w_branch.astype(BF16)
    wo = w_out.astype(BF16)
    wfi = w_ffn_in.astype(BF16)
    wfo = w_ffn_out.astype(BF16)

    row = lambda v, l: v[l].astype(F32).reshape(1, -1)
    x_lat = x.astype(F32).reshape(bsz * seq, dm)
    x_ctx = ctx.astype(F32).reshape(bsz * lc, dm)
    s5_zero = jnp.zeros((bsz, 2, S5_JB, 1, 2 * S5_BW), F32)
    ssd_zero = jnp.zeros((2, bsz, SSD_HEADS // 2, SSD_GN, LANE), F32)

    for l in range(depth):
        need_ctx = l < depth - 1
        lay = lambda w: _Layer(w, l)
        ws = tuple(lay(w) for w in (w_s5, w_sgu, w_ssd, w_dt, w_mla))
        w_merge = (lay(w_gate), lay(wb), lay(wo))
        w_ffn = (lay(wfi), lay(wfo))
        dts = (BF16, BF16, BF16, F32, BF16)
        g1 = row(norm1_g, l)
        h_l, u_l, z_l, p_l, dtr_l, m_l = in_proj(x_lat, g1, mod4, l, midx_lat, ws, dts)
        h_c, u_c, z_c, p_c, dtr_c, m_c = in_proj(x_ctx, g1, mod4, l, midx_ctx, ws, dts)

        y_c, s5_h = s5_scan(u_c, bblk, lam, cblk, l, bsz, s5_zero)
        y_l, _ = s5_scan(u_l, bblk, lam, cblk, l, bsz, s5_h)
        s5_tail = (row(s5_d, l), lay(s5_wg))
        b_l = sgu(z_l, row(sgu_ln_g, l), row(sgu_ln_b, l), lay(sgu_w), lay(sgu_b))
        conv_w, conv_b = ssd_conv_w[l].astype(F32), row(ssd_conv_b, l)
        ssd_w = (conv_w, conv_b, dt_bias[l], a_lane[l], esel, tri)
        xs_c, bc_c, ac_c, act_c, xt_c = ssd_prep(p_c, dtr_c, lc, *ssd_w)
        xs_l, bc_l, ac_l, act_l, xt_l = ssd_prep(p_l, dtr_l, seq, *ssd_w)
        yc_c, ssd_h = ssd_scan(xt_c, bc_c, ac_c, act_c, tri, bsz, ssd_zero)
        yc_l, _ = ssd_scan(xt_l, bc_l, ac_l, act_l, tri, bsz, ssd_h)
        ssd_tail = (ssd_dskip[l], row(ssd_norm_g, l))
        mla_w = (row(mla_q_a_norm, l), row(mla_kv_a_norm, l), lay(wq), lay(wkv), nq[l], nk[l])
        q_c, k_c, v_c = mla_prep(m_c, bsz, *mla_w, None)
        q_l, k_l, v_l = mla_prep(m_l, bsz, *mla_w, (lay(wq_sw), nq_sw[l], nk_sw[l], rope_cos, rope_sin))
        d_l = attention(q_l, [(k_l, v_l), (k_c, v_c)], score_bound[l:l + 1])

        x_lat = merge_ffn(x_lat, h_l, (u_l, y_l) + s5_tail, b_l, (xs_l, yc_l, p_l) + ssd_tail, d_l,
                          mod4, l, midx_lat, *w_merge, row(norm2_g, l), *w_ffn)
        if need_ctx:
            b_c = sgu(z_c, row(sgu_ln_g, l), row(sgu_ln_b, l), lay(sgu_w), lay(sgu_b))
            d_c = attention(q_c, [(k_c, v_c)], score_bound[l:l + 1])
            x_ctx = merge_ffn(x_ctx, h_c, (u_c, y_c) + s5_tail, b_c, (xs_c, yc_c, p_c) + ssd_tail, d_c,
                              mod4, l, midx_ctx, *w_merge, row(norm2_g, l), *w_ffn)
    return x_lat.reshape(bsz, seq, dm).astype(x.dtype)
```

```python
import functools

import jax
import jax.numpy as jnp
import numpy as np
from jax import lax
from jax.experimental import pallas as pl
from jax.experimental.pallas import tpu as pltpu

F32 = jnp.float32
BF16 = jnp.bfloat16
HIGHEST = lax.Precision.HIGHEST

LANE = 128
SUBLANE = 8
VMEM_LIMIT = 56 * 1024 * 1024

GRID_W = 64
BR_W = 384
S5_GROUP = 16
S5_GROUPS = BR_W // S5_GROUP
S5_STATE = 64
S5_NSEG = SUBLANE
S5_JB = BR_W // LANE
S5_BW = (LANE // S5_GROUP) * S5_STATE
SGU_CHUNK = 128
SGU_HEADS = 6
SSD_HEADS = 6
SSD_HEAD_DIM = 64
SSD_GROUPS = 2
SSD_STATE = 64
SSD_GN = SSD_GROUPS * SSD_STATE
SSD_CONV_CH = BR_W + 2 * SSD_GN
SSD_T = 128
MLA_HEADS = 6
MLA_NOPE = 64
MLA_ROPE = 32
MLA_V = 64
MLA_QK = MLA_NOPE + MLA_ROPE
MLA_Q_LORA = 384
MLA_KV_LORA = 256
ROPE_BASE = 10000.0
LOG2E = 1.4426950408889634
N_BRANCH = 4

TM = 512
TQ = 256
ATTN_FAST_BOUND = 30.0


def _cp(*sem):
    return pltpu.CompilerParams(dimension_semantics=sem, vmem_limit_bytes=VMEM_LIMIT)


def _full(shape):
    n = len(shape)
    return pl.BlockSpec(shape, lambda *_: (0,) * n)


class _Layer:
    def __init__(self, arr, l):
        self.arr, self.l = arr, l

    @property
    def shape(self):
        return self.arr.shape[1:]


def _wspec(w):
    if not isinstance(w, _Layer):
        return _full(w.shape)
    n, l = len(w.shape), w.l
    return pl.BlockSpec((None,) + tuple(w.shape), lambda *_: (l,) + (0,) * n, pipeline_mode=pl.Buffered(1))


def _warg(w):
    return w.arr if isinstance(w, _Layer) else w


def _silu(x):
    return x * jax.nn.sigmoid(x)


def _rms(x, n, eps=1e-6):
    return x * lax.rsqrt(jnp.sum(x * x, axis=-1, keepdims=True) * (1.0 / n) + eps)


def _ada_kernel(cc_ref, w_ref, b_ref, o_ref):
    s = _silu(cc_ref[...])
    o_ref[...] = jnp.dot(s, w_ref[...], preferred_element_type=F32, precision=HIGHEST) + b_ref[...]


def ada_table(cc8, w_ada, b_ada):
    depth, d, n = w_ada.shape
    tn = n // 4
    return pl.pallas_call(
        _ada_kernel,
        out_shape=jax.ShapeDtypeStruct((depth, 8, n), F32),
        grid=(depth, n // tn),
        in_specs=[pl.BlockSpec((8, d), lambda l, j: (0, 0)),
                  pl.BlockSpec((None, d, tn), lambda l, j: (l, 0, j)),
                  pl.BlockSpec((None, 1, tn), lambda l, j: (l, 0, j))],
        out_specs=pl.BlockSpec((None, 8, tn), lambda l, j: (l, 0, j)),
        compiler_params=_cp("arbitrary", "arbitrary"),
        name="ada_table",
    )(cc8, w_ada, b_ada.reshape(depth, 1, n))


def _mod_spec(l, midx, col, d):
    return pl.BlockSpec((None, None, 1, d), lambda i: (l, midx(i), 0, col))


def _sgu_body(z, g_ref, b_ref, w_ref, bias_ref, o_ref):
    z = jax.nn.gelu(z)
    u, v = z[:, :BR_W], z[:, BR_W:]
    mu = jnp.mean(v, axis=-1, keepdims=True)
    vc = v - mu
    vn = vc * lax.rsqrt(jnp.mean(vc * vc, axis=-1, keepdims=True) + 1e-5) * g_ref[...] + b_ref[...]
    n_chunk = z.shape[0] // SGU_CHUNK
    half = LANE // 2
    lane = lax.broadcasted_iota(jnp.int32, (SGU_CHUNK, LANE), 1)
    for k in range(BR_W // LANE):
        cols = []
        for c in range(n_chunk):
            blk = vn[c * SGU_CHUNK:(c + 1) * SGU_CHUNK, k * LANE:(k + 1) * LANE]
            lo = jnp.where(lane < half, blk, 0.0)
            cols.append(jnp.concatenate([lo, blk - lo], axis=0))
        rhs = jnp.concatenate(cols, axis=1).astype(BF16)
        mixed = jnp.dot(w_ref[k], rhs, preferred_element_type=F32)
        for c in range(n_chunk):
            rows = slice(c * SGU_CHUNK, (c + 1) * SGU_CHUNK)
            m = mixed[:, c * LANE:(c + 1) * LANE] + bias_ref[:, k * LANE:(k + 1) * LANE]
            o_ref[rows, k * LANE:(k + 1) * LANE] = (u[rows, k * LANE:(k + 1) * LANE] * m).astype(o_ref.dtype)


def _in_kernel(x_ref, g_ref, sh_ref, sc_ref, w_s5, w_sgu, w_ssd, w_dt, w_mla, lng_ref, lnb_ref, wpair_ref, bias_ref,
               h_ref, o_s5, o_sgu, o_ssd, o_dt, o_mla):
    x = x_ref[...]
    y = _rms(x, x.shape[-1]) * g_ref[...]
    hb = (y * (1.0 + sc_ref[...]) + sh_ref[...]).astype(BF16)
    h_ref[...] = hb
    for w, o in ((w_s5, o_s5), (w_ssd, o_ssd), (w_dt, o_dt), (w_mla, o_mla)):
        o[...] = jnp.dot(hb, w[...], preferred_element_type=F32).astype(o.dtype)
    _sgu_body(jnp.dot(hb, w_sgu[...], preferred_element_type=F32), lng_ref, lnb_ref, wpair_ref, bias_ref, o_sgu)


def in_proj(x, g, mod4, l, midx, ws, out_dtypes, sgu_params):
    r, d = x.shape
    in_specs = [pl.BlockSpec((TM, d), lambda i: (i, 0)), _full((1, d)),
                _mod_spec(l, midx, 0, d), _mod_spec(l, midx, 1, d)]
    in_specs += [_wspec(w) for w in ws] + [_wspec(p) for p in sgu_params]
    out_shape = [jax.ShapeDtypeStruct((r, d), BF16)]
    out_specs = [pl.BlockSpec((TM, d), lambda i: (i, 0))]
    for n, (w, dt) in enumerate(zip(ws, out_dtypes)):
        width = BR_W if n == 1 else w.shape[1]
        out_shape.append(jax.ShapeDtypeStruct((r, width), dt))
        out_specs.append(pl.BlockSpec((TM, width), lambda i: (i, 0)))
    return pl.pallas_call(
        _in_kernel, out_shape=out_shape, grid=(r // TM,), in_specs=in_specs, out_specs=out_specs,
        compiler_params=_cp("parallel"), name="in_proj",
    )(x, g, mod4, mod4, *[_warg(w) for w in ws], *[_warg(p) for p in sgu_params])


def _s5_disc_kernel(are, aim, ldt, bre, bim, lam_re, lam_im, bbre, bbim):
    a_re, a_im = are[...], aim[...]
    dt = jnp.exp(ldt[...])
    mag = jnp.exp(a_re * dt)
    ang = a_im * dt
    ab_re = mag * jnp.cos(ang)
    ab_im = mag * jnp.sin(ang)
    den = a_re * a_re + a_im * a_im
    f_re = ((ab_re - 1.0) * a_re + ab_im * a_im) / den
    f_im = (ab_im * a_re - (ab_re - 1.0) * a_im) / den
    lam_re[...] = ab_re
    lam_im[...] = ab_im
    for c in range(S5_GROUP):
        bbre[c] = f_re * bre[c] - f_im * bim[c]
        bbim[c] = f_re * bim[c] + f_im * bre[c]


def s5_discretise(a_re, a_im, b_re, b_im, log_dt):
    shp = a_re.shape
    rows = int(np.prod(shp)) // LANE
    are = a_re.astype(F32).reshape(rows, LANE)
    aim = a_im.astype(F32).reshape(rows, LANE)
    ldt = jnp.broadcast_to(log_dt.astype(F32)[..., None], shp).reshape(rows, LANE)
    bre = jnp.moveaxis(b_re.astype(F32), -1, 0).reshape(S5_GROUP, rows, LANE)
    bim = jnp.moveaxis(b_im.astype(F32), -1, 0).reshape(S5_GROUP, rows, LANE)
    outs = pl.pallas_call(
        _s5_disc_kernel,
        out_shape=[jax.ShapeDtypeStruct((rows, LANE), F32)] * 2
        + [jax.ShapeDtypeStruct((S5_GROUP, rows, LANE), F32)] * 2,
        name="s5_discretise",
    )(are, aim, ldt, bre, bim)
    lam_re, lam_im = outs[0].reshape(shp), outs[1].reshape(shp)
    bb_re = jnp.moveaxis(outs[2].reshape((S5_GROUP,) + shp), 0, -1)
    bb_im = jnp.moveaxis(outs[3].reshape((S5_GROUP,) + shp), 0, -1)
    return lam_re, lam_im, bb_re, bb_im


def s5_pack(lam_re, lam_im, bb_re, bb_im, c_re, c_im):
    depth = lam_re.shape[0]
    gpb = LANE // S5_GROUP
    eye = jnp.eye(gpb, dtype=F32)

    def lam_blocks(v):
        return v.reshape(depth, 2, S5_JB, 1, S5_BW)

    lam = jnp.concatenate([lam_blocks(lam_re), lam_blocks(lam_im)], axis=-1)
    lam = jnp.broadcast_to(lam, (depth, 2, S5_JB, SUBLANE, 2 * S5_BW))

    def b_blocks(bb):
        v = bb.reshape(depth, 2, S5_JB, gpb, S5_STATE, S5_GROUP)
        return jnp.einsum('ldjgpc,gh->ldjgchp', v, eye).reshape(depth, 2, S5_JB, LANE, S5_BW)

    bblk = jnp.concatenate([b_blocks(bb_re), b_blocks(bb_im)], axis=-1).astype(BF16)

    def c_blocks(cc):
        v = cc.astype(F32).reshape(depth, 2, S5_JB, gpb, S5_GROUP, S5_STATE)
        return jnp.einsum('ldjgcp,gh->ldjhpgc', v, eye).reshape(depth, 2, S5_JB, S5_BW, LANE)

    cblk = jnp.concatenate([c_blocks(c_re), -c_blocks(c_im)], axis=-2).astype(BF16)
    return lam, bblk, cblk


def _cpow(re, im, n):
    out = None
    while n:
        if n & 1:
            out = (re, im) if out is None else (out[0] * re - out[1] * im, out[0] * im + out[1] * re)
        n >>= 1
        if n:
            re, im = re * re - im * im, 2.0 * re * im
    return out


def _s5_pass_kernel(*refs, tt, lseg, with_y, n_sub):
    if with_y:
        u_ref, bblk, lam, cblk, sloc, h0, y_ref, hfin, uperm, hst, yperm = refs[:11]
    else:
        u_ref, bblk, lam, s_out, uperm, hst = refs[:6]
    bus = refs[-n_sub * S5_JB:]
    d = pl.program_id(1)
    j = pl.program_id(2)
    offs = [pl.multiple_of((ph + d * (n_sub - 1 - 2 * ph)) * tt, tt) for ph in range(n_sub)]

    @pl.when(j == 0)
    def _():
        if not with_y:
            hst[...] = jnp.zeros(hst.shape, F32)
        else:
            for jb in range(S5_JB):
                pr, pi = _cpow(lam[jb, 0:1, :S5_BW], lam[jb, 0:1, S5_BW:], lseg)

                def chain(order, jb=jb, pr=pr, pi=pi):
                    cr, ci = h0[jb, :, :S5_BW], h0[jb, :, S5_BW:]
                    for s in order:
                        hst[jb, s:s + 1, :S5_BW] = cr
                        hst[jb, s:s + 1, S5_BW:] = ci
                        sr, si = sloc[jb, s:s + 1, :S5_BW], sloc[jb, s:s + 1, S5_BW:]
                        cr, ci = pr * cr - pi * ci + sr, pr * ci + pi * cr + si
                    hfin[jb, :, :S5_BW] = cr
                    hfin[jb, :, S5_BW:] = ci

                @pl.when(d == 0)
                def _():
                    chain(range(S5_NSEG))

                @pl.when(d == 1)
                def _():
                    chain(range(S5_NSEG - 1, -1, -1))

    def drive(ph):
        for s in range(S5_NSEG):
            us = u_ref[s, pl.ds(offs[ph], tt), :].astype(F32)
            for k in range(S5_JB):
                uperm[ph, k, pl.ds(s, tt, stride=S5_NSEG), :] = us[:, k * LANE:(k + 1) * LANE]
        for jb in range(S5_JB):
            bus[ph * S5_JB + jb][...] = jnp.dot(uperm[ph, jb].astype(BF16), bblk[jb], preferred_element_type=F32)

    def scan(ph):
        for jb in range(S5_JB):
            bu = bus[ph * S5_JB + jb]
            lr, li = lam[jb, :, :S5_BW], lam[jb, :, S5_BW:]

            def step(i, carry, lr=lr, li=li, bu=bu):
                hr, hi = carry
                t = i + d * (tt - 1 - 2 * i)
                r0 = pl.multiple_of(t * S5_NSEG, S5_NSEG)
                nr = lr * hr - li * hi + bu[pl.ds(r0, S5_NSEG), :S5_BW]
                ni = lr * hi + li * hr + bu[pl.ds(r0, S5_NSEG), S5_BW:]
                if with_y:
                    bu[pl.ds(r0, S5_NSEG), :S5_BW] = nr
                    bu[pl.ds(r0, S5_NSEG), S5_BW:] = ni
                return nr, ni

            hr, hi = lax.fori_loop(0, tt, step, (hst[jb, :, :S5_BW], hst[jb, :, S5_BW:]), unroll=True)
            hst[jb, :, :S5_BW] = hr
            hst[jb, :, S5_BW:] = hi

    def read(ph):
        for jb in range(S5_JB):
            yperm[ph, jb] = jnp.dot(bus[ph * S5_JB + jb][...].astype(BF16), cblk[jb],
                                    preferred_element_type=F32)
        for s in range(S5_NSEG):
            for k in range(S5_JB):
                y_ref[s, pl.ds(offs[ph], tt), k * LANE:(k + 1) * LANE] = (
                    yperm[ph, k, pl.ds(s, tt, stride=S5_NSEG), :].astype(y_ref.dtype))

    for ph in range(n_sub):
        drive(ph)
    for ph in range(n_sub):
        scan(ph)
        if with_y:
            read(ph)
    if not with_y:
        s_out[...] = hst[...]


def s5_scan(u, bblk, lam, cblk, l, bsz, h0):
    seq_len = u.shape[0] // bsz
    lseg = seq_len // S5_NSEG
    tt = min(lseg, 64)
    n_sub = next(n for n in (4, 2, 1) if (lseg // tt) % n == 0)
    nt = lseg // (tt * n_sub)
    u5 = u.reshape(bsz, S5_NSEG, lseg, BR_W)

    def tile(d, j):
        return j + d * (nt - 1 - 2 * j)

    u_spec = pl.BlockSpec((None, S5_NSEG, n_sub * tt, BR_W), lambda b, d, j: (b, 0, tile(d, j), 0))
    y_spec = pl.BlockSpec((None, None, S5_NSEG, n_sub * tt, BR_W), lambda b, d, j: (d, b, 0, tile(d, j), 0))
    w_b = pl.BlockSpec((None, None, S5_JB, LANE, 2 * S5_BW), lambda b, d, j: (l, d, 0, 0, 0))
    w_lam = pl.BlockSpec((None, None, S5_JB, SUBLANE, 2 * S5_BW), lambda b, d, j: (l, d, 0, 0, 0))
    w_c = pl.BlockSpec((None, None, S5_JB, 2 * S5_BW, LANE), lambda b, d, j: (l, d, 0, 0, 0))
    st8 = pl.BlockSpec((None, None, S5_JB, SUBLANE, 2 * S5_BW), lambda b, d, j: (b, d, 0, 0, 0))
    st1 = pl.BlockSpec((None, None, S5_JB, 1, 2 * S5_BW), lambda b, d, j: (b, d, 0, 0, 0))
    n_rows = S5_NSEG * tt
    scratch = [pltpu.VMEM((n_sub, S5_JB, n_rows, LANE), F32), pltpu.VMEM((S5_JB, SUBLANE, 2 * S5_BW), F32)]
    bus = [pltpu.VMEM((n_rows, 2 * S5_BW), F32)] * (n_sub * S5_JB)
    grid = (bsz, 2, nt)
    cp = _cp("arbitrary", "arbitrary", "arbitrary")

    sloc = pl.pallas_call(
        functools.partial(_s5_pass_kernel, tt=tt, lseg=lseg, with_y=False, n_sub=n_sub),
        out_shape=jax.ShapeDtypeStruct((bsz, 2, S5_JB, SUBLANE, 2 * S5_BW), F32),
        grid=grid, in_specs=[u_spec, w_b, w_lam], out_specs=st8,
        scratch_shapes=scratch + bus, compiler_params=cp, name="s5_local",
    )(u5, bblk, lam)

    y, hfin = pl.pallas_call(
        functools.partial(_s5_pass_kernel, tt=tt, lseg=lseg, with_y=True, n_sub=n_sub),
        out_shape=[jax.ShapeDtypeStruct((2, bsz, S5_NSEG, lseg, BR_W), BF16),
                   jax.ShapeDtypeStruct((bsz, 2, S5_JB, 1, 2 * S5_BW), F32)],
        grid=grid, in_specs=[u_spec, w_b, w_lam, w_c, st8, st1], out_specs=[y_spec, st1],
        scratch_shapes=scratch + [pltpu.VMEM((n_sub, S5_JB, n_rows, LANE), F32)] + bus,
        compiler_params=cp, name="s5_emit",
    )(u5, bblk, lam, cblk, sloc, h0)
    return y.reshape(2, bsz * seq_len, BR_W), hfin


def _ssd_prep_kernel(cur_ref, prev_ref, next_ref, dt_ref, w_ref, b_ref, dtb_ref, alane_ref, esel_ref, tri_ref,
                     xs_ref, bc_ref, ac_ref, act_ref, xt_ref, *, seq_len):
    i = pl.program_id(0)
    x = cur_ref[:, BR_W:].astype(F32)
    tm = x.shape[0]
    row = lax.broadcasted_iota(jnp.int32, (tm, 1), 0)
    pos = lax.rem(row + i * tm, seq_len)
    prev_row = prev_ref[SUBLANE - 1:SUBLANE, BR_W:].astype(F32)
    next_row = next_ref[0:1, BR_W:].astype(F32)
    x_prev = jnp.where(row == 0, prev_row, pltpu.roll(x, 1, axis=0))
    x_prev = jnp.where(pos == 0, 0.0, x_prev)
    x_next = jnp.where(row == tm - 1, next_row, pltpu.roll(x, tm - 1, axis=0))
    x_next = jnp.where(pos == seq_len - 1, 0.0, x_next)
    y = _silu(w_ref[0:1, :] * x_prev + w_ref[1:2, :] * x + w_ref[2:3, :] * x_next + b_ref[...])
    xs = y[:, :BR_W]
    xs_ref[...] = xs
    bc_ref[...] = y[:, BR_W:].astype(bc_ref.dtype)
    t = dt_ref[...] + dtb_ref[...]
    sp = jnp.maximum(t, 0.0) + jnp.log1p(jnp.exp(-jnp.abs(t)))
    lane = lax.broadcasted_iota(jnp.int32, t.shape, 1)
    d_a = sp * alane_ref[...]
    d_a = (jnp.where(lane < SSD_HEADS, d_a, 0.0),
           jnp.where(lane < SSD_HEADS, pltpu.roll(d_a, LANE - SSD_HEADS, axis=1), 0.0))
    for d in range(2):
        for c in range(tm // SSD_T):
            rows = slice(c * SSD_T, (c + 1) * SSD_T)
            a_cum = jnp.dot(tri_ref[d], d_a[d][rows], preferred_element_type=F32, precision=HIGHEST)
            ac_ref[d, rows, :] = a_cum
            act_ref[d, c] = a_cum.T[:SUBLANE, :]
    sp_hi = sp.astype(BF16)
    sp_lo = (sp - sp_hi.astype(F32)).astype(BF16)
    for d in range(2):
        dt_x = (jnp.dot(sp_hi, esel_ref[d], preferred_element_type=F32)
                + jnp.dot(sp_lo, esel_ref[d], preferred_element_type=F32))
        xt_ref[d] = (xs * dt_x).astype(xt_ref.dtype)


def ssd_prep(p_ssd, dt_raw, seq_len, conv_w, conv_b, dt_bias, a_lane, esel, tri):
    r, w = p_ssd.shape
    nb = r // SUBLANE
    per = TM // SUBLANE
    cpt = TM // SSD_T
    return pl.pallas_call(
        functools.partial(_ssd_prep_kernel, seq_len=seq_len),
        out_shape=[jax.ShapeDtypeStruct((r, BR_W), F32), jax.ShapeDtypeStruct((r, 2 * SSD_GN), BF16),
                   jax.ShapeDtypeStruct((2, r, LANE), F32),
                   jax.ShapeDtypeStruct((2, r // SSD_T, SUBLANE, SSD_T), F32),
                   jax.ShapeDtypeStruct((2, r, BR_W), BF16)],
        grid=(r // TM,),
        in_specs=[pl.BlockSpec((TM, w), lambda i: (i, 0)),
                  pl.BlockSpec((SUBLANE, w), lambda i: (jnp.maximum(i * per - 1, 0), 0)),
                  pl.BlockSpec((SUBLANE, w), lambda i: (jnp.minimum((i + 1) * per, nb - 1), 0)),
                  pl.BlockSpec((TM, LANE), lambda i: (i, 0)),
                  _full(conv_w.shape), _full(conv_b.shape), _full(dt_bias.shape), _full(a_lane.shape),
                  _full(esel.shape), _full(tri.shape)],
        out_specs=[pl.BlockSpec((TM, BR_W), lambda i: (i, 0)), pl.BlockSpec((TM, 2 * SSD_GN), lambda i: (i, 0)),
                   pl.BlockSpec((2, TM, LANE), lambda i: (0, i, 0)),
                   pl.BlockSpec((2, cpt, SUBLANE, SSD_T), lambda i: (0, i, 0, 0)),
                   pl.BlockSpec((2, TM, BR_W), lambda i: (0, i, 0))],
        compiler_params=_cp("parallel"), name="ssd_prep",
    )(p_ssd, p_ssd, p_ssd, dt_raw, conv_w, conv_b, dt_bias, a_lane, esel, tri)


def _ssd_scan_kernel(xt_ref, bc_ref, ac_ref, act_ref, tri_ref, h0_ref, y_ref, hfin_ref, st_ref, *, n_chunk, bsz):
    d = pl.program_id(0)
    j = pl.program_id(1)
    T = SSD_T

    @pl.when(j == 0)
    def _():
        st_ref[...] = h0_ref[...]

    tri = tri_ref[...]
    mask = tri > 0.5
    rep = SSD_HEADS // SSD_GROUPS
    low_lane = lax.broadcasted_iota(jnp.int32, (1, LANE), 1) < SSD_HEAD_DIM
    group_row = [(lax.broadcasted_iota(jnp.int32, (SSD_GN, 1), 0) // SSD_STATE) == g for g in range(SSD_GROUPS)]

    def chunk(ci, carry):
        c = ci + d * (n_chunk - 1 - 2 * ci)
        r0 = pl.multiple_of(c * T, T)
        for b in range(bsz):
            a_cum = ac_ref[b, pl.ds(r0, T), :]
            a_cum_t = act_ref[b, c]
            total = jnp.where(d == 0, a_cum[T - 1:T, :], a_cum[0:1, :])
            bc = bc_ref[b, pl.ds(r0, T), :]
            bm_t = bc[:, :SSD_GN].astype(F32).T
            cm = bc[:, SSD_GN:]
            b_grp = [jnp.where(group_row[g], bm_t, 0.0) for g in range(SSD_GROUPS)]
            scores = [jnp.dot(cm, b_grp[g].astype(BF16), preferred_element_type=F32)
                      for g in range(SSD_GROUPS)]
            for k in range(SSD_HEADS // 2):
                x_pair = xt_ref[b, pl.ds(r0, T), k * LANE:(k + 1) * LANE]
                y_d, s_n, e_col, e_tot = [], [], [], []
                for h in (2 * k, 2 * k + 1):
                    g = h // rep
                    col = a_cum[:, h:h + 1]
                    rowv = a_cum_t[h:h + 1, :]
                    tot = total[:, h:h + 1]
                    decay = jnp.exp(jnp.where(mask, col - rowv, -1e30))
                    y_d.append(jnp.dot((scores[g] * decay).astype(BF16), x_pair, preferred_element_type=F32))
                    bw = (b_grp[g] * jnp.exp(tot - rowv)).astype(BF16)
                    s_n.append(jnp.dot(bw, x_pair, preferred_element_type=F32))
                    e_col.append(jnp.exp(col))
                    e_tot.append(jnp.exp(tot))
                s_old = st_ref[b, k]
                y_off = jnp.dot(cm, s_old.astype(BF16), preferred_element_type=F32)
                y = (jnp.where(low_lane, y_d[0], y_d[1])
                     + y_off * jnp.where(low_lane, e_col[0], e_col[1]))
                y_ref[b, pl.ds(r0, T), k * LANE:(k + 1) * LANE] = y.astype(y_ref.dtype)
                st_ref[b, k] = (s_old * jnp.where(low_lane, e_tot[0], e_tot[1])
                                + jnp.where(low_lane, s_n[0], s_n[1]))
        return carry

    lax.fori_loop(0, n_chunk, chunk, 0, unroll=True)
    hfin_ref[...] = st_ref[...]


def ssd_scan(xt, bc, a_cum, a_cum_t, tri, bsz, h0):
    r = bc.shape[0]
    seq_len = r // bsz
    ts = min(seq_len, TM)
    nt = seq_len // ts

    def tile(d, j):
        return j + d * (nt - 1 - 2 * j)

    st_shape = (bsz, SSD_HEADS // 2, SSD_GN, LANE)
    st = pl.BlockSpec((None,) + st_shape, lambda d, j: (d, 0, 0, 0, 0))
    y, hfin = pl.pallas_call(
        functools.partial(_ssd_scan_kernel, n_chunk=ts // SSD_T, bsz=bsz),
        out_shape=[jax.ShapeDtypeStruct((2, bsz, seq_len, BR_W), BF16),
                   jax.ShapeDtypeStruct((2,) + st_shape, F32)],
        grid=(2, nt),
        in_specs=[pl.BlockSpec((None, bsz, ts, BR_W), lambda d, j: (d, 0, tile(d, j), 0)),
                  pl.BlockSpec((bsz, ts, 2 * SSD_GN), lambda d, j: (0, tile(d, j), 0)),
                  pl.BlockSpec((None, bsz, ts, LANE), lambda d, j: (d, 0, tile(d, j), 0)),
                  pl.BlockSpec((None, bsz, ts // SSD_T, SUBLANE, SSD_T), lambda d, j: (d, 0, tile(d, j), 0, 0)),
                  pl.BlockSpec((None, SSD_T, SSD_T), lambda d, j: (d, 0, 0)), st],
        out_specs=[pl.BlockSpec((None, bsz, ts, BR_W), lambda d, j: (d, 0, tile(d, j), 0)), st],
        scratch_shapes=[pltpu.VMEM(st_shape, F32)],
        compiler_params=_cp("arbitrary", "arbitrary"), name="ssd_scan",
    )(xt.reshape(2, bsz, seq_len, BR_W), bc.reshape(bsz, seq_len, 2 * SSD_GN),
      a_cum.reshape(2, bsz, seq_len, LANE), a_cum_t.reshape(2, bsz, seq_len // SSD_T, SUBLANE, SSD_T), tri, h0)
    return y.reshape(2, r, BR_W), hfin


def _rope_table_kernel(frow_ref, fcol_ref, cos_ref, sin_ref):
    tr = cos_ref.shape[0]
    pos = (lax.broadcasted_iota(jnp.int32, (tr, 1), 0) + pl.program_id(0) * tr).astype(F32)
    pos_row = jnp.floor((pos + 0.5) * (1.0 / GRID_W))
    pos_col = pos - pos_row * GRID_W
    ang = pos_row * frow_ref[...] + pos_col * fcol_ref[...]
    lane = lax.broadcasted_iota(jnp.int32, ang.shape, 1)
    quarter = MLA_ROPE // 2
    first = (lane >= MLA_NOPE) & (lane < MLA_NOPE + quarter)
    second = (lane >= MLA_NOPE + quarter) & (lane < MLA_QK)
    c, s = jnp.cos(ang), jnp.sin(ang)
    cos_ref[...] = jnp.where(first | second, c, jnp.where(lane < MLA_NOPE, 1.0, 0.0))
    sin_ref[...] = jnp.where(first, -s, jnp.where(second, s, 0.0))


def _rope_partner(v):
    quarter = MLA_ROPE // 2
    src = np.arange(LANE)
    src[MLA_NOPE:MLA_NOPE + quarter] += quarter
    src[MLA_NOPE + quarter:MLA_QK] -= quarter
    valid = (np.arange(LANE) >= MLA_NOPE) & (np.arange(LANE) < MLA_QK)
    return jnp.where(valid, v[..., src], 0)


def rope_tables(seq_len):
    pairs = MLA_ROPE // 4
    inv_freq = ROPE_BASE ** (-jnp.arange(pairs, dtype=F32) / pairs)
    zeros = lambda n: jnp.zeros((n,), F32)
    half = jnp.concatenate([inv_freq, zeros(pairs)])
    f_row = jnp.concatenate([zeros(MLA_NOPE), half, half, zeros(LANE - MLA_QK)]).reshape(1, LANE)
    half = jnp.concatenate([zeros(pairs), inv_freq])
    f_col = jnp.concatenate([zeros(MLA_NOPE), half, half, zeros(LANE - MLA_QK)]).reshape(1, LANE)
    tr = min(seq_len, 1024)
    spec = pl.BlockSpec((tr, LANE), lambda i: (i, 0))
    return pl.pallas_call(
        _rope_table_kernel, out_shape=[jax.ShapeDtypeStruct((seq_len, LANE), F32)] * 2,
        grid=(seq_len // tr,), in_specs=[_full((1, LANE))] * 2, out_specs=[spec] * 2,
        compiler_params=_cp("parallel"), name="rope_tables",
    )(f_row, f_col)


def _mla_prep_kernel(*refs, rope):
    if rope:
        (p_ref, gq_ref, gkv_ref, wq_ref, wkv_ref, nq_ref, nk_ref, wqs_ref, nqs_ref, nks_ref, cos_ref, sin_ref,
         q_ref, k_ref, v_ref) = refs
    else:
        p_ref, gq_ref, gkv_ref, wq_ref, wkv_ref, nq_ref, nk_ref, q_ref, k_ref, v_ref = refs
    p = p_ref[...].astype(F32)
    cq = (_rms(p[:, :MLA_Q_LORA], MLA_Q_LORA) * gq_ref[...]).astype(BF16)
    ckv = (_rms(p[:, MLA_Q_LORA:MLA_Q_LORA + MLA_KV_LORA], MLA_KV_LORA) * gkv_ref[...]).astype(BF16)
    kr = p[:, MLA_Q_LORA + MLA_KV_LORA:MLA_Q_LORA + MLA_KV_LORA + LANE]
    q_all = jnp.dot(cq, wq_ref[...], preferred_element_type=F32)
    kv_all = jnp.dot(ckv, wkv_ref[...], preferred_element_type=F32)
    lane = lax.broadcasted_iota(jnp.int32, (1, LANE), 1)
    one_col = jnp.where(lane == MLA_V, 1.0, 0.0)
    scale = MLA_QK ** -0.5 * LOG2E
    if rope:
        qs_all = jnp.dot(cq, wqs_ref[...], preferred_element_type=F32)
        q_cos, q_sin = nq_ref[...] * cos_ref[...], nqs_ref[...] * sin_ref[...]
        k_cos = nk_ref[...] * cos_ref[...]
        k_part = p[:, MLA_Q_LORA + MLA_KV_LORA + LANE:] * (nks_ref[...] * sin_ref[...])

    def inv_rms(t):
        return lax.rsqrt(jnp.sum(t * t, axis=-1, keepdims=True) * (1.0 / MLA_QK) + 1e-6)

    for h in range(MLA_HEADS):
        q = q_all[:, h * LANE:(h + 1) * LANE]
        k = kv_all[:, 2 * h * LANE:(2 * h + 1) * LANE] + kr
        if rope:
            q_out = (q * q_cos + qs_all[:, h * LANE:(h + 1) * LANE] * q_sin) * (inv_rms(q) * scale)
            k_out = (k * k_cos + k_part) * inv_rms(k)
        else:
            q_out = q * nq_ref[...] * (inv_rms(q) * scale)
            k_out = k * nk_ref[...] * inv_rms(k)
        q_ref[h] = q_out.astype(q_ref.dtype)
        k_ref[h] = k_out.astype(k_ref.dtype)
        v_ref[h] = (kv_all[:, (2 * h + 1) * LANE:(2 * h + 2) * LANE] + one_col).astype(v_ref.dtype)


def mla_prep(p_mla, bsz, gq, gkv, wq, wkv, nq, nk, rope_args):
    r, w = p_mla.shape
    seq_len = r // bsz
    tr = min(seq_len, TM)
    nt = seq_len // tr
    rope = rope_args is not None
    in_specs = [pl.BlockSpec((tr, w), lambda b, i: (b * nt + i, 0)), _full(gq.shape), _full(gkv.shape),
                _wspec(wq), _wspec(wkv), _full(nq.shape), _full(nk.shape)]
    args = [p_mla, gq, gkv, _warg(wq), _warg(wkv), nq, nk]
    if rope:
        in_specs += [_wspec(a) for a in rope_args[:3]]
        in_specs += [pl.BlockSpec((tr, LANE), lambda b, i: (i, 0))] * 2
        args += [_warg(a) for a in rope_args]
    head = pl.BlockSpec((None, MLA_HEADS, tr, LANE), lambda b, i: (b, 0, i, 0))
    return pl.pallas_call(
        functools.partial(_mla_prep_kernel, rope=rope),
        out_shape=[jax.ShapeDtypeStruct((bsz, MLA_HEADS, seq_len, LANE), BF16)] * 3,
        grid=(bsz, nt), in_specs=in_specs, out_specs=[head] * 3,
        compiler_params=_cp("parallel", "parallel"), name="mla_prep",
    )(*args)


def _attn_kernel(*refs, n_kv):
    bound_ref, q_ref = refs[0], refs[1]
    kv_refs = refs[2:2 + 2 * n_kv]
    o_ref = refs[2 + 2 * n_kv]
    m_ref = refs[3 + 2 * n_kv]
    scores = [[lax.dot_general(q_ref[hh], kv_refs[2 * i][hh], (((1,), (1,)), ((), ())),
                               preferred_element_type=F32) for i in range(n_kv)] for hh in range(2)]

    def row_max(first_set):
        for hh in range(2):
            m = scores[hh][n_kv - 1].max(axis=-1, keepdims=True)
            for si in scores[hh][first_set:n_kv - 1]:
                m = jnp.maximum(m, si.max(axis=-1, keepdims=True))
            m_ref[hh] = m

    if n_kv > 1:
        fast = bound_ref[0] <= ATTN_FAST_BOUND

        @pl.when(fast)
        def _():
            row_max(n_kv - 1)

        @pl.when(jnp.logical_not(fast))
        def _():
            row_max(0)
    else:
        row_max(0)

    for hh in range(2):
        m = m_ref[hh]
        acc = None
        for i, si in enumerate(scores[hh]):
            pv = jnp.dot(jnp.exp2((si - m).astype(BF16)), kv_refs[2 * i + 1][hh], preferred_element_type=F32)
            acc = pv if acc is None else acc + pv
        o = acc[:, :MLA_V] / acc[:, MLA_V:MLA_V + 1]
        o_ref[:, hh * MLA_V:(hh + 1) * MLA_V] = o.astype(o_ref.dtype)


def attention(q, kvs, score_bound):
    bsz, nh, lq, _ = q.shape
    tq = min(lq, TQ)
    nq = lq // tq
    in_specs = [pl.BlockSpec(memory_space=pltpu.SMEM),
                pl.BlockSpec((None, 2, tq, LANE), lambda b, hp, i: (b, hp, i, 0))]
    args = [score_bound, q]
    for k, v in kvs:
        spec = pl.BlockSpec((None, 2, k.shape[2], LANE), lambda b, hp, i: (b, hp, 0, 0))
        in_specs += [spec, spec]
        args += [k, v]
    return pl.pallas_call(
        functools.partial(_attn_kernel, n_kv=len(kvs)),
        out_shape=jax.ShapeDtypeStruct((bsz * lq, nh * MLA_V), BF16),
        grid=(bsz, nh // 2, nq), in_specs=in_specs,
        out_specs=pl.BlockSpec((tq, 2 * MLA_V), lambda b, hp, i: (b * nq + i, hp)),
        scratch_shapes=[pltpu.VMEM((2, tq, 1), F32)],
        compiler_params=_cp("parallel", "parallel", "arbitrary"), name="attention",
    )(*args)


def _merge_kernel(x_ref, h_ref, u_ref, s5f_ref, s5b_ref, s5d_ref, wglu_ref, b_ref,
                  xs_ref, ssdf_ref, ssdb_ref, z_ref, ssdd_ref, ssdg_ref, d_ref,
                  g1_ref, wg_ref, wb_ref, wo_ref,
                  n2_ref, sh2_ref, sc2_ref, g2_ref, wi_ref, wfo_ref, o_ref, *, n_split):
    y = s5d_ref[...] * u_ref[...].astype(F32) + s5f_ref[...].astype(F32) + s5b_ref[...].astype(F32)
    g = jax.nn.gelu(y)
    a = (g * jax.nn.sigmoid(jnp.dot(g.astype(BF16), wglu_ref[...], preferred_element_type=F32))).astype(BF16)
    y = ssdd_ref[...] * xs_ref[...] + ssdf_ref[...].astype(F32) + ssdb_ref[...].astype(F32)
    c = (_rms(y * _silu(z_ref[...].astype(F32)), BR_W) * ssdg_ref[...]).astype(BF16)
    h = h_ref[...]
    dm = x_ref.shape[-1]
    merged = None
    for i, br in enumerate((a, b_ref[...], c, d_ref[...])):
        gate = jax.nn.sigmoid(jnp.dot(h, wg_ref[:, i * dm:(i + 1) * dm], preferred_element_type=F32))
        term = gate * jnp.dot(br, wb_ref[i], preferred_element_type=F32)
        merged = term if merged is None else merged + term
    mix = jnp.dot(merged.astype(BF16), wo_ref[...], preferred_element_type=F32)
    x = x_ref[...] + g1_ref[...] * mix

    y = _rms(x, dm) * n2_ref[...]
    hb = (y * (1.0 + sc2_ref[...]) + sh2_ref[...]).astype(BF16)
    hid = wfo_ref.shape[0]
    step = hid // n_split
    acc = None
    for c in range(n_split):
        gate = jnp.dot(hb, wi_ref[:, c * step:(c + 1) * step], preferred_element_type=F32)
        up = jnp.dot(hb, wi_ref[:, hid + c * step:hid + (c + 1) * step], preferred_element_type=F32)
        part = jnp.dot((_silu(gate) * up).astype(BF16), wfo_ref[c * step:(c + 1) * step, :],
                       preferred_element_type=F32)
        acc = part if acc is None else acc + part
    o_ref[...] = x + g2_ref[...] * acc


def merge_ffn(x, h, s5_in, b, ssd_in, d, mod4, l, midx, wg, wb, wo, norm2, wi, wfo):
    r, dm = x.shape
    n_split = 11 if wfo.shape[0] % (11 * LANE) == 0 else 1
    row = pl.BlockSpec((TM, dm), lambda i: (i, 0))
    br = pl.BlockSpec((TM, BR_W), lambda i: (i, 0))
    fwd = pl.BlockSpec((None, TM, BR_W), lambda i: (0, i, 0))
    bwd = pl.BlockSpec((None, TM, BR_W), lambda i: (1, i, 0))
    vec = _full((1, BR_W))
    u, y_s5, s5_d, w_glu = s5_in
    xs, y_ssd, p_ssd, ssd_d, ssd_g = ssd_in
    return pl.pallas_call(
        functools.partial(_merge_kernel, n_split=n_split),
        out_shape=jax.ShapeDtypeStruct((r, dm), F32), grid=(r // TM,),
        in_specs=[row, row, br, fwd, bwd, vec, _wspec(w_glu), br,
                  br, fwd, bwd, br, vec, vec, br,
                  _mod_spec(l, midx, 2, dm), _wspec(wg), _wspec(wb), _wspec(wo),
                  _full((1, dm)), _mod_spec(l, midx, 3, dm), _mod_spec(l, midx, 4, dm),
                  _mod_spec(l, midx, 5, dm), _wspec(wi), _wspec(wfo)],
        out_specs=row, compiler_params=_cp("parallel"), name="merge_ffn",
    )(x, h, u, y_s5, y_s5, s5_d, _warg(w_glu), b, xs, y_ssd, y_ssd, p_ssd, ssd_d, ssd_g, d, mod4,
      _warg(wg), _warg(wb), _warg(wo), norm2, mod4, mod4, mod4, _warg(wi), _warg(wfo))


def _pad_cols(w, n):
    return jnp.pad(w, ((0, 0),) * (w.ndim - 1) + ((0, n - w.shape[-1]),))


def kernel(x, c, ctx, c_ctx, w_ada, b_ada, norm1_g, norm2_g, w_in, s5_a_re, s5_a_im, s5_b_re, s5_b_im, s5_c_re, s5_c_im, s5_log_dt, s5_d, s5_w_glu, sgu_ln_g, sgu_ln_b, sgu_w_s, sgu_b_s, ssd_conv_w, ssd_conv_b, ssd_a_log, ssd_dt_bias, ssd_d, ssd_norm_g, mla_q_a_norm, mla_w_uq, mla_kv_a_norm, mla_w_ukv, mla_q_norm, mla_k_norm, w_branch, w_out, w_ffn_in, w_ffn_out):
    bsz, seq, dm = x.shape
    lc = ctx.shape[1]
    depth = w_ada.shape[0]
    assert seq % TM == 0 and (bsz * lc) % TM == 0 and seq % (S5_NSEG * SUBLANE) == 0
    assert lc % SSD_T == 0 and lc % (S5_NSEG * SUBLANE) == 0 and bsz + 1 <= 8

    cc8 = jnp.zeros((8, dm), F32).at[:bsz].set(c.astype(F32)).at[bsz].set(c_ctx.astype(F32))
    mod4 = ada_table(cc8, w_ada.astype(F32), b_ada.astype(F32)).reshape(depth, 8, 1, 6 * dm)
    lat_tiles = seq // TM
    midx_lat = lambda i: i // lat_tiles
    midx_ctx = lambda i: bsz

    off = np.cumsum([0, BR_W, 2 * BR_W, BR_W + SSD_CONV_CH + 2 * SSD_HEADS,
                     MLA_Q_LORA + MLA_KV_LORA + MLA_ROPE, N_BRANCH * dm])
    w_s5 = w_in[:, :, off[0]:off[1]].astype(BF16)
    w_sgu = w_in[:, :, off[1]:off[2]].astype(BF16)
    w_ssd = w_in[:, :, off[2]:off[2] + BR_W + SSD_CONV_CH].astype(BF16)
    w_dt = _pad_cols(w_in[:, :, off[2] + BR_W + SSD_CONV_CH:off[3]], LANE).astype(BF16)
    w_mla_main = w_in[:, :, off[3]:off[3] + MLA_Q_LORA + MLA_KV_LORA]
    w_kr = w_in[:, :, off[3] + MLA_Q_LORA + MLA_KV_LORA:off[4]]
    zeros = lambda n: jnp.zeros((depth, dm, n), w_in.dtype)
    kr_block = jnp.concatenate([zeros(MLA_NOPE), w_kr, zeros(LANE - MLA_QK)], axis=-1)
    w_mla = jnp.concatenate([w_mla_main, kr_block, _rope_partner(kr_block)], axis=-1).astype(BF16)
    w_gate = w_in[:, :, off[4]:off[5]].astype(BF16)

    lam_re, lam_im, bb_re, bb_im = s5_discretise(s5_a_re, s5_a_im, s5_b_re, s5_b_im, s5_log_dt)
    lam, bblk, cblk = s5_pack(lam_re, lam_im, bb_re, bb_im, s5_c_re, s5_c_im)
    s5_wg = s5_w_glu.astype(BF16)

    sgu_w = sgu_w_s.reshape(depth, SGU_HEADS // 2, 2, SGU_CHUNK, SGU_CHUNK)
    sgu_w = jnp.concatenate([sgu_w[:, :, 0], sgu_w[:, :, 1]], axis=-1).astype(BF16)
    sgu_b = jnp.repeat(jnp.swapaxes(sgu_b_s, 1, 2), BR_W // SGU_HEADS, axis=-1).astype(F32)

    a_neg = -jnp.exp(ssd_a_log.astype(F32))
    a_lane = _pad_cols(a_neg.reshape(depth, 1, 2 * SSD_HEADS), LANE)
    head_of_lane = jnp.arange(LANE)[None, :, None] - SSD_HEADS * jnp.arange(2)[:, None, None]
    col_head = jnp.arange(BR_W) // SSD_HEAD_DIM
    esel = ((head_of_lane == col_head[None, None, :]) & (head_of_lane >= 0)).astype(BF16)
    ti = jnp.arange(SSD_T)
    tri = jnp.stack([ti[None, :] <= ti[:, None], ti[None, :] >= ti[:, None]]).astype(F32)
    dt_bias = _pad_cols(ssd_dt_bias.astype(F32).reshape(depth, 1, 2 * SSD_HEADS), LANE)
    ssd_dskip = jnp.repeat(ssd_d.astype(F32), SSD_HEAD_DIM, axis=-1)[:, None, :]

    wq = mla_w_uq.reshape(depth, MLA_Q_LORA, MLA_HEADS, MLA_QK)
    wq = _pad_cols(wq, LANE)
    wq_sw = _rope_partner(wq).reshape(depth, MLA_Q_LORA, MLA_HEADS * LANE).astype(BF16)
    wq = wq.reshape(depth, MLA_Q_LORA, MLA_HEADS * LANE).astype(BF16)
    wkv = mla_w_ukv.reshape(depth, MLA_KV_LORA, MLA_HEADS, 2, MLA_NOPE)
    wkv = _pad_cols(wkv, LANE).reshape(depth, MLA_KV_LORA, MLA_HEADS * 2 * LANE).astype(BF16)
    nq = _pad_cols(mla_q_norm.astype(F32), LANE)[:, None, :]
    nk = _pad_cols(mla_k_norm.astype(F32), LANE)[:, None, :]
    nq_sw, nk_sw = _rope_partner(nq), _rope_partner(nk)
    rope_cos, rope_sin = rope_tables(seq)
    score_bound = (MLA_QK ** 0.5 * LOG2E * jnp.max(jnp.abs(mla_q_norm.astype(F32)), axis=-1)
                   * jnp.max(jnp.abs(mla_k_norm.astype(F32)), axis=-1))

    wb = w_branch.astype(BF16)
    wo = w_out.astype(BF16)
    wfi = w_ffn_in.astype(BF16)
    wfo = w_ffn_out.astype(BF16)

    row = lambda v, l: v[l].astype(F32).reshape(1, -1)
    x_lat = x.astype(F32).reshape(bsz * seq, dm)
    x_ctx = ctx.astype(F32).reshape(bsz * lc, dm)
    s5_zero = jnp.zeros((bsz, 2, S5_JB, 1, 2 * S5_BW), F32)
    ssd_zero = jnp.zeros((2, bsz, SSD_HEADS // 2, SSD_GN, LANE), F32)

    for l in range(depth):
        need_ctx = l < depth - 1
        lay = lambda w: _Layer(w, l)
        ws = tuple(lay(w) for w in (w_s5, w_sgu, w_ssd, w_dt, w_mla))
        w_merge = (lay(w_gate), lay(wb), lay(wo))
        w_ffn = (lay(wfi), lay(wfo))
        dts = (BF16, BF16, BF16, F32, BF16)
        g1 = row(norm1_g, l)
        sgu_p = (row(sgu_ln_g, l), row(sgu_ln_b, l), lay(sgu_w), lay(sgu_b))
        h_l, u_l, b_l, p_l, dtr_l, m_l = in_proj(x_lat, g1, mod4, l, midx_lat, ws, dts, sgu_p)
        h_c, u_c, b_c, p_c, dtr_c, m_c = in_proj(x_ctx, g1, mod4, l, midx_ctx, ws, dts, sgu_p)

        y_c, s5_h = s5_scan(u_c, bblk, lam, cblk, l, bsz, s5_zero)
        y_l, _ = s5_scan(u_l, bblk, lam, cblk, l, bsz, s5_h)
        s5_tail = (row(s5_d, l), lay(s5_wg))
        conv_w, conv_b = ssd_conv_w[l].astype(F32), row(ssd_conv_b, l)
        ssd_w = (conv_w, conv_b, dt_bias[l], a_lane[l], esel, tri)
        xs_c, bc_c, ac_c, act_c, xt_c = ssd_prep(p_c, dtr_c, lc, *ssd_w)
        xs_l, bc_l, ac_l, act_l, xt_l = ssd_prep(p_l, dtr_l, seq, *ssd_w)
        yc_c, ssd_h = ssd_scan(xt_c, bc_c, ac_c, act_c, tri, bsz, ssd_zero)
        yc_l, _ = ssd_scan(xt_l, bc_l, ac_l, act_l, tri, bsz, ssd_h)
        ssd_tail = (ssd_dskip[l], row(ssd_norm_g, l))
        mla_w = (row(mla_q_a_norm, l), row(mla_kv_a_norm, l), lay(wq), lay(wkv), nq[l], nk[l])
        q_c, k_c, v_c = mla_prep(m_c, bsz, *mla_w, None)
        q_l, k_l, v_l = mla_prep(m_l, bsz, *mla_w, (lay(wq_sw), nq_sw[l], nk_sw[l], rope_cos, rope_sin))
        d_l = attention(q_l, [(k_l, v_l), (k_c, v_c)], score_bound[l:l + 1])

        x_lat = merge_ffn(x_lat, h_l, (u_l, y_l) + s5_tail, b_l, (xs_l, yc_l, p_l) + ssd_tail, d_l,
                          mod4, l, midx_lat, *w_merge, row(norm2_g, l), *w_ffn)
        if need_ctx:
            d_c = attention(q_c, [(k_c, v_c)], score_bound[l:l + 1])
            x_ctx = merge_ffn(x_ctx, h_c, (u_c, y_c) + s5_tail, b_c, (xs_c, yc_c, p_c) + ssd_tail, d_c,
                              mod4, l, midx_ctx, *w_merge, row(norm2_g, l), *w_ffn)
    return x_lat.reshape(bsz, seq, dm).astype(x.dtype)
```

```python
import functools

import jax
import jax.numpy as jnp
import numpy as np
from jax import lax
from jax.experimental import pallas as pl
from jax.experimental.pallas import tpu as pltpu

F32 = jnp.float32
BF16 = jnp.bfloat16
HIGHEST = lax.Precision.HIGHEST

LANE = 128
SUBLANE = 8
VMEM_LIMIT = 56 * 1024 * 1024

GRID_W = 64
BR_W = 384
S5_GROUP = 16
S5_GROUPS = BR_W // S5_GROUP
S5_STATE = 64
S5_NSEG = SUBLANE
S5_JB = BR_W // LANE
S5_BW = (LANE // S5_GROUP) * S5_STATE
SGU_CHUNK = 128
SGU_HEADS = 6
SSD_HEADS = 6
SSD_HEAD_DIM = 64
SSD_GROUPS = 2
SSD_STATE = 64
SSD_GN = SSD_GROUPS * SSD_STATE
SSD_CONV_CH = BR_W + 2 * SSD_GN
SSD_T = 128
MLA_HEADS = 6
MLA_NOPE = 64
MLA_ROPE = 32
MLA_V = 64
MLA_QK = MLA_NOPE + MLA_ROPE
MLA_Q_LORA = 384
MLA_KV_LORA = 256
ROPE_BASE = 10000.0
LOG2E = 1.4426950408889634
N_BRANCH = 4

TM = 512
TQ = 256
ATTN_FAST_BOUND = 30.0


def _cp(*sem):
    return pltpu.CompilerParams(dimension_semantics=sem, vmem_limit_bytes=VMEM_LIMIT)


def _full(shape):
    n = len(shape)
    return pl.BlockSpec(shape, lambda *_: (0,) * n)


class _Layer:
    def __init__(self, arr, l):
        self.arr, self.l = arr, l

    @property
    def shape(self):
        return self.arr.shape[1:]


def _wspec(w):
    if not isinstance(w, _Layer):
        return _full(w.shape)
    n, l = len(w.shape), w.l
    return pl.BlockSpec((None,) + tuple(w.shape), lambda *_: (l,) + (0,) * n, pipeline_mode=pl.Buffered(1))


def _warg(w):
    return w.arr if isinstance(w, _Layer) else w


def _sigmoid(x):
    return 0.5 * jnp.tanh(0.5 * x) + 0.5


def _silu(x):
    return x * _sigmoid(x)


def _rms(x, n, eps=1e-6):
    return x * lax.rsqrt(jnp.sum(x * x, axis=-1, keepdims=True) * (1.0 / n) + eps)


def _ada_kernel(cc_ref, w_ref, b_ref, o_ref):
    s = _silu(cc_ref[...])
    o_ref[...] = jnp.dot(s, w_ref[...], preferred_element_type=F32, precision=HIGHEST) + b_ref[...]


def ada_table(cc8, w_ada, b_ada):
    depth, d, n = w_ada.shape
    tn = n // 4
    return pl.pallas_call(
        _ada_kernel,
        out_shape=jax.ShapeDtypeStruct((depth, 8, n), F32),
        grid=(depth, n // tn),
        in_specs=[pl.BlockSpec((8, d), lambda l, j: (0, 0)),
                  pl.BlockSpec((None, d, tn), lambda l, j: (l, 0, j)),
                  pl.BlockSpec((None, 1, tn), lambda l, j: (l, 0, j))],
        out_specs=pl.BlockSpec((None, 8, tn), lambda l, j: (l, 0, j)),
        compiler_params=_cp("arbitrary", "arbitrary"),
        name="ada_table",
    )(cc8, w_ada, b_ada.reshape(depth, 1, n))


def _mod_spec(l, midx, col, d):
    return pl.BlockSpec((None, None, 1, d), lambda i: (l, midx(i), 0, col))


def _sgu_body(z, g_ref, b_ref, w_ref, bias_ref, o_ref):
    z = jax.nn.gelu(z)
    u, v = z[:, :BR_W], z[:, BR_W:]
    mu = jnp.mean(v, axis=-1, keepdims=True)
    vc = v - mu
    vn = vc * lax.rsqrt(jnp.mean(vc * vc, axis=-1, keepdims=True) + 1e-5) * g_ref[...] + b_ref[...]
    n_chunk = z.shape[0] // SGU_CHUNK
    half = LANE // 2
    lane = lax.broadcasted_iota(jnp.int32, (SGU_CHUNK, LANE), 1)
    for k in range(BR_W // LANE):
        cols = []
        for c in range(n_chunk):
            blk = vn[c * SGU_CHUNK:(c + 1) * SGU_CHUNK, k * LANE:(k + 1) * LANE]
            lo = jnp.where(lane < half, blk, 0.0)
            cols.append(jnp.concatenate([lo, blk - lo], axis=0))
        rhs = jnp.concatenate(cols, axis=1).astype(BF16)
        mixed = jnp.dot(w_ref[k], rhs, preferred_element_type=F32)
        for c in range(n_chunk):
            rows = slice(c * SGU_CHUNK, (c + 1) * SGU_CHUNK)
            m = mixed[:, c * LANE:(c + 1) * LANE] + bias_ref[:, k * LANE:(k + 1) * LANE]
            o_ref[rows, k * LANE:(k + 1) * LANE] = (u[rows, k * LANE:(k + 1) * LANE] * m).astype(o_ref.dtype)


def _in_kernel(x_ref, g_ref, sh_ref, sc_ref, w_s5, w_sgu, w_ssd, w_dt, w_mla, lng_ref, lnb_ref, wpair_ref, bias_ref,
               h_ref, o_s5, o_sgu, o_ssd, o_dt, o_mla):
    x = x_ref[...]
    y = _rms(x, x.shape[-1]) * g_ref[...]
    hb = (y * (1.0 + sc_ref[...]) + sh_ref[...]).astype(BF16)
    h_ref[...] = hb
    for w, o in ((w_s5, o_s5), (w_ssd, o_ssd), (w_dt, o_dt), (w_mla, o_mla)):
        o[...] = jnp.dot(hb, w[...], preferred_element_type=F32).astype(o.dtype)
    _sgu_body(jnp.dot(hb, w_sgu[...], preferred_element_type=F32), lng_ref, lnb_ref, wpair_ref, bias_ref, o_sgu)


def in_proj(x, g, mod4, l, midx, ws, out_dtypes, sgu_params):
    r, d = x.shape
    in_specs = [pl.BlockSpec((TM, d), lambda i: (i, 0)), _full((1, d)),
                _mod_spec(l, midx, 0, d), _mod_spec(l, midx, 1, d)]
    in_specs += [_wspec(w) for w in ws] + [_wspec(p) for p in sgu_params]
    out_shape = [jax.ShapeDtypeStruct((r, d), BF16)]
    out_specs = [pl.BlockSpec((TM, d), lambda i: (i, 0))]
    for n, (w, dt) in enumerate(zip(ws, out_dtypes)):
        width = BR_W if n == 1 else w.shape[1]
        out_shape.append(jax.ShapeDtypeStruct((r, width), dt))
        out_specs.append(pl.BlockSpec((TM, width), lambda i: (i, 0)))
    return pl.pallas_call(
        _in_kernel, out_shape=out_shape, grid=(r // TM,), in_specs=in_specs, out_specs=out_specs,
        compiler_params=_cp("parallel"), name="in_proj",
    )(x, g, mod4, mod4, *[_warg(w) for w in ws], *[_warg(p) for p in sgu_params])


def _s5_disc_kernel(are, aim, ldt, bre, bim, lam_re, lam_im, bbre, bbim):
    a_re, a_im = are[...], aim[...]
    dt = jnp.exp(ldt[...])
    mag = jnp.exp(a_re * dt)
    ang = a_im * dt
    ab_re = mag * jnp.cos(ang)
    ab_im = mag * jnp.sin(ang)
    den = a_re * a_re + a_im * a_im
    f_re = ((ab_re - 1.0) * a_re + ab_im * a_im) / den
    f_im = (ab_im * a_re - (ab_re - 1.0) * a_im) / den
    lam_re[...] = ab_re
    lam_im[...] = ab_im
    for c in range(S5_GROUP):
        bbre[c] = f_re * bre[c] - f_im * bim[c]
        bbim[c] = f_re * bim[c] + f_im * bre[c]


def s5_discretise(a_re, a_im, b_re, b_im, log_dt):
    shp = a_re.shape
    rows = int(np.prod(shp)) // LANE
    are = a_re.astype(F32).reshape(rows, LANE)
    aim = a_im.astype(F32).reshape(rows, LANE)
    ldt = jnp.broadcast_to(log_dt.astype(F32)[..., None], shp).reshape(rows, LANE)
    bre = jnp.moveaxis(b_re.astype(F32), -1, 0).reshape(S5_GROUP, rows, LANE)
    bim = jnp.moveaxis(b_im.astype(F32), -1, 0).reshape(S5_GROUP, rows, LANE)
    outs = pl.pallas_call(
        _s5_disc_kernel,
        out_shape=[jax.ShapeDtypeStruct((rows, LANE), F32)] * 2
        + [jax.ShapeDtypeStruct((S5_GROUP, rows, LANE), F32)] * 2,
        name="s5_discretise",
    )(are, aim, ldt, bre, bim)
    lam_re, lam_im = outs[0].reshape(shp), outs[1].reshape(shp)
    bb_re = jnp.moveaxis(outs[2].reshape((S5_GROUP,) + shp), 0, -1)
    bb_im = jnp.moveaxis(outs[3].reshape((S5_GROUP,) + shp), 0, -1)
    return lam_re, lam_im, bb_re, bb_im


def s5_pack(lam_re, lam_im, bb_re, bb_im, c_re, c_im):
    depth = lam_re.shape[0]
    gpb = LANE // S5_GROUP
    eye = jnp.eye(gpb, dtype=F32)

    def lam_blocks(v):
        return v.reshape(depth, 2, S5_JB, 1, S5_BW)

    lam = jnp.concatenate([lam_blocks(lam_re), lam_blocks(lam_im)], axis=-1)
    lam = jnp.broadcast_to(lam, (depth, 2, S5_JB, SUBLANE, 2 * S5_BW))

    def b_blocks(bb):
        v = bb.reshape(depth, 2, S5_JB, gpb, S5_STATE, S5_GROUP)
        return jnp.einsum('ldjgpc,gh->ldjgchp', v, eye).reshape(depth, 2, S5_JB, LANE, S5_BW)

    bblk = jnp.concatenate([b_blocks(bb_re), b_blocks(bb_im)], axis=-1).astype(BF16)

    def c_blocks(cc):
        v = cc.astype(F32).reshape(depth, 2, S5_JB, gpb, S5_GROUP, S5_STATE)
        return jnp.einsum('ldjgcp,gh->ldjhpgc', v, eye).reshape(depth, 2, S5_JB, S5_BW, LANE)

    cblk = jnp.concatenate([c_blocks(c_re), -c_blocks(c_im)], axis=-2).astype(BF16)
    return lam, bblk, cblk


def _cpow(re, im, n):
    out = None
    while n:
        if n & 1:
            out = (re, im) if out is None else (out[0] * re - out[1] * im, out[0] * im + out[1] * re)
        n >>= 1
        if n:
            re, im = re * re - im * im, 2.0 * re * im
    return out


def _s5_pass_kernel(*refs, tt, lseg, with_y, n_sub):
    if with_y:
        u_ref, bblk, lam, cblk, sloc, h0, y_ref, hfin, uperm, hst, yperm = refs[:11]
    else:
        u_ref, bblk, lam, s_out, uperm, hst = refs[:6]
    bus = refs[-n_sub * S5_JB:]
    d = pl.program_id(1)
    j = pl.program_id(2)
    offs = [pl.multiple_of((ph + d * (n_sub - 1 - 2 * ph)) * tt, tt) for ph in range(n_sub)]

    @pl.when(j == 0)
    def _():
        if not with_y:
            hst[...] = jnp.zeros(hst.shape, F32)
        else:
            for jb in range(S5_JB):
                pr, pi = _cpow(lam[jb, 0:1, :S5_BW], lam[jb, 0:1, S5_BW:], lseg)

                def chain(order, jb=jb, pr=pr, pi=pi):
                    cr, ci = h0[jb, :, :S5_BW], h0[jb, :, S5_BW:]
                    for s in order:
                        hst[jb, s:s + 1, :S5_BW] = cr
                        hst[jb, s:s + 1, S5_BW:] = ci
                        sr, si = sloc[jb, s:s + 1, :S5_BW], sloc[jb, s:s + 1, S5_BW:]
                        cr, ci = pr * cr - pi * ci + sr, pr * ci + pi * cr + si
                    hfin[jb, :, :S5_BW] = cr
                    hfin[jb, :, S5_BW:] = ci

                @pl.when(d == 0)
                def _():
                    chain(range(S5_NSEG))

                @pl.when(d == 1)
                def _():
                    chain(range(S5_NSEG - 1, -1, -1))

    def drive(ph):
        for s in range(S5_NSEG):
            us = u_ref[s, pl.ds(offs[ph], tt), :].astype(F32)
            for k in range(S5_JB):
                uperm[ph, k, pl.ds(s, tt, stride=S5_NSEG), :] = us[:, k * LANE:(k + 1) * LANE]
        for jb in range(S5_JB):
            bus[ph * S5_JB + jb][...] = jnp.dot(uperm[ph, jb].astype(BF16), bblk[jb], preferred_element_type=F32)

    def scan(ph):
        for jb in range(S5_JB):
            bu = bus[ph * S5_JB + jb]
            lr, li = lam[jb, :, :S5_BW], lam[jb, :, S5_BW:]

            def step(i, carry, lr=lr, li=li, bu=bu):
                hr, hi = carry
                t = i + d * (tt - 1 - 2 * i)
                r0 = pl.multiple_of(t * S5_NSEG, S5_NSEG)
                nr = lr * hr - li * hi + bu[pl.ds(r0, S5_NSEG), :S5_BW]
                ni = lr * hi + li * hr + bu[pl.ds(r0, S5_NSEG), S5_BW:]
                if with_y:
                    bu[pl.ds(r0, S5_NSEG), :S5_BW] = nr
                    bu[pl.ds(r0, S5_NSEG), S5_BW:] = ni
                return nr, ni

            hr, hi = lax.fori_loop(0, tt, step, (hst[jb, :, :S5_BW], hst[jb, :, S5_BW:]), unroll=True)
            hst[jb, :, :S5_BW] = hr
            hst[jb, :, S5_BW:] = hi

    def read(ph):
        for jb in range(S5_JB):
            yperm[ph, jb] = jnp.dot(bus[ph * S5_JB + jb][...].astype(BF16), cblk[jb],
                                    preferred_element_type=F32)
        for s in range(S5_NSEG):
            for k in range(S5_JB):
                y_ref[s, pl.ds(offs[ph], tt), k * LANE:(k + 1) * LANE] = (
                    yperm[ph, k, pl.ds(s, tt, stride=S5_NSEG), :].astype(y_ref.dtype))

    for ph in range(n_sub):
        drive(ph)
    for ph in range(n_sub):
        scan(ph)
        if with_y:
            read(ph)
    if not with_y:
        s_out[...] = hst[...]


def s5_scan(u, bblk, lam, cblk, l, bsz, h0):
    seq_len = u.shape[0] // bsz
    lseg = seq_len // S5_NSEG
    tt = min(lseg, 64)
    n_sub = next(n for n in (4, 2, 1) if (lseg // tt) % n == 0)
    nt = lseg // (tt * n_sub)
    u5 = u.reshape(bsz, S5_NSEG, lseg, BR_W)

    def tile(d, j):
        return j + d * (nt - 1 - 2 * j)

    u_spec = pl.BlockSpec((None, S5_NSEG, n_sub * tt, BR_W), lambda b, d, j: (b, 0, tile(d, j), 0))
    y_spec = pl.BlockSpec((None, None, S5_NSEG, n_sub * tt, BR_W), lambda b, d, j: (d, b, 0, tile(d, j), 0))
    w_b = pl.BlockSpec((None, None, S5_JB, LANE, 2 * S5_BW), lambda b, d, j: (l, d, 0, 0, 0))
    w_lam = pl.BlockSpec((None, None, S5_JB, SUBLANE, 2 * S5_BW), lambda b, d, j: (l, d, 0, 0, 0))
    w_c = pl.BlockSpec((None, None, S5_JB, 2 * S5_BW, LANE), lambda b, d, j: (l, d, 0, 0, 0))
    st8 = pl.BlockSpec((None, None, S5_JB, SUBLANE, 2 * S5_BW), lambda b, d, j: (b, d, 0, 0, 0))
    st1 = pl.BlockSpec((None, None, S5_JB, 1, 2 * S5_BW), lambda b, d, j: (b, d, 0, 0, 0))
    n_rows = S5_NSEG * tt
    scratch = [pltpu.VMEM((n_sub, S5_JB, n_rows, LANE), F32), pltpu.VMEM((S5_JB, SUBLANE, 2 * S5_BW), F32)]
    bus = [pltpu.VMEM((n_rows, 2 * S5_BW), F32)] * (n_sub * S5_JB)
    grid = (bsz, 2, nt)
    cp = _cp("arbitrary", "arbitrary", "arbitrary")

    sloc = pl.pallas_call(
        functools.partial(_s5_pass_kernel, tt=tt, lseg=lseg, with_y=False, n_sub=n_sub),
        out_shape=jax.ShapeDtypeStruct((bsz, 2, S5_JB, SUBLANE, 2 * S5_BW), F32),
        grid=grid, in_specs=[u_spec, w_b, w_lam], out_specs=st8,
        scratch_shapes=scratch + bus, compiler_params=cp, name="s5_local",
    )(u5, bblk, lam)

    y, hfin = pl.pallas_call(
        functools.partial(_s5_pass_kernel, tt=tt, lseg=lseg, with_y=True, n_sub=n_sub),
        out_shape=[jax.ShapeDtypeStruct((2, bsz, S5_NSEG, lseg, BR_W), BF16),
                   jax.ShapeDtypeStruct((bsz, 2, S5_JB, 1, 2 * S5_BW), F32)],
        grid=grid, in_specs=[u_spec, w_b, w_lam, w_c, st8, st1], out_specs=[y_spec, st1],
        scratch_shapes=scratch + [pltpu.VMEM((n_sub, S5_JB, n_rows, LANE), F32)] + bus,
        compiler_params=cp, name="s5_emit",
    )(u5, bblk, lam, cblk, sloc, h0)
    return y.reshape(2, bsz * seq_len, BR_W), hfin


def _ssd_prep_kernel(cur_ref, prev_ref, next_ref, dt_ref, w_ref, b_ref, dtb_ref, alane_ref, esel_ref, tri_ref,
                     xs_ref, bc_ref, ac_ref, act_ref, xt_ref, *, seq_len):
    i = pl.program_id(0)
    x = cur_ref[:, BR_W:].astype(F32)
    tm = x.shape[0]
    row = lax.broadcasted_iota(jnp.int32, (tm, 1), 0)
    pos = lax.rem(row + i * tm, seq_len)
    prev_row = prev_ref[SUBLANE - 1:SUBLANE, BR_W:].astype(F32)
    next_row = next_ref[0:1, BR_W:].astype(F32)
    x_prev = jnp.where(row == 0, prev_row, pltpu.roll(x, 1, axis=0))
    x_prev = jnp.where(pos == 0, 0.0, x_prev)
    x_next = jnp.where(row == tm - 1, next_row, pltpu.roll(x, tm - 1, axis=0))
    x_next = jnp.where(pos == seq_len - 1, 0.0, x_next)
    y = _silu(w_ref[0:1, :] * x_prev + w_ref[1:2, :] * x + w_ref[2:3, :] * x_next + b_ref[...])
    xs = y[:, :BR_W]
    xs_ref[...] = xs
    bc_ref[...] = y[:, BR_W:].astype(bc_ref.dtype)
    t = dt_ref[...] + dtb_ref[...]
    sp = jnp.maximum(t, 0.0) + jnp.log1p(jnp.exp(-jnp.abs(t)))
    lane = lax.broadcasted_iota(jnp.int32, t.shape, 1)
    d_a = sp * alane_ref[...]
    d_a = (jnp.where(lane < SSD_HEADS, d_a, 0.0),
           jnp.where(lane < SSD_HEADS, pltpu.roll(d_a, LANE - SSD_HEADS, axis=1), 0.0))
    for d in range(2):
        for c in range(tm // SSD_T):
            rows = slice(c * SSD_T, (c + 1) * SSD_T)
            a_cum = jnp.dot(tri_ref[d], d_a[d][rows], preferred_element_type=F32, precision=HIGHEST)
            ac_ref[d, rows, :] = a_cum
            act_ref[d, c] = a_cum.T[:SUBLANE, :]
    sp_hi = sp.astype(BF16)
    sp_lo = (sp - sp_hi.astype(F32)).astype(BF16)
    for d in range(2):
        dt_x = (jnp.dot(sp_hi, esel_ref[d], preferred_element_type=F32)
                + jnp.dot(sp_lo, esel_ref[d], preferred_element_type=F32))
        xt_ref[d] = (xs * dt_x).astype(xt_ref.dtype)


def ssd_prep(p_ssd, dt_raw, seq_len, conv_w, conv_b, dt_bias, a_lane, esel, tri):
    r, w = p_ssd.shape
    nb = r // SUBLANE
    per = TM // SUBLANE
    cpt = TM // SSD_T
    return pl.pallas_call(
        functools.partial(_ssd_prep_kernel, seq_len=seq_len),
        out_shape=[jax.ShapeDtypeStruct((r, BR_W), F32), jax.ShapeDtypeStruct((r, 2 * SSD_GN), BF16),
                   jax.ShapeDtypeStruct((2, r, LANE), F32),
                   jax.ShapeDtypeStruct((2, r // SSD_T, SUBLANE, SSD_T), F32),
                   jax.ShapeDtypeStruct((2, r, BR_W), BF16)],
        grid=(r // TM,),
        in_specs=[pl.BlockSpec((TM, w), lambda i: (i, 0)),
                  pl.BlockSpec((SUBLANE, w), lambda i: (jnp.maximum(i * per - 1, 0), 0)),
                  pl.BlockSpec((SUBLANE, w), lambda i: (jnp.minimum((i + 1) * per, nb - 1), 0)),
                  pl.BlockSpec((TM, LANE), lambda i: (i, 0)),
                  _full(conv_w.shape), _full(conv_b.shape), _full(dt_bias.shape), _full(a_lane.shape),
                  _full(esel.shape), _full(tri.shape)],
        out_specs=[pl.BlockSpec((TM, BR_W), lambda i: (i, 0)), pl.BlockSpec((TM, 2 * SSD_GN), lambda i: (i, 0)),
                   pl.BlockSpec((2, TM, LANE), lambda i: (0, i, 0)),
                   pl.BlockSpec((2, cpt, SUBLANE, SSD_T), lambda i: (0, i, 0, 0)),
                   pl.BlockSpec((2, TM, BR_W), lambda i: (0, i, 0))],
        compiler_params=_cp("parallel"), name="ssd_prep",
    )(p_ssd, p_ssd, p_ssd, dt_raw, conv_w, conv_b, dt_bias, a_lane, esel, tri)


def _ssd_scan_kernel(xt_ref, bc_ref, ac_ref, act_ref, tri_ref, h0_ref, y_ref, hfin_ref, st_ref, *, n_chunk, bsz):
    d = pl.program_id(0)
    j = pl.program_id(1)
    T = SSD_T

    @pl.when(j == 0)
    def _():
        st_ref[...] = h0_ref[...]

    tri = tri_ref[...]
    mask = tri > 0.5
    rep = SSD_HEADS // SSD_GROUPS
    low_lane = lax.broadcasted_iota(jnp.int32, (1, LANE), 1) < SSD_HEAD_DIM
    group_row = [(lax.broadcasted_iota(jnp.int32, (SSD_GN, 1), 0) // SSD_STATE) == g for g in range(SSD_GROUPS)]

    def chunk(ci, carry):
        c = ci + d * (n_chunk - 1 - 2 * ci)
        r0 = pl.multiple_of(c * T, T)
        for b in range(bsz):
            a_cum = ac_ref[b, pl.ds(r0, T), :]
            a_cum_t = act_ref[b, c]
            total = jnp.where(d == 0, a_cum[T - 1:T, :], a_cum[0:1, :])
            bc = bc_ref[b, pl.ds(r0, T), :]
            bm_t = bc[:, :SSD_GN].astype(F32).T
            cm = bc[:, SSD_GN:]
            b_grp = [jnp.where(group_row[g], bm_t, 0.0) for g in range(SSD_GROUPS)]
            scores = [jnp.dot(cm, b_grp[g].astype(BF16), preferred_element_type=F32)
                      for g in range(SSD_GROUPS)]
            for k in range(SSD_HEADS // 2):
                x_pair = xt_ref[b, pl.ds(r0, T), k * LANE:(k + 1) * LANE]
                y_d, s_n, e_col, e_tot = [], [], [], []
                for h in (2 * k, 2 * k + 1):
                    g = h // rep
                    col = a_cum[:, h:h + 1]
                    rowv = a_cum_t[h:h + 1, :]
                    tot = total[:, h:h + 1]
                    decay = jnp.exp(jnp.where(mask, col - rowv, -1e30))
                    y_d.append(jnp.dot((scores[g] * decay).astype(BF16), x_pair, preferred_element_type=F32))
                    bw = (b_grp[g] * jnp.exp(tot - rowv)).astype(BF16)
                    s_n.append(jnp.dot(bw, x_pair, preferred_element_type=F32))
                    e_col.append(jnp.exp(col))
                    e_tot.append(jnp.exp(tot))
                s_old = st_ref[b, k]
                y_off = jnp.dot(cm, s_old.astype(BF16), preferred_element_type=F32)
                y = (jnp.where(low_lane, y_d[0], y_d[1])
                     + y_off * jnp.where(low_lane, e_col[0], e_col[1]))
                y_ref[b, pl.ds(r0, T), k * LANE:(k + 1) * LANE] = y.astype(y_ref.dtype)
                st_ref[b, k] = (s_old * jnp.where(low_lane, e_tot[0], e_tot[1])
                                + jnp.where(low_lane, s_n[0], s_n[1]))
        return carry

    lax.fori_loop(0, n_chunk, chunk, 0, unroll=True)
    hfin_ref[...] = st_ref[...]


def ssd_scan(xt, bc, a_cum, a_cum_t, tri, bsz, h0):
    r = bc.shape[0]
    seq_len = r // bsz
    ts = min(seq_len, TM)
    nt = seq_len // ts

    def tile(d, j):
        return j + d * (nt - 1 - 2 * j)

    st_shape = (bsz, SSD_HEADS // 2, SSD_GN, LANE)
    st = pl.BlockSpec((None,) + st_shape, lambda d, j: (d, 0, 0, 0, 0))
    y, hfin = pl.pallas_call(
        functools.partial(_ssd_scan_kernel, n_chunk=ts // SSD_T, bsz=bsz),
        out_shape=[jax.ShapeDtypeStruct((2, bsz, seq_len, BR_W), BF16),
                   jax.ShapeDtypeStruct((2,) + st_shape, F32)],
        grid=(2, nt),
        in_specs=[pl.BlockSpec((None, bsz, ts, BR_W), lambda d, j: (d, 0, tile(d, j), 0)),
                  pl.BlockSpec((bsz, ts, 2 * SSD_GN), lambda d, j: (0, tile(d, j), 0)),
                  pl.BlockSpec((None, bsz, ts, LANE), lambda d, j: (d, 0, tile(d, j), 0)),
                  pl.BlockSpec((None, bsz, ts // SSD_T, SUBLANE, SSD_T), lambda d, j: (d, 0, tile(d, j), 0, 0)),
                  pl.BlockSpec((None, SSD_T, SSD_T), lambda d, j: (d, 0, 0)), st],
        out_specs=[pl.BlockSpec((None, bsz, ts, BR_W), lambda d, j: (d, 0, tile(d, j), 0)), st],
        scratch_shapes=[pltpu.VMEM(st_shape, F32)],
        compiler_params=_cp("arbitrary", "arbitrary"), name="ssd_scan",
    )(xt.reshape(2, bsz, seq_len, BR_W), bc.reshape(bsz, seq_len, 2 * SSD_GN),
      a_cum.reshape(2, bsz, seq_len, LANE), a_cum_t.reshape(2, bsz, seq_len // SSD_T, SUBLANE, SSD_T), tri, h0)
    return y.reshape(2, r, BR_W), hfin


def _rope_table_kernel(frow_ref, fcol_ref, cos_ref, sin_ref):
    tr = cos_ref.shape[0]
    pos = (lax.broadcasted_iota(jnp.int32, (tr, 1), 0) + pl.program_id(0) * tr).astype(F32)
    pos_row = jnp.floor((pos + 0.5) * (1.0 / GRID_W))
    pos_col = pos - pos_row * GRID_W
    ang = pos_row * frow_ref[...] + pos_col * fcol_ref[...]
    lane = lax.broadcasted_iota(jnp.int32, ang.shape, 1)
    quarter = MLA_ROPE // 2
    first = (lane >= MLA_NOPE) & (lane < MLA_NOPE + quarter)
    second = (lane >= MLA_NOPE + quarter) & (lane < MLA_QK)
    c, s = jnp.cos(ang), jnp.sin(ang)
    cos_ref[...] = jnp.where(first | second, c, jnp.where(lane < MLA_NOPE, 1.0, 0.0))
    sin_ref[...] = jnp.where(first, -s, jnp.where(second, s, 0.0))


def _rope_partner(v):
    quarter = MLA_ROPE // 2
    src = np.arange(LANE)
    src[MLA_NOPE:MLA_NOPE + quarter] += quarter
    src[MLA_NOPE + quarter:MLA_QK] -= quarter
    valid = (np.arange(LANE) >= MLA_NOPE) & (np.arange(LANE) < MLA_QK)
    return jnp.where(valid, v[..., src], 0)


def rope_tables(seq_len):
    pairs = MLA_ROPE // 4
    inv_freq = ROPE_BASE ** (-jnp.arange(pairs, dtype=F32) / pairs)
    zeros = lambda n: jnp.zeros((n,), F32)
    half = jnp.concatenate([inv_freq, zeros(pairs)])
    f_row = jnp.concatenate([zeros(MLA_NOPE), half, half, zeros(LANE - MLA_QK)]).reshape(1, LANE)
    half = jnp.concatenate([zeros(pairs), inv_freq])
    f_col = jnp.concatenate([zeros(MLA_NOPE), half, half, zeros(LANE - MLA_QK)]).reshape(1, LANE)
    tr = min(seq_len, 1024)
    spec = pl.BlockSpec((tr, LANE), lambda i: (i, 0))
    return pl.pallas_call(
        _rope_table_kernel, out_shape=[jax.ShapeDtypeStruct((seq_len, LANE), F32)] * 2,
        grid=(seq_len // tr,), in_specs=[_full((1, LANE))] * 2, out_specs=[spec] * 2,
        compiler_params=_cp("parallel"), name="rope_tables",
    )(f_row, f_col)


def _mla_prep_kernel(*refs, rope):
    if rope:
        (p_ref, gq_ref, gkv_ref, wq_ref, wkv_ref, nq_ref, nk_ref, wqs_ref, nqs_ref, nks_ref, cos_ref, sin_ref,
         q_ref, k_ref, v_ref) = refs
    else:
        p_ref, gq_ref, gkv_ref, wq_ref, wkv_ref, nq_ref, nk_ref, q_ref, k_ref, v_ref = refs
    p = p_ref[...].astype(F32)
    cq = (_rms(p[:, :MLA_Q_LORA], MLA_Q_LORA) * gq_ref[...]).astype(BF16)
    ckv = (_rms(p[:, MLA_Q_LORA:MLA_Q_LORA + MLA_KV_LORA], MLA_KV_LORA) * gkv_ref[...]).astype(BF16)
    kr = p[:, MLA_Q_LORA + MLA_KV_LORA:MLA_Q_LORA + MLA_KV_LORA + LANE]
    q_all = jnp.dot(cq, wq_ref[...], preferred_element_type=F32)
    kv_all = jnp.dot(ckv, wkv_ref[...], preferred_element_type=F32)
    lane = lax.broadcasted_iota(jnp.int32, (1, LANE), 1)
    one_col = jnp.where(lane == MLA_V, 1.0, 0.0)
    scale = MLA_QK ** -0.5 * LOG2E
    if rope:
        qs_all = jnp.dot(cq, wqs_ref[...], preferred_element_type=F32)
        q_cos, q_sin = nq_ref[...] * cos_ref[...], nqs_ref[...] * sin_ref[...]
        k_cos = nk_ref[...] * cos_ref[...]
        k_part = p[:, MLA_Q_LORA + MLA_KV_LORA + LANE:] * (nks_ref[...] * sin_ref[...])

    def inv_rms(t):
        return lax.rsqrt(jnp.sum(t * t, axis=-1, keepdims=True) * (1.0 / MLA_QK) + 1e-6)

    for h in range(MLA_HEADS):
        q = q_all[:, h * LANE:(h + 1) * LANE]
        k = kv_all[:, 2 * h * LANE:(2 * h + 1) * LANE] + kr
        if rope:
            q_out = (q * q_cos + qs_all[:, h * LANE:(h + 1) * LANE] * q_sin) * (inv_rms(q) * scale)
            k_out = (k * k_cos + k_part) * inv_rms(k)
        else:
            q_out = q * nq_ref[...] * (inv_rms(q) * scale)
            k_out = k * nk_ref[...] * inv_rms(k)
        q_ref[h] = q_out.astype(q_ref.dtype)
        k_ref[h] = k_out.astype(k_ref.dtype)
        v_ref[h] = (kv_all[:, (2 * h + 1) * LANE:(2 * h + 2) * LANE] + one_col).astype(v_ref.dtype)


def mla_prep(p_mla, bsz, gq, gkv, wq, wkv, nq, nk, rope_args):
    r, w = p_mla.shape
    seq_len = r // bsz
    tr = min(seq_len, TM)
    nt = seq_len // tr
    rope = rope_args is not None
    in_specs = [pl.BlockSpec((tr, w), lambda b, i: (b * nt + i, 0)), _full(gq.shape), _full(gkv.shape),
                _wspec(wq), _wspec(wkv), _full(nq.shape), _full(nk.shape)]
    args = [p_mla, gq, gkv, _warg(wq), _warg(wkv), nq, nk]
    if rope:
        in_specs += [_wspec(a) for a in rope_args[:3]]
        in_specs += [pl.BlockSpec((tr, LANE), lambda b, i: (i, 0))] * 2
        args += [_warg(a) for a in rope_args]
    head = pl.BlockSpec((None, MLA_HEADS, tr, LANE), lambda b, i: (b, 0, i, 0))
    return pl.pallas_call(
        functools.partial(_mla_prep_kernel, rope=rope),
        out_shape=[jax.ShapeDtypeStruct((bsz, MLA_HEADS, seq_len, LANE), BF16)] * 3,
        grid=(bsz, nt), in_specs=in_specs, out_specs=[head] * 3,
        compiler_params=_cp("parallel", "parallel"), name="mla_prep",
    )(*args)


def _attn_kernel(*refs, n_kv):
    bound_ref, q_ref = refs[0], refs[1]
    kv_refs = refs[2:2 + 2 * n_kv]
    o_ref = refs[2 + 2 * n_kv]
    m_ref = refs[3 + 2 * n_kv]
    scores = [[lax.dot_general(q_ref[hh], kv_refs[2 * i][hh], (((1,), (1,)), ((), ())),
                               preferred_element_type=F32) for i in range(n_kv)] for hh in range(2)]

    def row_max(first_set):
        for hh in range(2):
            m = scores[hh][n_kv - 1].max(axis=-1, keepdims=True)
            for si in scores[hh][first_set:n_kv - 1]:
                m = jnp.maximum(m, si.max(axis=-1, keepdims=True))
            m_ref[hh] = m

    if n_kv > 1:
        fast = bound_ref[0] <= ATTN_FAST_BOUND

        @pl.when(fast)
        def _():
            row_max(n_kv - 1)

        @pl.when(jnp.logical_not(fast))
        def _():
            row_max(0)
    else:
        row_max(0)

    for hh in range(2):
        m = m_ref[hh]
        acc = None
        for i, si in enumerate(scores[hh]):
            pv = jnp.dot(jnp.exp2((si - m).astype(BF16)), kv_refs[2 * i + 1][hh], preferred_element_type=F32)
            acc = pv if acc is None else acc + pv
        o = acc[:, :MLA_V] / acc[:, MLA_V:MLA_V + 1]
        o_ref[:, hh * MLA_V:(hh + 1) * MLA_V] = o.astype(o_ref.dtype)


def attention(q, kvs, score_bound):
    bsz, nh, lq, _ = q.shape
    tq = min(lq, TQ)
    nq = lq // tq
    in_specs = [pl.BlockSpec(memory_space=pltpu.SMEM),
                pl.BlockSpec((None, 2, tq, LANE), lambda b, hp, i: (b, hp, i, 0))]
    args = [score_bound, q]
    for k, v in kvs:
        spec = pl.BlockSpec((None, 2, k.shape[2], LANE), lambda b, hp, i: (b, hp, 0, 0))
        in_specs += [spec, spec]
        args += [k, v]
    return pl.pallas_call(
        functools.partial(_attn_kernel, n_kv=len(kvs)),
        out_shape=jax.ShapeDtypeStruct((bsz * lq, nh * MLA_V), BF16),
        grid=(bsz, nh // 2, nq), in_specs=in_specs,
        out_specs=pl.BlockSpec((tq, 2 * MLA_V), lambda b, hp, i: (b * nq + i, hp)),
        scratch_shapes=[pltpu.VMEM((2, tq, 1), F32)],
        compiler_params=_cp("parallel", "parallel", "arbitrary"), name="attention",
    )(*args)


def _merge_kernel(x_ref, h_ref, u_ref, s5f_ref, s5b_ref, s5d_ref, wglu_ref, b_ref,
                  xs_ref, ssdf_ref, ssdb_ref, z_ref, ssdd_ref, ssdg_ref, d_ref,
                  g1_ref, wg_ref, wb_ref, wo_ref,
                  n2_ref, sh2_ref, sc2_ref, g2_ref, wi_ref, wfo_ref, o_ref, *, n_split):
    y = s5d_ref[...] * u_ref[...].astype(F32) + s5f_ref[...].astype(F32) + s5b_ref[...].astype(F32)
    g = jax.nn.gelu(y)
    a = (g * _sigmoid(jnp.dot(g.astype(BF16), wglu_ref[...], preferred_element_type=F32))).astype(BF16)
    y = ssdd_ref[...] * xs_ref[...] + ssdf_ref[...].astype(F32) + ssdb_ref[...].astype(F32)
    c = (_rms(y * _silu(z_ref[...].astype(F32)), BR_W) * ssdg_ref[...]).astype(BF16)
    h = h_ref[...]
    dm = x_ref.shape[-1]
    merged = None
    for i, br in enumerate((a, b_ref[...], c, d_ref[...])):
        gate = _sigmoid(jnp.dot(h, wg_ref[:, i * dm:(i + 1) * dm], preferred_element_type=F32))
        term = gate * jnp.dot(br, wb_ref[i], preferred_element_type=F32)
        merged = term if merged is None else merged + term
    mix = jnp.dot(merged.astype(BF16), wo_ref[...], preferred_element_type=F32)
    x = x_ref[...] + g1_ref[...] * mix

    y = _rms(x, dm) * n2_ref[...]
    hb = (y * (1.0 + sc2_ref[...]) + sh2_ref[...]).astype(BF16)
    hid = wfo_ref.shape[0]
    step = hid // n_split
    acc = None
    for c in range(n_split):
        gate = jnp.dot(hb, wi_ref[:, c * step:(c + 1) * step], preferred_element_type=F32)
        up = jnp.dot(hb, wi_ref[:, hid + c * step:hid + (c + 1) * step], preferred_element_type=F32)
        part = jnp.dot((_silu(gate) * up).astype(BF16), wfo_ref[c * step:(c + 1) * step, :],
                       preferred_element_type=F32)
        acc = part if acc is None else acc + part
    o_ref[...] = x + g2_ref[...] * acc


def merge_ffn(x, h, s5_in, b, ssd_in, d, mod4, l, midx, wg, wb, wo, norm2, wi, wfo):
    r, dm = x.shape
    n_split = 11 if wfo.shape[0] % (11 * LANE) == 0 else 1
    row = pl.BlockSpec((TM, dm), lambda i: (i, 0))
    br = pl.BlockSpec((TM, BR_W), lambda i: (i, 0))
    fwd = pl.BlockSpec((None, TM, BR_W), lambda i: (0, i, 0))
    bwd = pl.BlockSpec((None, TM, BR_W), lambda i: (1, i, 0))
    vec = _full((1, BR_W))
    u, y_s5, s5_d, w_glu = s5_in
    xs, y_ssd, p_ssd, ssd_d, ssd_g = ssd_in
    return pl.pallas_call(
        functools.partial(_merge_kernel, n_split=n_split),
        out_shape=jax.ShapeDtypeStruct((r, dm), F32), grid=(r // TM,),
        in_specs=[row, row, br, fwd, bwd, vec, _wspec(w_glu), br,
                  br, fwd, bwd, br, vec, vec, br,
                  _mod_spec(l, midx, 2, dm), _wspec(wg), _wspec(wb), _wspec(wo),
                  _full((1, dm)), _mod_spec(l, midx, 3, dm), _mod_spec(l, midx, 4, dm),
                  _mod_spec(l, midx, 5, dm), _wspec(wi), _wspec(wfo)],
        out_specs=row, compiler_params=_cp("parallel"), name="merge_ffn",
    )(x, h, u, y_s5, y_s5, s5_d, _warg(w_glu), b, xs, y_ssd, y_ssd, p_ssd, ssd_d, ssd_g, d, mod4,
      _warg(wg), _warg(wb), _warg(wo), norm2, mod4, mod4, mod4, _warg(wi), _warg(wfo))


def _pad_cols(w, n):
    return jnp.pad(w, ((0, 0),) * (w.ndim - 1) + ((0, n - w.shape[-1]),))


def kernel(x, c, ctx, c_ctx, w_ada, b_ada, norm1_g, norm2_g, w_in, s5_a_re, s5_a_im, s5_b_re, s5_b_im, s5_c_re, s5_c_im, s5_log_dt, s5_d, s5_w_glu, sgu_ln_g, sgu_ln_b, sgu_w_s, sgu_b_s, ssd_conv_w, ssd_conv_b, ssd_a_log, ssd_dt_bias, ssd_d, ssd_norm_g, mla_q_a_norm, mla_w_uq, mla_kv_a_norm, mla_w_ukv, mla_q_norm, mla_k_norm, w_branch, w_out, w_ffn_in, w_ffn_out):
    bsz, seq, dm = x.shape
    lc = ctx.shape[1]
    depth = w_ada.shape[0]
    assert seq % TM == 0 and (bsz * lc) % TM == 0 and seq % (S5_NSEG * SUBLANE) == 0
    assert lc % SSD_T == 0 and lc % (S5_NSEG * SUBLANE) == 0 and bsz + 1 <= 8

    cc8 = jnp.zeros((8, dm), F32).at[:bsz].set(c.astype(F32)).at[bsz].set(c_ctx.astype(F32))
    mod4 = ada_table(cc8, w_ada.astype(F32), b_ada.astype(F32)).reshape(depth, 8, 1, 6 * dm)
    lat_tiles = seq // TM
    midx_lat = lambda i: i // lat_tiles
    midx_ctx = lambda i: bsz

    off = np.cumsum([0, BR_W, 2 * BR_W, BR_W + SSD_CONV_CH + 2 * SSD_HEADS,
                     MLA_Q_LORA + MLA_KV_LORA + MLA_ROPE, N_BRANCH * dm])
    w_s5 = w_in[:, :, off[0]:off[1]].astype(BF16)
    w_sgu = w_in[:, :, off[1]:off[2]].astype(BF16)
    w_ssd = w_in[:, :, off[2]:off[2] + BR_W + SSD_CONV_CH].astype(BF16)
    w_dt = _pad_cols(w_in[:, :, off[2] + BR_W + SSD_CONV_CH:off[3]], LANE).astype(BF16)
    w_mla_main = w_in[:, :, off[3]:off[3] + MLA_Q_LORA + MLA_KV_LORA]
    w_kr = w_in[:, :, off[3] + MLA_Q_LORA + MLA_KV_LORA:off[4]]
    zeros = lambda n: jnp.zeros((depth, dm, n), w_in.dtype)
    kr_block = jnp.concatenate([zeros(MLA_NOPE), w_kr, zeros(LANE - MLA_QK)], axis=-1)
    w_mla = jnp.concatenate([w_mla_main, kr_block, _rope_partner(kr_block)], axis=-1).astype(BF16)
    w_gate = w_in[:, :, off[4]:off[5]].astype(BF16)

    lam_re, lam_im, bb_re, bb_im = s5_discretise(s5_a_re, s5_a_im, s5_b_re, s5_b_im, s5_log_dt)
    lam, bblk, cblk = s5_pack(lam_re, lam_im, bb_re, bb_im, s5_c_re, s5_c_im)
    s5_wg = s5_w_glu.astype(BF16)

    sgu_w = sgu_w_s.reshape(depth, SGU_HEADS // 2, 2, SGU_CHUNK, SGU_CHUNK)
    sgu_w = jnp.concatenate([sgu_w[:, :, 0], sgu_w[:, :, 1]], axis=-1).astype(BF16)
    sgu_b = jnp.repeat(jnp.swapaxes(sgu_b_s, 1, 2), BR_W // SGU_HEADS, axis=-1).astype(F32)

    a_neg = -jnp.exp(ssd_a_log.astype(F32))
    a_lane = _pad_cols(a_neg.reshape(depth, 1, 2 * SSD_HEADS), LANE)
    head_of_lane = jnp.arange(LANE)[None, :, None] - SSD_HEADS * jnp.arange(2)[:, None, None]
    col_head = jnp.arange(BR_W) // SSD_HEAD_DIM
    esel = ((head_of_lane == col_head[None, None, :]) & (head_of_lane >= 0)).astype(BF16)
    ti = jnp.arange(SSD_T)
    tri = jnp.stack([ti[None, :] <= ti[:, None], ti[None, :] >= ti[:, None]]).astype(F32)
    dt_bias = _pad_cols(ssd_dt_bias.astype(F32).reshape(depth, 1, 2 * SSD_HEADS), LANE)
    ssd_dskip = jnp.repeat(ssd_d.astype(F32), SSD_HEAD_DIM, axis=-1)[:, None, :]

    wq = mla_w_uq.reshape(depth, MLA_Q_LORA, MLA_HEADS, MLA_QK)
    wq = _pad_cols(wq, LANE)
    wq_sw = _rope_partner(wq).reshape(depth, MLA_Q_LORA, MLA_HEADS * LANE).astype(BF16)
    wq = wq.reshape(depth, MLA_Q_LORA, MLA_HEADS * LANE).astype(BF16)
    wkv = mla_w_ukv.reshape(depth, MLA_KV_LORA, MLA_HEADS, 2, MLA_NOPE)
    wkv = _pad_cols(wkv, LANE).reshape(depth, MLA_KV_LORA, MLA_HEADS * 2 * LANE).astype(BF16)
    nq = _pad_cols(mla_q_norm.astype(F32), LANE)[:, None, :]
    nk = _pad_cols(mla_k_norm.astype(F32), LANE)[:, None, :]
    nq_sw, nk_sw = _rope_partner(nq), _rope_partner(nk)
    rope_cos, rope_sin = rope_tables(seq)
    score_bound = (MLA_QK ** 0.5 * LOG2E * jnp.max(jnp.abs(mla_q_norm.astype(F32)), axis=-1)
                   * jnp.max(jnp.abs(mla_k_norm.astype(F32)), axis=-1))

    wb = w_branch.astype(BF16)
    wo = w_out.astype(BF16)
    wfi = w_ffn_in.astype(BF16)
    wfo = w_ffn_out.astype(BF16)

    row = lambda v, l: v[l].astype(F32).reshape(1, -1)
    x_lat = x.astype(F32).reshape(bsz * seq, dm)
    x_ctx = ctx.astype(F32).reshape(bsz * lc, dm)
    s5_zero = jnp.zeros((bsz, 2, S5_JB, 1, 2 * S5_BW), F32)
    ssd_zero = jnp.zeros((2, bsz, SSD_HEADS // 2, SSD_GN, LANE), F32)

    for l in range(depth):
        need_ctx = l < depth - 1
        lay = lambda w: _Layer(w, l)
        ws = tuple(lay(w) for w in (w_s5, w_sgu, w_ssd, w_dt, w_mla))
        w_merge = (lay(w_gate), lay(wb), lay(wo))
        w_ffn = (lay(wfi), lay(wfo))
        dts = (BF16, BF16, BF16, F32, BF16)
        g1 = row(norm1_g, l)
        sgu_p = (row(sgu_ln_g, l), row(sgu_ln_b, l), lay(sgu_w), lay(sgu_b))
        h_l, u_l, b_l, p_l, dtr_l, m_l = in_proj(x_lat, g1, mod4, l, midx_lat, ws, dts, sgu_p)
        h_c, u_c, b_c, p_c, dtr_c, m_c = in_proj(x_ctx, g1, mod4, l, midx_ctx, ws, dts, sgu_p)

        y_c, s5_h = s5_scan(u_c, bblk, lam, cblk, l, bsz, s5_zero)
        y_l, _ = s5_scan(u_l, bblk, lam, cblk, l, bsz, s5_h)
        s5_tail = (row(s5_d, l), lay(s5_wg))
        conv_w, conv_b = ssd_conv_w[l].astype(F32), row(ssd_conv_b, l)
        ssd_w = (conv_w, conv_b, dt_bias[l], a_lane[l], esel, tri)
        xs_c, bc_c, ac_c, act_c, xt_c = ssd_prep(p_c, dtr_c, lc, *ssd_w)
        xs_l, bc_l, ac_l, act_l, xt_l = ssd_prep(p_l, dtr_l, seq, *ssd_w)
        yc_c, ssd_h = ssd_scan(xt_c, bc_c, ac_c, act_c, tri, bsz, ssd_zero)
        yc_l, _ = ssd_scan(xt_l, bc_l, ac_l, act_l, tri, bsz, ssd_h)
        ssd_tail = (ssd_dskip[l], row(ssd_norm_g, l))
        mla_w = (row(mla_q_a_norm, l), row(mla_kv_a_norm, l), lay(wq), lay(wkv), nq[l], nk[l])
        q_c, k_c, v_c = mla_prep(m_c, bsz, *mla_w, None)
        q_l, k_l, v_l = mla_prep(m_l, bsz, *mla_w, (lay(wq_sw), nq_sw[l], nk_sw[l], rope_cos, rope_sin))
        d_l = attention(q_l, [(k_l, v_l), (k_c, v_c)], score_bound[l:l + 1])

        x_lat = merge_ffn(x_lat, h_l, (u_l, y_l) + s5_tail, b_l, (xs_l, yc_l, p_l) + ssd_tail, d_l,
                          mod4, l, midx_lat, *w_merge, row(norm2_g, l), *w_ffn)
        if need_ctx:
            d_c = attention(q_c, [(k_c, v_c)], score_bound[l:l + 1])
            x_ctx = merge_ffn(x_ctx, h_c, (u_c, y_c) + s5_tail, b_c, (xs_c, yc_c, p_c) + ssd_tail, d_c,
                              mod4, l, midx_ctx, *w_merge, row(norm2_g, l), *w_ffn)
    return x_lat.reshape(bsz, seq, dm).astype(x.dtype)
```

```python
import functools

import jax
import jax.numpy as jnp
import numpy as np
from jax import lax
from jax.experimental import pallas as pl
from jax.experimental.pallas import tpu as pltpu

F32 = jnp.float32
BF16 = jnp.bfloat16
HIGHEST = lax.Precision.HIGHEST

LANE = 128
SUBLANE = 8
VMEM_LIMIT = 56 * 1024 * 1024

GRID_W = 64
BR_W = 384
S5_GROUP = 16
S5_GROUPS = BR_W // S5_GROUP
S5_STATE = 64
S5_NSEG = SUBLANE
S5_JB = BR_W // LANE
S5_BW = (LANE // S5_GROUP) * S5_STATE
SGU_CHUNK = 128
SGU_HEADS = 6
SSD_HEADS = 6
SSD_HEAD_DIM = 64
SSD_GROUPS = 2
SSD_STATE = 64
SSD_GN = SSD_GROUPS * SSD_STATE
SSD_CONV_CH = BR_W + 2 * SSD_GN
SSD_T = 128
MLA_HEADS = 6
MLA_NOPE = 64
MLA_ROPE = 32
MLA_V = 64
MLA_QK = MLA_NOPE + MLA_ROPE
MLA_Q_LORA = 384
MLA_KV_LORA = 256
ROPE_BASE = 10000.0
LOG2E = 1.4426950408889634
N_BRANCH = 4

TM = 512
TQ = 256
ATTN_FAST_BOUND = 30.0


def _cp(*sem):
    return pltpu.CompilerParams(dimension_semantics=sem, vmem_limit_bytes=VMEM_LIMIT)


def _full(shape):
    n = len(shape)
    return pl.BlockSpec(shape, lambda *_: (0,) * n)


class _Layer:
    def __init__(self, arr, l):
        self.arr, self.l = arr, l

    @property
    def shape(self):
        return self.arr.shape[1:]


def _wspec(w):
    if not isinstance(w, _Layer):
        return _full(w.shape)
    n, l = len(w.shape), w.l
    return pl.BlockSpec((None,) + tuple(w.shape), lambda *_: (l,) + (0,) * n, pipeline_mode=pl.Buffered(1))


def _warg(w):
    return w.arr if isinstance(w, _Layer) else w


def _sigmoid(x):
    return 0.5 * jnp.tanh(0.5 * x) + 0.5


def _silu(x):
    return x * _sigmoid(x)


def _rms(x, n, eps=1e-6):
    return x * lax.rsqrt(jnp.sum(x * x, axis=-1, keepdims=True) * (1.0 / n) + eps)


def _ada_kernel(cc_ref, w_ref, b_ref, o_ref):
    s = _silu(cc_ref[...])
    o_ref[...] = jnp.dot(s, w_ref[...], preferred_element_type=F32, precision=HIGHEST) + b_ref[...]


def ada_table(cc8, w_ada, b_ada):
    depth, d, n = w_ada.shape
    tn = n // 4
    return pl.pallas_call(
        _ada_kernel,
        out_shape=jax.ShapeDtypeStruct((depth, 8, n), F32),
        grid=(depth, n // tn),
        in_specs=[pl.BlockSpec((8, d), lambda l, j: (0, 0)),
                  pl.BlockSpec((None, d, tn), lambda l, j: (l, 0, j)),
                  pl.BlockSpec((None, 1, tn), lambda l, j: (l, 0, j))],
        out_specs=pl.BlockSpec((None, 8, tn), lambda l, j: (l, 0, j)),
        compiler_params=_cp("arbitrary", "arbitrary"),
        name="ada_table",
    )(cc8, w_ada, b_ada.reshape(depth, 1, n))


def _mod_spec(l, midx, col, d):
    return pl.BlockSpec((None, None, 1, d), lambda i: (l, midx(i), 0, col))


def _sgu_body(z, g_ref, b_ref, w_ref, bias_ref, o_ref):
    z = jax.nn.gelu(z)
    u, v = z[:, :BR_W], z[:, BR_W:]
    mu = jnp.mean(v, axis=-1, keepdims=True)
    vc = v - mu
    vn = vc * lax.rsqrt(jnp.mean(vc * vc, axis=-1, keepdims=True) + 1e-5) * g_ref[...] + b_ref[...]
    n_chunk = z.shape[0] // SGU_CHUNK
    half = LANE // 2
    lane = lax.broadcasted_iota(jnp.int32, (SGU_CHUNK, LANE), 1)
    for k in range(BR_W // LANE):
        cols = []
        for c in range(n_chunk):
            blk = vn[c * SGU_CHUNK:(c + 1) * SGU_CHUNK, k * LANE:(k + 1) * LANE]
            lo = jnp.where(lane < half, blk, 0.0)
            cols.append(jnp.concatenate([lo, blk - lo], axis=0))
        rhs = jnp.concatenate(cols, axis=1).astype(BF16)
        mixed = jnp.dot(w_ref[k], rhs, preferred_element_type=F32)
        for c in range(n_chunk):
            rows = slice(c * SGU_CHUNK, (c + 1) * SGU_CHUNK)
            m = mixed[:, c * LANE:(c + 1) * LANE] + bias_ref[:, k * LANE:(k + 1) * LANE]
            o_ref[rows, k * LANE:(k + 1) * LANE] = (u[rows, k * LANE:(k + 1) * LANE] * m).astype(o_ref.dtype)


def _in_kernel(x_ref, g_ref, sh_ref, sc_ref, w_s5, w_sgu, w_ssd, w_dt, w_mla, lng_ref, lnb_ref, wpair_ref, bias_ref,
               h_ref, o_s5, o_sgu, o_ssd, o_dt, o_mla):
    x = x_ref[...]
    y = _rms(x, x.shape[-1]) * g_ref[...]
    hb = (y * (1.0 + sc_ref[...]) + sh_ref[...]).astype(BF16)
    h_ref[...] = hb
    for w, o in ((w_s5, o_s5), (w_ssd, o_ssd), (w_dt, o_dt), (w_mla, o_mla)):
        o[...] = jnp.dot(hb, w[...], preferred_element_type=F32).astype(o.dtype)
    _sgu_body(jnp.dot(hb, w_sgu[...], preferred_element_type=F32), lng_ref, lnb_ref, wpair_ref, bias_ref, o_sgu)


def in_proj(x, g, mod4, l, midx, ws, out_dtypes, sgu_params):
    r, d = x.shape
    in_specs = [pl.BlockSpec((TM, d), lambda i: (i, 0)), _full((1, d)),
                _mod_spec(l, midx, 0, d), _mod_spec(l, midx, 1, d)]
    in_specs += [_wspec(w) for w in ws] + [_wspec(p) for p in sgu_params]
    out_shape = [jax.ShapeDtypeStruct((r, d), BF16)]
    out_specs = [pl.BlockSpec((TM, d), lambda i: (i, 0))]
    for n, (w, dt) in enumerate(zip(ws, out_dtypes)):
        width = BR_W if n == 1 else w.shape[1]
        out_shape.append(jax.ShapeDtypeStruct((r, width), dt))
        out_specs.append(pl.BlockSpec((TM, width), lambda i: (i, 0)))
    return pl.pallas_call(
        _in_kernel, out_shape=out_shape, grid=(r // TM,), in_specs=in_specs, out_specs=out_specs,
        compiler_params=_cp("parallel"), name="in_proj",
    )(x, g, mod4, mod4, *[_warg(w) for w in ws], *[_warg(p) for p in sgu_params])


def _s5_disc_kernel(are, aim, ldt, bre, bim, lam_re, lam_im, bbre, bbim):
    a_re, a_im = are[...], aim[...]
    dt = jnp.exp(ldt[...])
    mag = jnp.exp(a_re * dt)
    ang = a_im * dt
    ab_re = mag * jnp.cos(ang)
    ab_im = mag * jnp.sin(ang)
    den = a_re * a_re + a_im * a_im
    f_re = ((ab_re - 1.0) * a_re + ab_im * a_im) / den
    f_im = (ab_im * a_re - (ab_re - 1.0) * a_im) / den
    lam_re[...] = ab_re
    lam_im[...] = ab_im
    for c in range(S5_GROUP):
        bbre[c] = f_re * bre[c] - f_im * bim[c]
        bbim[c] = f_re * bim[c] + f_im * bre[c]


def s5_discretise(a_re, a_im, b_re, b_im, log_dt):
    shp = a_re.shape
    rows = int(np.prod(shp)) // LANE
    are = a_re.astype(F32).reshape(rows, LANE)
    aim = a_im.astype(F32).reshape(rows, LANE)
    ldt = jnp.broadcast_to(log_dt.astype(F32)[..., None], shp).reshape(rows, LANE)
    bre = jnp.moveaxis(b_re.astype(F32), -1, 0).reshape(S5_GROUP, rows, LANE)
    bim = jnp.moveaxis(b_im.astype(F32), -1, 0).reshape(S5_GROUP, rows, LANE)
    outs = pl.pallas_call(
        _s5_disc_kernel,
        out_shape=[jax.ShapeDtypeStruct((rows, LANE), F32)] * 2
        + [jax.ShapeDtypeStruct((S5_GROUP, rows, LANE), F32)] * 2,
        name="s5_discretise",
    )(are, aim, ldt, bre, bim)
    lam_re, lam_im = outs[0].reshape(shp), outs[1].reshape(shp)
    bb_re = jnp.moveaxis(outs[2].reshape((S5_GROUP,) + shp), 0, -1)
    bb_im = jnp.moveaxis(outs[3].reshape((S5_GROUP,) + shp), 0, -1)
    return lam_re, lam_im, bb_re, bb_im


def s5_pack(lam_re, lam_im, bb_re, bb_im, c_re, c_im):
    depth = lam_re.shape[0]
    gpb = LANE // S5_GROUP
    eye = jnp.eye(gpb, dtype=F32)

    def lam_blocks(v):
        return v.reshape(depth, 2, S5_JB, 1, S5_BW)

    lam = jnp.concatenate([lam_blocks(lam_re), lam_blocks(lam_im)], axis=-1)
    lam = jnp.broadcast_to(lam, (depth, 2, S5_JB, SUBLANE, 2 * S5_BW))

    def b_blocks(bb):
        v = bb.reshape(depth, 2, S5_JB, gpb, S5_STATE, S5_GROUP)
        return jnp.einsum('ldjgpc,gh->ldjgchp', v, eye).reshape(depth, 2, S5_JB, LANE, S5_BW)

    bblk = jnp.concatenate([b_blocks(bb_re), b_blocks(bb_im)], axis=-1).astype(BF16)

    def c_blocks(cc):
        v = cc.astype(F32).reshape(depth, 2, S5_JB, gpb, S5_GROUP, S5_STATE)
        return jnp.einsum('ldjgcp,gh->ldjhpgc', v, eye).reshape(depth, 2, S5_JB, S5_BW, LANE)

    cblk = jnp.concatenate([c_blocks(c_re), -c_blocks(c_im)], axis=-2).astype(BF16)
    return lam, bblk, cblk


def _cpow(re, im, n):
    out = None
    while n:
        if n & 1:
            out = (re, im) if out is None else (out[0] * re - out[1] * im, out[0] * im + out[1] * re)
        n >>= 1
        if n:
            re, im = re * re - im * im, 2.0 * re * im
    return out


def _s5_pass_kernel(*refs, tt, lseg, with_y, n_sub):
    if with_y:
        u_ref, bblk, lam, cblk, sloc, h0, y_ref, hfin, uperm, hst, yperm = refs[:11]
    else:
        u_ref, bblk, lam, s_out, uperm, hst = refs[:6]
    bus = refs[-n_sub * S5_JB:]
    d = pl.program_id(1)
    j = pl.program_id(2)
    offs = [pl.multiple_of((ph + d * (n_sub - 1 - 2 * ph)) * tt, tt) for ph in range(n_sub)]

    @pl.when(j == 0)
    def _():
        if not with_y:
            hst[...] = jnp.zeros(hst.shape, F32)
        else:
            for jb in range(S5_JB):
                pr, pi = _cpow(lam[jb, 0:1, :S5_BW], lam[jb, 0:1, S5_BW:], lseg)

                def chain(order, jb=jb, pr=pr, pi=pi):
                    cr, ci = h0[jb, :, :S5_BW], h0[jb, :, S5_BW:]
                    for s in order:
                        hst[jb, s:s + 1, :S5_BW] = cr
                        hst[jb, s:s + 1, S5_BW:] = ci
                        sr, si = sloc[jb, s:s + 1, :S5_BW], sloc[jb, s:s + 1, S5_BW:]
                        cr, ci = pr * cr - pi * ci + sr, pr * ci + pi * cr + si
                    hfin[jb, :, :S5_BW] = cr
                    hfin[jb, :, S5_BW:] = ci

                @pl.when(d == 0)
                def _():
                    chain(range(S5_NSEG))

                @pl.when(d == 1)
                def _():
                    chain(range(S5_NSEG - 1, -1, -1))

    def drive(ph):
        for s in range(S5_NSEG):
            us = u_ref[s, pl.ds(offs[ph], tt), :].astype(F32)
            for k in range(S5_JB):
                uperm[ph, k, pl.ds(s, tt, stride=S5_NSEG), :] = us[:, k * LANE:(k + 1) * LANE]
        for jb in range(S5_JB):
            bus[ph * S5_JB + jb][...] = jnp.dot(uperm[ph, jb].astype(BF16), bblk[jb], preferred_element_type=F32)

    def scan(ph):
        for jb in range(S5_JB):
            bu = bus[ph * S5_JB + jb]
            lr, li = lam[jb, :, :S5_BW], lam[jb, :, S5_BW:]

            def step(i, carry, lr=lr, li=li, bu=bu):
                hr, hi = carry
                t = i + d * (tt - 1 - 2 * i)
                r0 = pl.multiple_of(t * S5_NSEG, S5_NSEG)
                nr = lr * hr - li * hi + bu[pl.ds(r0, S5_NSEG), :S5_BW]
                ni = lr * hi + li * hr + bu[pl.ds(r0, S5_NSEG), S5_BW:]
                if with_y:
                    bu[pl.ds(r0, S5_NSEG), :S5_BW] = nr
                    bu[pl.ds(r0, S5_NSEG), S5_BW:] = ni
                return nr, ni

            hr, hi = lax.fori_loop(0, tt, step, (hst[jb, :, :S5_BW], hst[jb, :, S5_BW:]), unroll=True)
            hst[jb, :, :S5_BW] = hr
            hst[jb, :, S5_BW:] = hi

    def read(ph):
        for jb in range(S5_JB):
            yperm[ph, jb] = jnp.dot(bus[ph * S5_JB + jb][...].astype(BF16), cblk[jb],
                                    preferred_element_type=F32)
        for s in range(S5_NSEG):
            for k in range(S5_JB):
                y_ref[s, pl.ds(offs[ph], tt), k * LANE:(k + 1) * LANE] = (
                    yperm[ph, k, pl.ds(s, tt, stride=S5_NSEG), :].astype(y_ref.dtype))

    for ph in range(n_sub):
        drive(ph)
    for ph in range(n_sub):
        scan(ph)
        if with_y:
            read(ph)
    if not with_y:
        s_out[...] = hst[...]


def s5_scan(u, bblk, lam, cblk, l, bsz, h0):
    seq_len = u.shape[0] // bsz
    lseg = seq_len // S5_NSEG
    tt = min(lseg, 64)
    n_sub = next(n for n in (4, 2, 1) if (lseg // tt) % n == 0)
    nt = lseg // (tt * n_sub)
    u5 = u.reshape(bsz, S5_NSEG, lseg, BR_W)

    def tile(d, j):
        return j + d * (nt - 1 - 2 * j)

    u_spec = pl.BlockSpec((None, S5_NSEG, n_sub * tt, BR_W), lambda b, d, j: (b, 0, tile(d, j), 0))
    y_spec = pl.BlockSpec((None, None, S5_NSEG, n_sub * tt, BR_W), lambda b, d, j: (d, b, 0, tile(d, j), 0))
    w_b = pl.BlockSpec((None, None, S5_JB, LANE, 2 * S5_BW), lambda b, d, j: (l, d, 0, 0, 0))
    w_lam = pl.BlockSpec((None, None, S5_JB, SUBLANE, 2 * S5_BW), lambda b, d, j: (l, d, 0, 0, 0))
    w_c = pl.BlockSpec((None, None, S5_JB, 2 * S5_BW, LANE), lambda b, d, j: (l, d, 0, 0, 0))
    st8 = pl.BlockSpec((None, None, S5_JB, SUBLANE, 2 * S5_BW), lambda b, d, j: (b, d, 0, 0, 0))
    st1 = pl.BlockSpec((None, None, S5_JB, 1, 2 * S5_BW), lambda b, d, j: (b, d, 0, 0, 0))
    n_rows = S5_NSEG * tt
    scratch = [pltpu.VMEM((n_sub, S5_JB, n_rows, LANE), F32), pltpu.VMEM((S5_JB, SUBLANE, 2 * S5_BW), F32)]
    bus = [pltpu.VMEM((n_rows, 2 * S5_BW), F32)] * (n_sub * S5_JB)
    grid = (bsz, 2, nt)
    cp = _cp("arbitrary", "arbitrary", "arbitrary")

    sloc = pl.pallas_call(
        functools.partial(_s5_pass_kernel, tt=tt, lseg=lseg, with_y=False, n_sub=n_sub),
        out_shape=jax.ShapeDtypeStruct((bsz, 2, S5_JB, SUBLANE, 2 * S5_BW), F32),
        grid=grid, in_specs=[u_spec, w_b, w_lam], out_specs=st8,
        scratch_shapes=scratch + bus, compiler_params=cp, name="s5_local",
    )(u5, bblk, lam)

    y, hfin = pl.pallas_call(
        functools.partial(_s5_pass_kernel, tt=tt, lseg=lseg, with_y=True, n_sub=n_sub),
        out_shape=[jax.ShapeDtypeStruct((2, bsz, S5_NSEG, lseg, BR_W), BF16),
                   jax.ShapeDtypeStruct((bsz, 2, S5_JB, 1, 2 * S5_BW), F32)],
        grid=grid, in_specs=[u_spec, w_b, w_lam, w_c, st8, st1], out_specs=[y_spec, st1],
        scratch_shapes=scratch + [pltpu.VMEM((n_sub, S5_JB, n_rows, LANE), F32)] + bus,
        compiler_params=cp, name="s5_emit",
    )(u5, bblk, lam, cblk, sloc, h0)
    return y.reshape(2, bsz * seq_len, BR_W), hfin


def _ssd_prep_kernel(cur_ref, prev_ref, next_ref, dt_ref, w_ref, b_ref, dtb_ref, alane_ref, esel_ref, tri_ref,
                     xs_ref, bc_ref, ac_ref, act_ref, xt_ref, *, seq_len):
    i = pl.program_id(0)
    x = cur_ref[:, BR_W:].astype(F32)
    tm = x.shape[0]
    row = lax.broadcasted_iota(jnp.int32, (tm, 1), 0)
    pos = lax.rem(row + i * tm, seq_len)
    prev_row = prev_ref[SUBLANE - 1:SUBLANE, BR_W:].astype(F32)
    next_row = next_ref[0:1, BR_W:].astype(F32)
    x_prev = jnp.where(row == 0, prev_row, pltpu.roll(x, 1, axis=0))
    x_prev = jnp.where(pos == 0, 0.0, x_prev)
    x_next = jnp.where(row == tm - 1, next_row, pltpu.roll(x, tm - 1, axis=0))
    x_next = jnp.where(pos == seq_len - 1, 0.0, x_next)
    y = _silu(w_ref[0:1, :] * x_prev + w_ref[1:2, :] * x + w_ref[2:3, :] * x_next + b_ref[...])
    xs = y[:, :BR_W]
    xs_ref[...] = xs
    bc_ref[...] = y[:, BR_W:].astype(bc_ref.dtype)
    t = dt_ref[...] + dtb_ref[...]
    sp = jnp.maximum(t, 0.0) + jnp.log1p(jnp.exp(-jnp.abs(t)))
    lane = lax.broadcasted_iota(jnp.int32, t.shape, 1)
    d_a = sp * alane_ref[...]
    d_a = (jnp.where(lane < SSD_HEADS, d_a, 0.0),
           jnp.where(lane < SSD_HEADS, pltpu.roll(d_a, LANE - SSD_HEADS, axis=1), 0.0))
    tri_bf = [tri_ref[d].astype(BF16) for d in range(2)]
    for d in range(2):
        for c in range(tm // SSD_T):
            rows = slice(c * SSD_T, (c + 1) * SSD_T)
            rem, a_cum = d_a[d][rows], None
            for _ in range(3):
                piece = rem.astype(BF16)
                part = jnp.dot(tri_bf[d], piece, preferred_element_type=F32)
                a_cum = part if a_cum is None else a_cum + part
                rem = rem - piece.astype(F32)
            ac_ref[d, rows, :] = a_cum
            act_ref[d, c] = a_cum.T[:SUBLANE, :]
    sp_hi = sp.astype(BF16)
    sp_lo = (sp - sp_hi.astype(F32)).astype(BF16)
    for d in range(2):
        dt_x = (jnp.dot(sp_hi, esel_ref[d], preferred_element_type=F32)
                + jnp.dot(sp_lo, esel_ref[d], preferred_element_type=F32))
        xt_ref[d] = (xs * dt_x).astype(xt_ref.dtype)


def ssd_prep(p_ssd, dt_raw, seq_len, conv_w, conv_b, dt_bias, a_lane, esel, tri):
    r, w = p_ssd.shape
    nb = r // SUBLANE
    per = TM // SUBLANE
    cpt = TM // SSD_T
    return pl.pallas_call(
        functools.partial(_ssd_prep_kernel, seq_len=seq_len),
        out_shape=[jax.ShapeDtypeStruct((r, BR_W), F32), jax.ShapeDtypeStruct((r, 2 * SSD_GN), BF16),
                   jax.ShapeDtypeStruct((2, r, LANE), F32),
                   jax.ShapeDtypeStruct((2, r // SSD_T, SUBLANE, SSD_T), F32),
                   jax.ShapeDtypeStruct((2, r, BR_W), BF16)],
        grid=(r // TM,),
        in_specs=[pl.BlockSpec((TM, w), lambda i: (i, 0)),
                  pl.BlockSpec((SUBLANE, w), lambda i: (jnp.maximum(i * per - 1, 0), 0)),
                  pl.BlockSpec((SUBLANE, w), lambda i: (jnp.minimum((i + 1) * per, nb - 1), 0)),
                  pl.BlockSpec((TM, LANE), lambda i: (i, 0)),
                  _full(conv_w.shape), _full(conv_b.shape), _full(dt_bias.shape), _full(a_lane.shape),
                  _full(esel.shape), _full(tri.shape)],
        out_specs=[pl.BlockSpec((TM, BR_W), lambda i: (i, 0)), pl.BlockSpec((TM, 2 * SSD_GN), lambda i: (i, 0)),
                   pl.BlockSpec((2, TM, LANE), lambda i: (0, i, 0)),
                   pl.BlockSpec((2, cpt, SUBLANE, SSD_T), lambda i: (0, i, 0, 0)),
                   pl.BlockSpec((2, TM, BR_W), lambda i: (0, i, 0))],
        compiler_params=_cp("parallel"), name="ssd_prep",
    )(p_ssd, p_ssd, p_ssd, dt_raw, conv_w, conv_b, dt_bias, a_lane, esel, tri)


def _ssd_scan_kernel(xt_ref, bc_ref, ac_ref, act_ref, tri_ref, h0_ref, y_ref, hfin_ref, st_ref, *, n_chunk, bsz):
    d = pl.program_id(0)
    j = pl.program_id(1)
    T = SSD_T

    @pl.when(j == 0)
    def _():
        st_ref[...] = h0_ref[...]

    tri = tri_ref[...]
    mask = tri > 0.5
    rep = SSD_HEADS // SSD_GROUPS
    low_lane = lax.broadcasted_iota(jnp.int32, (1, LANE), 1) < SSD_HEAD_DIM
    group_row = [(lax.broadcasted_iota(jnp.int32, (SSD_GN, 1), 0) // SSD_STATE) == g for g in range(SSD_GROUPS)]

    def chunk(ci, carry):
        c = ci + d * (n_chunk - 1 - 2 * ci)
        r0 = pl.multiple_of(c * T, T)
        for b in range(bsz):
            a_cum = ac_ref[b, pl.ds(r0, T), :]
            a_cum_t = act_ref[b, c]
            total = jnp.where(d == 0, a_cum[T - 1:T, :], a_cum[0:1, :])
            bc = bc_ref[b, pl.ds(r0, T), :]
            bm_t = bc[:, :SSD_GN].astype(F32).T
            cm = bc[:, SSD_GN:]
            b_grp = [jnp.where(group_row[g], bm_t, 0.0) for g in range(SSD_GROUPS)]
            scores = [jnp.dot(cm, b_grp[g].astype(BF16), preferred_element_type=F32)
                      for g in range(SSD_GROUPS)]
            for k in range(SSD_HEADS // 2):
                x_pair = xt_ref[b, pl.ds(r0, T), k * LANE:(k + 1) * LANE]
                y_d, s_n, e_col, e_tot = [], [], [], []
                for h in (2 * k, 2 * k + 1):
                    g = h // rep
                    col = a_cum[:, h:h + 1]
                    rowv = a_cum_t[h:h + 1, :]
                    tot = total[:, h:h + 1]
                    decay = jnp.exp(jnp.where(mask, col - rowv, -1e30))
                    y_d.append(jnp.dot((scores[g] * decay).astype(BF16), x_pair, preferred_element_type=F32))
                    bw = (b_grp[g] * jnp.exp(tot - rowv)).astype(BF16)
                    s_n.append(jnp.dot(bw, x_pair, preferred_element_type=F32))
                    e_col.append(jnp.exp(col))
                    e_tot.append(jnp.exp(tot))
                s_old = st_ref[b, k]
                y_off = jnp.dot(cm, s_old.astype(BF16), preferred_element_type=F32)
                y = (jnp.where(low_lane, y_d[0], y_d[1])
                     + y_off * jnp.where(low_lane, e_col[0], e_col[1]))
                y_ref[b, pl.ds(r0, T), k * LANE:(k + 1) * LANE] = y.astype(y_ref.dtype)
                st_ref[b, k] = (s_old * jnp.where(low_lane, e_tot[0], e_tot[1])
                                + jnp.where(low_lane, s_n[0], s_n[1]))
        return carry

    lax.fori_loop(0, n_chunk, chunk, 0, unroll=True)
    hfin_ref[...] = st_ref[...]


def ssd_scan(xt, bc, a_cum, a_cum_t, tri, bsz, h0):
    r = bc.shape[0]
    seq_len = r // bsz
    ts = min(seq_len, TM)
    nt = seq_len // ts

    def tile(d, j):
        return j + d * (nt - 1 - 2 * j)

    st_shape = (bsz, SSD_HEADS // 2, SSD_GN, LANE)
    st = pl.BlockSpec((None,) + st_shape, lambda d, j: (d, 0, 0, 0, 0))
    y, hfin = pl.pallas_call(
        functools.partial(_ssd_scan_kernel, n_chunk=ts // SSD_T, bsz=bsz),
        out_shape=[jax.ShapeDtypeStruct((2, bsz, seq_len, BR_W), BF16),
                   jax.ShapeDtypeStruct((2,) + st_shape, F32)],
        grid=(2, nt),
        in_specs=[pl.BlockSpec((None, bsz, ts, BR_W), lambda d, j: (d, 0, tile(d, j), 0)),
                  pl.BlockSpec((bsz, ts, 2 * SSD_GN), lambda d, j: (0, tile(d, j), 0)),
                  pl.BlockSpec((None, bsz, ts, LANE), lambda d, j: (d, 0, tile(d, j), 0)),
                  pl.BlockSpec((None, bsz, ts // SSD_T, SUBLANE, SSD_T), lambda d, j: (d, 0, tile(d, j), 0, 0)),
                  pl.BlockSpec((None, SSD_T, SSD_T), lambda d, j: (d, 0, 0)), st],
        out_specs=[pl.BlockSpec((None, bsz, ts, BR_W), lambda d, j: (d, 0, tile(d, j), 0)), st],
        scratch_shapes=[pltpu.VMEM(st_shape, F32)],
        compiler_params=_cp("arbitrary", "arbitrary"), name="ssd_scan",
    )(xt.reshape(2, bsz, seq_len, BR_W), bc.reshape(bsz, seq_len, 2 * SSD_GN),
      a_cum.reshape(2, bsz, seq_len, LANE), a_cum_t.reshape(2, bsz, seq_len // SSD_T, SUBLANE, SSD_T), tri, h0)
    return y.reshape(2, r, BR_W), hfin


def _rope_table_kernel(frow_ref, fcol_ref, cos_ref, sin_ref):
    tr = cos_ref.shape[0]
    pos = (lax.broadcasted_iota(jnp.int32, (tr, 1), 0) + pl.program_id(0) * tr).astype(F32)
    pos_row = jnp.floor((pos + 0.5) * (1.0 / GRID_W))
    pos_col = pos - pos_row * GRID_W
    ang = pos_row * frow_ref[...] + pos_col * fcol_ref[...]
    lane = lax.broadcasted_iota(jnp.int32, ang.shape, 1)
    quarter = MLA_ROPE // 2
    first = (lane >= MLA_NOPE) & (lane < MLA_NOPE + quarter)
    second = (lane >= MLA_NOPE + quarter) & (lane < MLA_QK)
    c, s = jnp.cos(ang), jnp.sin(ang)
    cos_ref[...] = jnp.where(first | second, c, jnp.where(lane < MLA_NOPE, 1.0, 0.0))
    sin_ref[...] = jnp.where(first, -s, jnp.where(second, s, 0.0))


def _rope_partner(v):
    quarter = MLA_ROPE // 2
    src = np.arange(LANE)
    src[MLA_NOPE:MLA_NOPE + quarter] += quarter
    src[MLA_NOPE + quarter:MLA_QK] -= quarter
    valid = (np.arange(LANE) >= MLA_NOPE) & (np.arange(LANE) < MLA_QK)
    return jnp.where(valid, v[..., src], 0)


def rope_tables(seq_len):
    pairs = MLA_ROPE // 4
    inv_freq = ROPE_BASE ** (-jnp.arange(pairs, dtype=F32) / pairs)
    zeros = lambda n: jnp.zeros((n,), F32)
    half = jnp.concatenate([inv_freq, zeros(pairs)])
    f_row = jnp.concatenate([zeros(MLA_NOPE), half, half, zeros(LANE - MLA_QK)]).reshape(1, LANE)
    half = jnp.concatenate([zeros(pairs), inv_freq])
    f_col = jnp.concatenate([zeros(MLA_NOPE), half, half, zeros(LANE - MLA_QK)]).reshape(1, LANE)
    tr = min(seq_len, 1024)
    spec = pl.BlockSpec((tr, LANE), lambda i: (i, 0))
    return pl.pallas_call(
        _rope_table_kernel, out_shape=[jax.ShapeDtypeStruct((seq_len, LANE), F32)] * 2,
        grid=(seq_len // tr,), in_specs=[_full((1, LANE))] * 2, out_specs=[spec] * 2,
        compiler_params=_cp("parallel"), name="rope_tables",
    )(f_row, f_col)


def _mla_prep_kernel(*refs, rope):
    if rope:
        (p_ref, gq_ref, gkv_ref, wq_ref, wkv_ref, nq_ref, nk_ref, wqs_ref, nqs_ref, nks_ref, cos_ref, sin_ref,
         q_ref, k_ref, v_ref) = refs
    else:
        p_ref, gq_ref, gkv_ref, wq_ref, wkv_ref, nq_ref, nk_ref, q_ref, k_ref, v_ref = refs
    p = p_ref[...].astype(F32)
    cq = (_rms(p[:, :MLA_Q_LORA], MLA_Q_LORA) * gq_ref[...]).astype(BF16)
    ckv = (_rms(p[:, MLA_Q_LORA:MLA_Q_LORA + MLA_KV_LORA], MLA_KV_LORA) * gkv_ref[...]).astype(BF16)
    kr = p[:, MLA_Q_LORA + MLA_KV_LORA:MLA_Q_LORA + MLA_KV_LORA + LANE]
    q_all = jnp.dot(cq, wq_ref[...], preferred_element_type=F32)
    kv_all = jnp.dot(ckv, wkv_ref[...], preferred_element_type=F32)
    lane = lax.broadcasted_iota(jnp.int32, (1, LANE), 1)
    one_col = jnp.where(lane == MLA_V, 1.0, 0.0)
    scale = MLA_QK ** -0.5 * LOG2E
    if rope:
        qs_all = jnp.dot(cq, wqs_ref[...], preferred_element_type=F32)
        q_cos, q_sin = nq_ref[...] * cos_ref[...], nqs_ref[...] * sin_ref[...]
        k_cos = nk_ref[...] * cos_ref[...]
        k_part = p[:, MLA_Q_LORA + MLA_KV_LORA + LANE:] * (nks_ref[...] * sin_ref[...])

    def inv_rms(t):
        return lax.rsqrt(jnp.sum(t * t, axis=-1, keepdims=True) * (1.0 / MLA_QK) + 1e-6)

    for h in range(MLA_HEADS):
        q = q_all[:, h * LANE:(h + 1) * LANE]
        k = kv_all[:, 2 * h * LANE:(2 * h + 1) * LANE] + kr
        if rope:
            q_out = (q * q_cos + qs_all[:, h * LANE:(h + 1) * LANE] * q_sin) * (inv_rms(q) * scale)
            k_out = (k * k_cos + k_part) * inv_rms(k)
        else:
            q_out = q * nq_ref[...] * (inv_rms(q) * scale)
            k_out = k * nk_ref[...] * inv_rms(k)
        q_ref[h] = q_out.astype(q_ref.dtype)
        k_ref[h] = k_out.astype(k_ref.dtype)
        v_ref[h] = (kv_all[:, (2 * h + 1) * LANE:(2 * h + 2) * LANE] + one_col).astype(v_ref.dtype)


def mla_prep(p_mla, bsz, gq, gkv, wq, wkv, nq, nk, rope_args):
    r, w = p_mla.shape
    seq_len = r // bsz
    tr = min(seq_len, TM)
    nt = seq_len // tr
    rope = rope_args is not None
    in_specs = [pl.BlockSpec((tr, w), lambda b, i: (b * nt + i, 0)), _full(gq.shape), _full(gkv.shape),
                _wspec(wq), _wspec(wkv), _full(nq.shape), _full(nk.shape)]
    args = [p_mla, gq, gkv, _warg(wq), _warg(wkv), nq, nk]
    if rope:
        in_specs += [_wspec(a) for a in rope_args[:3]]
        in_specs += [pl.BlockSpec((tr, LANE), lambda b, i: (i, 0))] * 2
        args += [_warg(a) for a in rope_args]
    head = pl.BlockSpec((None, MLA_HEADS, tr, LANE), lambda b, i: (b, 0, i, 0))
    return pl.pallas_call(
        functools.partial(_mla_prep_kernel, rope=rope),
        out_shape=[jax.ShapeDtypeStruct((bsz, MLA_HEADS, seq_len, LANE), BF16)] * 3,
        grid=(bsz, nt), in_specs=in_specs, out_specs=[head] * 3,
        compiler_params=_cp("parallel", "parallel"), name="mla_prep",
    )(*args)


def _attn_kernel(*refs, n_kv):
    bound_ref, q_ref = refs[0], refs[1]
    kv_refs = refs[2:2 + 2 * n_kv]
    o_ref = refs[2 + 2 * n_kv]
    m_ref = refs[3 + 2 * n_kv]
    scores = [[lax.dot_general(q_ref[hh], kv_refs[2 * i][hh], (((1,), (1,)), ((), ())),
                               preferred_element_type=F32) for i in range(n_kv)] for hh in range(2)]

    def row_max(first_set):
        for hh in range(2):
            m = scores[hh][n_kv - 1].max(axis=-1, keepdims=True)
            for si in scores[hh][first_set:n_kv - 1]:
                m = jnp.maximum(m, si.max(axis=-1, keepdims=True))
            m_ref[hh] = m

    if n_kv > 1:
        fast = bound_ref[0] <= ATTN_FAST_BOUND

        @pl.when(fast)
        def _():
            row_max(n_kv - 1)

        @pl.when(jnp.logical_not(fast))
        def _():
            row_max(0)
    else:
        row_max(0)

    for hh in range(2):
        m = m_ref[hh]
        acc = None
        for i, si in enumerate(scores[hh]):
            pv = jnp.dot(jnp.exp2((si - m).astype(BF16)), kv_refs[2 * i + 1][hh], preferred_element_type=F32)
            acc = pv if acc is None else acc + pv
        o = acc[:, :MLA_V] / acc[:, MLA_V:MLA_V + 1]
        o_ref[:, hh * MLA_V:(hh + 1) * MLA_V] = o.astype(o_ref.dtype)


def attention(q, kvs, score_bound):
    bsz, nh, lq, _ = q.shape
    tq = min(lq, TQ)
    nq = lq // tq
    in_specs = [pl.BlockSpec(memory_space=pltpu.SMEM),
                pl.BlockSpec((None, 2, tq, LANE), lambda b, hp, i: (b, hp, i, 0))]
    args = [score_bound, q]
    for k, v in kvs:
        spec = pl.BlockSpec((None, 2, k.shape[2], LANE), lambda b, hp, i: (b, hp, 0, 0))
        in_specs += [spec, spec]
        args += [k, v]
    return pl.pallas_call(
        functools.partial(_attn_kernel, n_kv=len(kvs)),
        out_shape=jax.ShapeDtypeStruct((bsz * lq, nh * MLA_V), BF16),
        grid=(bsz, nh // 2, nq), in_specs=in_specs,
        out_specs=pl.BlockSpec((tq, 2 * MLA_V), lambda b, hp, i: (b * nq + i, hp)),
        scratch_shapes=[pltpu.VMEM((2, tq, 1), F32)],
        compiler_params=_cp("parallel", "parallel", "arbitrary"), name="attention",
    )(*args)


def _merge_kernel(x_ref, h_ref, u_ref, s5f_ref, s5b_ref, s5d_ref, wglu_ref, b_ref,
                  xs_ref, ssdf_ref, ssdb_ref, z_ref, ssdd_ref, ssdg_ref, d_ref,
                  g1_ref, wg_ref, wb_ref, wo_ref,
                  n2_ref, sh2_ref, sc2_ref, g2_ref, wi_ref, wfo_ref, o_ref, *, n_split):
    y = s5d_ref[...] * u_ref[...].astype(F32) + s5f_ref[...].astype(F32) + s5b_ref[...].astype(F32)
    g = jax.nn.gelu(y)
    a = (g * _sigmoid(jnp.dot(g.astype(BF16), wglu_ref[...], preferred_element_type=F32))).astype(BF16)
    y = ssdd_ref[...] * xs_ref[...] + ssdf_ref[...].astype(F32) + ssdb_ref[...].astype(F32)
    c = (_rms(y * _silu(z_ref[...].astype(F32)), BR_W) * ssdg_ref[...]).astype(BF16)
    h = h_ref[...]
    dm = x_ref.shape[-1]
    merged = None
    for i, br in enumerate((a, b_ref[...], c, d_ref[...])):
        gate = _sigmoid(jnp.dot(h, wg_ref[:, i * dm:(i + 1) * dm], preferred_element_type=F32))
        term = gate * jnp.dot(br, wb_ref[i], preferred_element_type=F32)
        merged = term if merged is None else merged + term
    mix = jnp.dot(merged.astype(BF16), wo_ref[...], preferred_element_type=F32)
    x = x_ref[...] + g1_ref[...] * mix

    y = _rms(x, dm) * n2_ref[...]
    hb = (y * (1.0 + sc2_ref[...]) + sh2_ref[...]).astype(BF16)
    hid = wfo_ref.shape[0]
    step = hid // n_split
    acc = None
    for c in range(n_split):
        gate = jnp.dot(hb, wi_ref[:, c * step:(c + 1) * step], preferred_element_type=F32)
        up = jnp.dot(hb, wi_ref[:, hid + c * step:hid + (c + 1) * step], preferred_element_type=F32)
        part = jnp.dot((_silu(gate) * up).astype(BF16), wfo_ref[c * step:(c + 1) * step, :],
                       preferred_element_type=F32)
        acc = part if acc is None else acc + part
    o_ref[...] = x + g2_ref[...] * acc


def merge_ffn(x, h, s5_in, b, ssd_in, d, mod4, l, midx, wg, wb, wo, norm2, wi, wfo):
    r, dm = x.shape
    n_split = 11 if wfo.shape[0] % (11 * LANE) == 0 else 1
    row = pl.BlockSpec((TM, dm), lambda i: (i, 0))
    br = pl.BlockSpec((TM, BR_W), lambda i: (i, 0))
    fwd = pl.BlockSpec((None, TM, BR_W), lambda i: (0, i, 0))
    bwd = pl.BlockSpec((None, TM, BR_W), lambda i: (1, i, 0))
    vec = _full((1, BR_W))
    u, y_s5, s5_d, w_glu = s5_in
    xs, y_ssd, p_ssd, ssd_d, ssd_g = ssd_in
    return pl.pallas_call(
        functools.partial(_merge_kernel, n_split=n_split),
        out_shape=jax.ShapeDtypeStruct((r, dm), F32), grid=(r // TM,),
        in_specs=[row, row, br, fwd, bwd, vec, _wspec(w_glu), br,
                  br, fwd, bwd, br, vec, vec, br,
                  _mod_spec(l, midx, 2, dm), _wspec(wg), _wspec(wb), _wspec(wo),
                  _full((1, dm)), _mod_spec(l, midx, 3, dm), _mod_spec(l, midx, 4, dm),
                  _mod_spec(l, midx, 5, dm), _wspec(wi), _wspec(wfo)],
        out_specs=row, compiler_params=_cp("parallel"), name="merge_ffn",
    )(x, h, u, y_s5, y_s5, s5_d, _warg(w_glu), b, xs, y_ssd, y_ssd, p_ssd, ssd_d, ssd_g, d, mod4,
      _warg(wg), _warg(wb), _warg(wo), norm2, mod4, mod4, mod4, _warg(wi), _warg(wfo))


def _pad_cols(w, n):
    return jnp.pad(w, ((0, 0),) * (w.ndim - 1) + ((0, n - w.shape[-1]),))


def kernel(x, c, ctx, c_ctx, w_ada, b_ada, norm1_g, norm2_g, w_in, s5_a_re, s5_a_im, s5_b_re, s5_b_im, s5_c_re, s5_c_im, s5_log_dt, s5_d, s5_w_glu, sgu_ln_g, sgu_ln_b, sgu_w_s, sgu_b_s, ssd_conv_w, ssd_conv_b, ssd_a_log, ssd_dt_bias, ssd_d, ssd_norm_g, mla_q_a_norm, mla_w_uq, mla_kv_a_norm, mla_w_ukv, mla_q_norm, mla_k_norm, w_branch, w_out, w_ffn_in, w_ffn_out):
    bsz, seq, dm = x.shape
    lc = ctx.shape[1]
    depth = w_ada.shape[0]
    assert seq % TM == 0 and (bsz * lc) % TM == 0 and seq % (S5_NSEG * SUBLANE) == 0
    assert lc % SSD_T == 0 and lc % (S5_NSEG * SUBLANE) == 0 and bsz + 1 <= 8

    cc8 = jnp.zeros((8, dm), F32).at[:bsz].set(c.astype(F32)).at[bsz].set(c_ctx.astype(F32))
    mod4 = ada_table(cc8, w_ada.astype(F32), b_ada.astype(F32)).reshape(depth, 8, 1, 6 * dm)
    lat_tiles = seq // TM
    midx_lat = lambda i: i // lat_tiles
    midx_ctx = lambda i: bsz

    off = np.cumsum([0, BR_W, 2 * BR_W, BR_W + SSD_CONV_CH + 2 * SSD_HEADS,
                     MLA_Q_LORA + MLA_KV_LORA + MLA_ROPE, N_BRANCH * dm])
    w_s5 = w_in[:, :, off[0]:off[1]].astype(BF16)
    w_sgu = w_in[:, :, off[1]:off[2]].astype(BF16)
    w_ssd = w_in[:, :, off[2]:off[2] + BR_W + SSD_CONV_CH].astype(BF16)
    w_dt = _pad_cols(w_in[:, :, off[2] + BR_W + SSD_CONV_CH:off[3]], LANE).astype(BF16)
    w_mla_main = w_in[:, :, off[3]:off[3] + MLA_Q_LORA + MLA_KV_LORA]
    w_kr = w_in[:, :, off[3] + MLA_Q_LORA + MLA_KV_LORA:off[4]]
    zeros = lambda n: jnp.zeros((depth, dm, n), w_in.dtype)
    kr_block = jnp.concatenate([zeros(MLA_NOPE), w_kr, zeros(LANE - MLA_QK)], axis=-1)
    w_mla = jnp.concatenate([w_mla_main, kr_block, _rope_partner(kr_block)], axis=-1).astype(BF16)
    w_gate = w_in[:, :, off[4]:off[5]].astype(BF16)

    lam_re, lam_im, bb_re, bb_im = s5_discretise(s5_a_re, s5_a_im, s5_b_re, s5_b_im, s5_log_dt)
    lam, bblk, cblk = s5_pack(lam_re, lam_im, bb_re, bb_im, s5_c_re, s5_c_im)
    s5_wg = s5_w_glu.astype(BF16)

    sgu_w = sgu_w_s.reshape(depth, SGU_HEADS // 2, 2, SGU_CHUNK, SGU_CHUNK)
    sgu_w = jnp.concatenate([sgu_w[:, :, 0], sgu_w[:, :, 1]], axis=-1).astype(BF16)
    sgu_b = jnp.repeat(jnp.swapaxes(sgu_b_s, 1, 2), BR_W // SGU_HEADS, axis=-1).astype(F32)

    a_neg = -jnp.exp(ssd_a_log.astype(F32))
    a_lane = _pad_cols(a_neg.reshape(depth, 1, 2 * SSD_HEADS), LANE)
    head_of_lane = jnp.arange(LANE)[None, :, None] - SSD_HEADS * jnp.arange(2)[:, None, None]
    col_head = jnp.arange(BR_W) // SSD_HEAD_DIM
    esel = ((head_of_lane == col_head[None, None, :]) & (head_of_lane >= 0)).astype(BF16)
    ti = jnp.arange(SSD_T)
    tri = jnp.stack([ti[None, :] <= ti[:, None], ti[None, :] >= ti[:, None]]).astype(F32)
    dt_bias = _pad_cols(ssd_dt_bias.astype(F32).reshape(depth, 1, 2 * SSD_HEADS), LANE)
    ssd_dskip = jnp.repeat(ssd_d.astype(F32), SSD_HEAD_DIM, axis=-1)[:, None, :]

    wq = mla_w_uq.reshape(depth, MLA_Q_LORA, MLA_HEADS, MLA_QK)
    wq = _pad_cols(wq, LANE)
    wq_sw = _rope_partner(wq).reshape(depth, MLA_Q_LORA, MLA_HEADS * LANE).astype(BF16)
    wq = wq.reshape(depth, MLA_Q_LORA, MLA_HEADS * LANE).astype(BF16)
    wkv = mla_w_ukv.reshape(depth, MLA_KV_LORA, MLA_HEADS, 2, MLA_NOPE)
    wkv = _pad_cols(wkv, LANE).reshape(depth, MLA_KV_LORA, MLA_HEADS * 2 * LANE).astype(BF16)
    nq = _pad_cols(mla_q_norm.astype(F32), LANE)[:, None, :]
    nk = _pad_cols(mla_k_norm.astype(F32), LANE)[:, None, :]
    nq_sw, nk_sw = _rope_partner(nq), _rope_partner(nk)
    rope_cos, rope_sin = rope_tables(seq)
    score_bound = (MLA_QK ** 0.5 * LOG2E * jnp.max(jnp.abs(mla_q_norm.astype(F32)), axis=-1)
                   * jnp.max(jnp.abs(mla_k_norm.astype(F32)), axis=-1))

    wb = w_branch.astype(BF16)
    wo = w_out.astype(BF16)
    wfi = w_ffn_in.astype(BF16)
    wfo = w_ffn_out.astype(BF16)

    row = lambda v, l: v[l].astype(F32).reshape(1, -1)
    x_lat = x.astype(F32).reshape(bsz * seq, dm)
    x_ctx = ctx.astype(F32).reshape(bsz * lc, dm)
    s5_zero = jnp.zeros((bsz, 2, S5_JB, 1, 2 * S5_BW), F32)
    ssd_zero = jnp.zeros((2, bsz, SSD_HEADS // 2, SSD_GN, LANE), F32)

    for l in range(depth):
        need_ctx = l < depth - 1
        lay = lambda w: _Layer(w, l)
        ws = tuple(lay(w) for w in (w_s5, w_sgu, w_ssd, w_dt, w_mla))
        w_merge = (lay(w_gate), lay(wb), lay(wo))
        w_ffn = (lay(wfi), lay(wfo))
        dts = (BF16, BF16, BF16, F32, BF16)
        g1 = row(norm1_g, l)
        sgu_p = (row(sgu_ln_g, l), row(sgu_ln_b, l), lay(sgu_w), lay(sgu_b))
        h_l, u_l, b_l, p_l, dtr_l, m_l = in_proj(x_lat, g1, mod4, l, midx_lat, ws, dts, sgu_p)
        h_c, u_c, b_c, p_c, dtr_c, m_c = in_proj(x_ctx, g1, mod4, l, midx_ctx, ws, dts, sgu_p)

        y_c, s5_h = s5_scan(u_c, bblk, lam, cblk, l, bsz, s5_zero)
        y_l, _ = s5_scan(u_l, bblk, lam, cblk, l, bsz, s5_h)
        s5_tail = (row(s5_d, l), lay(s5_wg))
        conv_w, conv_b = ssd_conv_w[l].astype(F32), row(ssd_conv_b, l)
        ssd_w = (conv_w, conv_b, dt_bias[l], a_lane[l], esel, tri)
        xs_c, bc_c, ac_c, act_c, xt_c = ssd_prep(p_c, dtr_c, lc, *ssd_w)
        xs_l, bc_l, ac_l, act_l, xt_l = ssd_prep(p_l, dtr_l, seq, *ssd_w)
        yc_c, ssd_h = ssd_scan(xt_c, bc_c, ac_c, act_c, tri, bsz, ssd_zero)
        yc_l, _ = ssd_scan(xt_l, bc_l, ac_l, act_l, tri, bsz, ssd_h)
        ssd_tail = (ssd_dskip[l], row(ssd_norm_g, l))
        mla_w = (row(mla_q_a_norm, l), row(mla_kv_a_norm, l), lay(wq), lay(wkv), nq[l], nk[l])
        q_c, k_c, v_c = mla_prep(m_c, bsz, *mla_w, None)
        q_l, k_l, v_l = mla_prep(m_l, bsz, *mla_w, (lay(wq_sw), nq_sw[l], nk_sw[l], rope_cos, rope_sin))
        d_l = attention(q_l, [(k_l, v_l), (k_c, v_c)], score_bound[l:l + 1])

        x_lat = merge_ffn(x_lat, h_l, (u_l, y_l) + s5_tail, b_l, (xs_l, yc_l, p_l) + ssd_tail, d_l,
                          mod4, l, midx_lat, *w_merge, row(norm2_g, l), *w_ffn)
        if need_ctx:
            d_c = attention(q_c, [(k_c, v_c)], score_bound[l:l + 1])
            x_ctx = merge_ffn(x_ctx, h_c, (u_c, y_c) + s5_tail, b_c, (xs_c, yc_c, p_c) + ssd_tail, d_c,
                              mod4, l, midx_ctx, *w_merge, row(norm2_g, l), *w_ffn)
    return x_lat.reshape(bsz, seq, dm).astype(x.dtype)
```
